```python
import jax, jax.numpy as jnp
from jax import lax
import numpy as np

D_MODEL = 1024
BATCH = 32
SEQ = 2048
DEPTH = 4

CHUNK = 64
D_RWKV = D_MODEL // 2
RWKV_HEAD = 64
RWKV_HEADS = D_RWKV // RWKV_HEAD
LORA_W = 64
LORA_A = 64
LORA_V = 32
LORA_G = 160
RWKV_GN_EPS = 64e-5
D_MLSTM = D_MODEL // 2
MLSTM_HEADS = 4
MLSTM_HEAD = D_MLSTM // MLSTM_HEADS
CONV_W = 4
MLSTM_NORM_EPS = 1e-5
D_FF = ((8 * D_MODEL // 3 + 255) // 256) * 256
NORM_EPS = 1e-6
HALF = 0.5

RW_SPLITS = (D_RWKV, D_RWKV, D_RWKV, LORA_W, LORA_A, LORA_G)
ML_SPLITS = (D_MLSTM, D_MLSTM, D_MLSTM, D_MLSTM, MLSTM_HEADS, MLSTM_HEADS)
N_RW_IN = 3 * D_RWKV + LORA_W + LORA_A + LORA_G
N_ML_IN = 4 * D_MLSTM + 2 * MLSTM_HEADS
N_GATE = 2 * D_MODEL
N_IN = N_RW_IN + N_ML_IN + N_GATE

kernel_name = "hybrid_rwkv7_mlstm_macaron"


def _rmsnorm(x, g):
    xf = x.astype(jnp.float32)
    y = xf * lax.rsqrt(jnp.mean(xf * xf, axis=-1, keepdims=True) + NORM_EPS)
    return (y * g.astype(jnp.float32)).astype(x.dtype)


def _swiglu(x, w_in, w_out):
    gate, up = jnp.split(x @ w_in, 2, axis=-1)
    return (jax.nn.silu(gate) * up) @ w_out


def _time_shift(z):
    return jnp.pad(z, ((0, 0), (1, 0), (0, 0)))[:, :-1]


def _causal_dwconv(z, w, b):
    K, T = w.shape[0], z.shape[1]
    zp = jnp.pad(z, ((0, 0), (K - 1, 0), (0, 0)))
    out = b
    for j in range(K):
        out = out + w[j] * zp[:, j:j + T]
    return out


def _split_cols(z, sizes):
    out, start = [], 0
    for s in sizes:
        out.append(z[..., start:start + s])
        start += s
    return out


def _head_standardize(y, eps):
    yf = y.astype(jnp.float32)
    mu = jnp.mean(yf, axis=-1, keepdims=True)
    var = jnp.mean(jnp.square(yf - mu), axis=-1, keepdims=True)
    yn = (yf - mu) * lax.rsqrt(var + eps)
    return yn.reshape(y.shape[0], y.shape[1], -1)


def _rwkv7_recurrence(r, w, k, v, kk, b):
    Bn, T, H, N = r.shape
    xs = tuple(jnp.moveaxis(t.astype(jnp.float32), 1, 0) for t in (r, w, k, v, kk, b))

    def step(S, inp):
        r_t, w_t, k_t, v_t, kk_t, b_t = inp
        sa = jnp.einsum('bhvk,bhk->bhv', S, kk_t)
        S = (S * w_t[:, :, None, :]
             - sa[..., :, None] * b_t[:, :, None, :]
             + v_t[..., :, None] * k_t[:, :, None, :])
        return S, jnp.einsum('bhvk,bhk->bhv', S, r_t)

    S0 = jnp.zeros((Bn, H, N, N), jnp.float32)
    _, ys = lax.scan(step, S0, xs)
    return jnp.moveaxis(ys, 0, 1)


def _rwkv7_mixer(r, k, v, xw, xa, xg, w0, w_up, a0, a_up, g_up, k_k, k_a, r_k, gn_w, gn_b):
    Bn, T, _ = r.shape
    heads = lambda t: t.reshape(Bn, T, RWKV_HEADS, RWKV_HEAD)
    w_log = -jax.nn.softplus(-(w0 + jnp.tanh(xw) @ w_up)) - 0.5
    decay = jnp.exp(-jnp.exp(w_log.astype(jnp.float32)))
    a = jax.nn.sigmoid(a0 + xa @ a_up)
    g = jax.nn.sigmoid(xg) @ g_up
    kk = heads(k * k_k).astype(jnp.float32)
    kk = kk / jnp.maximum(jnp.sqrt(jnp.sum(kk * kk, axis=-1, keepdims=True)), 1e-12)
    k = k * (1.0 + (a - 1.0) * k_a)
    rh, kh, vh, ah = heads(r), heads(k), heads(v), heads(a)
    y = _rwkv7_recurrence(rh, heads(decay), kh, vh, kk, kk * ah.astype(jnp.float32))
    y = _head_standardize(y, RWKV_GN_EPS) * gn_w + gn_b
    bonus = (jnp.sum(rh * kh * r_k, axis=-1, keepdims=True) * vh).reshape(Bn, T, -1)
    return ((y + bonus) * g).astype(r.dtype)


def _mlstm_chunkwise(q, k, v, i_pre, f_pre):
    Bn, T, H, d = q.shape
    nc = T // CHUNK
    to_chunks = lambda t: t.astype(jnp.float32).reshape(Bn, nc, CHUNK, H, -1).transpose(1, 0, 3, 2, 4)
    qc = to_chunks(q) * (d ** -0.5)
    kc, vc = to_chunks(k), to_chunks(v)
    lic = to_chunks(i_pre[..., None])[..., 0]
    lfc = jax.nn.log_sigmoid(to_chunks(f_pre[..., None])[..., 0])
    causal = jnp.tril(jnp.ones((CHUNK, CHUNK), bool))

    def step(carry, inp):
        C, n, m = carry
        q_, k_, v_, li, lf = inp
        bcum = jnp.cumsum(lf, axis=-1)
        Dm = jnp.where(causal, bcum[..., :, None] - bcum[..., None, :] + li[..., None, :], -jnp.inf)
        inter = bcum + m[..., None]
        m_t = jnp.maximum(inter, jnp.max(Dm, axis=-1))
        inter_w = jnp.exp(inter - m_t)
        s = jnp.einsum('bhtd,bhsd->bhts', q_, k_) * jnp.exp(Dm - m_t[..., None])
        num = inter_w[..., None] * jnp.einsum('bhvk,bhtk->bhtv', C, q_) + jnp.einsum('bhts,bhsv->bhtv', s, v_)
        den = inter_w * jnp.einsum('bhk,bhtk->bht', n, q_) + jnp.sum(s, axis=-1)
        h = num / jnp.maximum(jnp.abs(den), jnp.exp(-m_t))[..., None]
        bL = bcum[..., -1]
        ws_log = bL[..., None] - bcum + li
        m_new = jnp.maximum(bL + m, jnp.max(ws_log, axis=-1))
        sdec = jnp.exp(bL + m - m_new)
        ws = jnp.exp(ws_log - m_new[..., None])
        C_new = sdec[..., None, None] * C + jnp.einsum('bhs,bhsv,bhsk->bhvk', ws, v_, k_)
        n_new = sdec[..., None] * n + jnp.einsum('bhs,bhsk->bhk', ws, k_)
        return (C_new, n_new, m_new), h

    carry0 = (jnp.zeros((Bn, H, d, d), jnp.float32), jnp.zeros((Bn, H, d), jnp.float32),
              jnp.zeros((Bn, H), jnp.float32))
    _, hs = lax.scan(step, carry0, (qc, kc, vc, lic, lfc))
    return hs.transpose(1, 0, 3, 2, 4).reshape(Bn, T, H, d)


def _mlstm_mixer(mq, mk, mv, mo, mi, mf, conv_w, conv_b, i_bias, f_bias, norm_w):
    Bn, T, _ = mq.shape
    heads = lambda t: t.reshape(Bn, T, MLSTM_HEADS, MLSTM_HEAD)
    qk = jax.nn.silu(_causal_dwconv(jnp.concatenate([mq, mk], axis=-1), conv_w, conv_b))
    q, k = jnp.split(qk, 2, axis=-1)
    h = _mlstm_chunkwise(heads(q), heads(k), heads(mv), mi + i_bias, mf + f_bias)
    h = _head_standardize(h, MLSTM_NORM_EPS) * norm_w
    return (jax.nn.sigmoid(mo) * h).astype(mq.dtype)


def setup_inputs(seed: int = 0) -> dict:
    key = jax.random.key(seed)
    ks = iter(jax.random.split(key, 48))
    f32 = jnp.float32
    L = DEPTH

    def nrm(shape, scale):
        return jax.random.normal(next(ks), shape, f32) * scale

    def gain(shape):
        return 1.0 + nrm(shape, 0.05)

    return {
        "x": nrm((BATCH, SEQ, D_MODEL), 1.0),
        "ffn1_norm": gain((L, D_MODEL)),
        "ffn1_w_in": nrm((L, D_MODEL, 2 * D_FF), D_MODEL ** -0.5),
        "ffn1_w_out": nrm((L, D_FF, D_MODEL), D_FF ** -0.5),
        "mix_norm": gain((L, D_MODEL)),
        "w_in": nrm((L, D_MODEL, N_IN), D_MODEL ** -0.5),
        "shift_mu": jax.random.uniform(next(ks), (L, N_RW_IN), f32, 0.1, 0.9),
        "rw_w0": jnp.linspace(-6.0, -1.0, D_RWKV, dtype=f32)[None, :] + nrm((L, D_RWKV), 0.1),
        "rw_w_up": nrm((L, LORA_W, D_RWKV), 0.5 * LORA_W ** -0.5),
        "rw_a0": nrm((L, D_RWKV), 0.1),
        "rw_a_up": nrm((L, LORA_A, D_RWKV), 0.5 * LORA_A ** -0.5),
        "rw_g_up": nrm((L, LORA_G, D_RWKV), LORA_G ** -0.5),
        "rw_k_k": 0.85 + nrm((L, D_RWKV), 0.05),
        "rw_k_a": gain((L, D_RWKV)),
        "rw_r_k": nrm((L, RWKV_HEADS, RWKV_HEAD), 0.1),
        "rw_gn_w": gain((L, D_RWKV)),
        "rw_gn_b": nrm((L, D_RWKV), 0.02),
        "vres_down": nrm((L - 1, D_MODEL, LORA_V), D_MODEL ** -0.5),
        "vres_up": nrm((L - 1, LORA_V, D_RWKV), 0.5 * LORA_V ** -0.5),
        "vres_bias": 1.0 + nrm((L - 1, D_RWKV), 0.1),
        "ml_conv_w": nrm((L, CONV_W, 2 * D_MLSTM), CONV_W ** -0.5),
        "ml_conv_b": nrm((L, 2 * D_MLSTM), 0.02),
        "ml_i_bias": nrm((L, MLSTM_HEADS), 0.1),
        "ml_f_bias": jnp.linspace(3.0, 6.0, MLSTM_HEADS, dtype=f32)[None, :] + nrm((L, MLSTM_HEADS), 0.1),
        "ml_norm_w": gain((L, D_MLSTM)),
        "br_a": nrm((L, D_RWKV, D_MODEL), D_RWKV ** -0.5),
        "br_b": nrm((L, D_MLSTM, D_MODEL), D_MLSTM ** -0.5),
        "w_out": nrm((L, D_MODEL, D_MODEL), D_MODEL ** -0.5),
        "ffn2_norm": gain((L, D_MODEL)),
        "ffn2_w_in": nrm((L, D_MODEL, 2 * D_FF), D_MODEL ** -0.5),
        "ffn2_w_out": nrm((L, D_FF, D_MODEL), D_FF ** -0.5),
        "final_norm": gain((D_MODEL,)),
    }


def reference(x, ffn1_norm, ffn1_w_in, ffn1_w_out, mix_norm, w_in, shift_mu,
              rw_w0, rw_w_up, rw_a0, rw_a_up, rw_g_up, rw_k_k, rw_k_a, rw_r_k, rw_gn_w, rw_gn_b,
              vres_down, vres_up, vres_bias,
              ml_conv_w, ml_conv_b, ml_i_bias, ml_f_bias, ml_norm_w,
              br_a, br_b, w_out, ffn2_norm, ffn2_w_in, ffn2_w_out, final_norm):
    h = x
    v_first = None
    for l in range(DEPTH):
        h = h + HALF * _swiglu(_rmsnorm(h, ffn1_norm[l]), ffn1_w_in[l], ffn1_w_out[l])

        xn = _rmsnorm(h, mix_norm[l])
        z = xn @ w_in[l]
        z_rw = z[..., :N_RW_IN]
        z_ml = z[..., N_RW_IN:N_RW_IN + N_ML_IN]
        z_gate = z[..., N_RW_IN + N_ML_IN:]

        z_rw = z_rw + shift_mu[l] * (_time_shift(z_rw) - z_rw)
        r, k, v, xw, xa, xg = _split_cols(z_rw, RW_SPLITS)
        if l == 0:
            v_first = v
        else:
            vg = jax.nn.sigmoid(vres_bias[l - 1] + (xn @ vres_down[l - 1]) @ vres_up[l - 1])
            v = v + (v_first - v) * vg
        y_a = _rwkv7_mixer(r, k, v, xw, xa, xg, rw_w0[l], rw_w_up[l], rw_a0[l], rw_a_up[l],
                           rw_g_up[l], rw_k_k[l], rw_k_a[l], rw_r_k[l], rw_gn_w[l], rw_gn_b[l])

        mq, mk, mv, mo, mi, mf = _split_cols(z_ml, ML_SPLITS)
        y_b = _mlstm_mixer(mq, mk, mv, mo, mi, mf, ml_conv_w[l], ml_conv_b[l],
                           ml_i_bias[l], ml_f_bias[l], ml_norm_w[l])

        g_a, g_b = jnp.split(z_gate, 2, axis=-1)
        u = jax.nn.sigmoid(g_a) * (y_a @ br_a[l]) + jax.nn.sigmoid(g_b) * (y_b @ br_b[l])
        h = h + u @ w_out[l]

        h = h + HALF * _swiglu(_rmsnorm(h, ffn2_norm[l]), ffn2_w_in[l], ffn2_w_out[l])
    return _rmsnorm(h, final_norm)
```

```python
import functools

import jax
import jax.numpy as jnp
from jax import lax
from jax.experimental import pallas as pl
from jax.experimental.pallas import tpu as pltpu

F32 = jnp.float32
BF16 = jnp.bfloat16

D_MODEL = 1024
DEPTH = 4
CHUNK = 64
D_RWKV = 512
RWKV_HEAD = 64
RWKV_HEADS = 8
LORA_W, LORA_A, LORA_V, LORA_G = 64, 64, 32, 160
RWKV_GN_EPS = 64e-5
D_MLSTM = 512
MLSTM_HEADS = 4
MLSTM_HEAD = 128
CONV_W = 4
MLSTM_NORM_EPS = 1e-5
D_FF = 2816
NORM_EPS = 1e-6
N_RW_IN = 3 * D_RWKV + LORA_W + LORA_A + LORA_G
N_ML_IN = 4 * D_MLSTM + 2 * MLSTM_HEADS

SM_W = 384
SM_IF = 288
SM_VRES = 296
IF_LANE = SM_IF - 256

FF_CHUNK = 256
TM_FFN = 512
TT_MIX = 256
TT_REC = 256
VMEM_LIMIT = 56 * 1024 * 1024

HI = lax.Precision.HIGHEST


def _const_spec(shape):
    nd = len(shape)
    return pl.BlockSpec(shape, lambda *_: (0,) * nd, pipeline_mode=pl.Buffered(1))


def _rms(h, g):
    return h * lax.rsqrt(jnp.mean(h * h, axis=-1, keepdims=True) + NORM_EPS) * g


def _sigmoid(x):
    return 1.0 / (1.0 + jnp.exp(-x))


def _dot(a, b):
    return jnp.dot(a, b, preferred_element_type=F32)


def _dot_hi(a, b):
    return jnp.dot(a, b, preferred_element_type=F32, precision=HI)


def _dot_nt_hi(a, b):
    return lax.dot_general(a, b, (((1,), (1,)), ((), ())), preferred_element_type=F32, precision=HI)


def _dot_tn_hi(a, b):
    return lax.dot_general(a, b, (((0,), (0,)), ((), ())), preferred_element_type=F32, precision=HI)


def _seg_sum(x, seg_ref):
    hi = x.astype(BF16)
    lo = (x - hi.astype(F32)).astype(BF16)
    return _dot(hi, seg_ref[...]) + _dot(lo, seg_ref[...])


def _ffn_body(h, g_ref, wg_ref, wu_ref, wo_ref, acc_ref):
    xb = _rms(h, g_ref[...]).astype(BF16)
    acc_ref[...] = jnp.zeros_like(acc_ref)

    def body(c, carry):
        gate = _dot(xb, wg_ref[c])
        up = _dot(xb, wu_ref[c])
        act = (gate * _sigmoid(gate) * up).astype(BF16)
        acc_ref[...] += _dot(act, wo_ref[c])
        return carry

    lax.fori_loop(0, D_FF // FF_CHUNK, body, 0)
    return h + 0.5 * acc_ref[...]


def _ffn_kernel(h_ref, g_ref, wg_ref, wu_ref, wo_ref, out_ref, acc_ref):
    out_ref[...] = _ffn_body(h_ref[...], g_ref, wg_ref, wu_ref, wo_ref, acc_ref)


def _merge_ffn_kernel(h_ref, ya_ref, yb_ref, sg_ref, bra_ref, brb_ref, wout_ref,
                      g_ref, wg_ref, wu_ref, wo_ref, fin_ref, out_ref, acc_ref, *, final):
    sg = sg_ref[...]
    pa = _dot(ya_ref[...].astype(BF16), bra_ref[...])
    pb = _dot(yb_ref[...].astype(BF16), brb_ref[...])
    u = sg[:, :D_MODEL] * pa + sg[:, D_MODEL:] * pb
    h = h_ref[...] + _dot(u.astype(BF16), wout_ref[...])
    h = _ffn_body(h, g_ref, wg_ref, wu_ref, wo_ref, acc_ref)
    if final:
        h = _rms(h, fin_ref[...])
    out_ref[...] = h


def _ffn_weight_specs():
    nc = D_FF // FF_CHUNK
    return [_const_spec((1, D_MODEL)), _const_spec((nc, D_MODEL, FF_CHUNK)),
            _const_spec((nc, D_MODEL, FF_CHUNK)), _const_spec((nc, FF_CHUNK, D_MODEL))]


def _ffn_call(h, g, wg, wu, wo):
    n = h.shape[0]
    tm = min(TM_FFN, n)
    row = pl.BlockSpec((tm, D_MODEL), lambda i: (i, 0))
    return pl.pallas_call(
        _ffn_kernel,
        grid=(n // tm,),
        in_specs=[row] + _ffn_weight_specs(),
        out_specs=row,
        out_shape=jax.ShapeDtypeStruct((n, D_MODEL), F32),
        scratch_shapes=[pltpu.VMEM((tm, D_MODEL), F32)],
        compiler_params=pltpu.CompilerParams(dimension_semantics=("arbitrary",),
                                             vmem_limit_bytes=VMEM_LIMIT),
        name="ffn",
    )(h, g, wg, wu, wo)


def _merge_ffn_call(h, ya, yb, sg, bra, brb, wout, g, wg, wu, wo, fin, final):
    n = h.shape[0]
    tm = min(TM_FFN, n)
    row = lambda w: pl.BlockSpec((tm, w), lambda i: (i, 0))
    return pl.pallas_call(
        functools.partial(_merge_ffn_kernel, final=final),
        grid=(n // tm,),
        in_specs=[row(D_MODEL), row(D_RWKV), row(D_MLSTM), row(2 * D_MODEL),
                  _const_spec((D_RWKV, D_MODEL)), _const_spec((D_MLSTM, D_MODEL)),
                  _const_spec((D_MODEL, D_MODEL))] + _ffn_weight_specs() + [_const_spec((1, D_MODEL))],
        out_specs=row(D_MODEL),
        out_shape=jax.ShapeDtypeStruct((n, D_MODEL), F32),
        scratch_shapes=[pltpu.VMEM((tm, D_MODEL), F32)],
        compiler_params=pltpu.CompilerParams(dimension_semantics=("arbitrary",),
                                             vmem_limit_bytes=VMEM_LIMIT),
        name="merge_ffn",
    )(h, ya, yb, sg, bra, brb, wout, g, wg, wu, wo, fin)


def _shift_rows(z, carry8, s, row8):
    rolled = pltpu.roll(z, s, axis=0)
    top = jnp.where(row8 < s, pltpu.roll(carry8, s, axis=0), rolled[:8])
    return jnp.concatenate([top, rolled[8:]], axis=0)


def _mix_in_kernel(*refs, has_vres):
    if has_vres:
        h_ref, vfirst_ref = refs[0], refs[1]
        refs = refs[2:]
    else:
        h_ref, vfirst_ref = refs[0], None
        refs = refs[1:]
    (norm_ref, w_rkv_ref, w_sm_ref, w_ml_ref, w_gate_ref, mu_rkv_ref, mu_sm_ref,
     w0_ref, wup_ref, a0_ref, aup_ref, gup_ref, kk_ref, ka_ref, rk_ref,
     vbias_ref, vup_ref, convw_ref, convb_ref, seg_ref,
     r_o, k_o, v_o, ld_o, kk_o, a_o, g_o, bonus_o, q_o, mk_o, mv_o, o_o, if_o, sg_o,
     c_rkv, c_sm, c_qk) = refs

    @pl.when(pl.program_id(1) == 0)
    def _():
        c_rkv[...] = jnp.zeros_like(c_rkv)
        c_sm[...] = jnp.zeros_like(c_sm)
        c_qk[...] = jnp.zeros_like(c_qk)

    tt = h_ref.shape[0]
    row8 = lax.broadcasted_iota(jnp.int32, (8, 1), 0)
    xb = _rms(h_ref[...], norm_ref[...]).astype(BF16)

    z_rkv = _dot(xb, w_rkv_ref[...])
    z_sm = _dot(xb, w_sm_ref[...])
    zs_rkv = _shift_rows(z_rkv, c_rkv[...], 1, row8)
    zs_sm = _shift_rows(z_sm, c_sm[...], 1, row8)
    c_rkv[...] = z_rkv[tt - 8:]
    c_sm[...] = z_sm[tt - 8:]
    m_rkv = z_rkv + mu_rkv_ref[...] * (zs_rkv - z_rkv)
    m_sm = z_sm + mu_sm_ref[...] * (zs_sm - z_sm)
    r = m_rkv[:, :D_RWKV]
    k = m_rkv[:, D_RWKV:2 * D_RWKV]
    v = m_rkv[:, 2 * D_RWKV:]
    xwa = m_sm[:, :128]
    xg_blk = m_sm[:, 128:]

    if has_vres:
        vz = _dot(z_sm[:, 128:].astype(BF16), vup_ref[...])
        vg = _sigmoid(vbias_ref[...] + vz)
        v = v + (vfirst_ref[...] - v) * vg

    y = -(w0_ref[...] + _dot(jnp.tanh(xwa).astype(BF16), wup_ref[...]))
    softplus = jnp.maximum(y, 0.0) + jnp.log(1.0 + jnp.exp(-jnp.abs(y)))
    w_log = -softplus - 0.5
    ld_o[...] = -jnp.exp(w_log)
    a = _sigmoid(a0_ref[...] + _dot(xwa.astype(BF16), aup_ref[...]))
    g_o[...] = _dot(_sigmoid(xg_blk).astype(BF16), gup_ref[...])
    kk = k * kk_ref[...]
    kk = kk / jnp.maximum(jnp.sqrt(_seg_sum(kk * kk, seg_ref)), 1e-12)
    k2 = k * (1.0 + (a - 1.0) * ka_ref[...])
    bonus_o[...] = _seg_sum(r * k2 * rk_ref[...], seg_ref) * v
    r_o[...] = r
    k_o[...] = k2
    v_o[...] = v
    kk_o[...] = kk
    a_o[...] = a

    z_ml = _dot(xb, w_ml_ref[...])
    zqk = z_ml[:, :2 * D_MLSTM]
    cw = convw_ref[...]
    conv = convb_ref[...] + cw[CONV_W - 1:CONV_W] * zqk
    carry = c_qk[...]
    for s in range(1, CONV_W):
        conv = conv + cw[CONV_W - 1 - s:CONV_W - s] * _shift_rows(zqk, carry, s, row8)
    c_qk[...] = zqk[tt - 8:]
    qk = conv * _sigmoid(conv)
    q_o[...] = qk[:, :D_MLSTM] * (MLSTM_HEAD ** -0.5)
    mk_o[...] = qk[:, D_MLSTM:]
    mv_o[...] = z_ml[:, 2 * D_MLSTM:3 * D_MLSTM]
    o_o[...] = _sigmoid(z_ml[:, 3 * D_MLSTM:])
    if_o[...] = z_sm[:, 256:]

    sg_o[...] = _sigmoid(_dot(xb, w_gate_ref[...]))


def _mix_in_call(h3, vfirst, p):
    B, T, _ = h3.shape
    tt = min(TT_MIX, T)
    has_vres = vfirst is not None
    tok = lambda w: pl.BlockSpec((None, tt, w), lambda b, t: (b, t, 0))
    in_specs = [tok(D_MODEL)] + ([tok(D_RWKV)] if has_vres else [])
    weights = [p["mix_norm"], p["w_rkv"], p["w_sm"], p["w_ml"], p["w_gate"], p["mu_rkv"], p["mu_sm"],
               p["w0"], p["wup"], p["a0"], p["aup"], p["gup"], p["k_k"], p["k_a"], p["r_k"],
               p["vbias"], p["vup"], p["convw"], p["convb"], p["seg"]]
    in_specs += [_const_spec(w.shape) for w in weights]
    out_w = [D_RWKV] * 8 + [D_MLSTM] * 4 + [128, 2 * D_MODEL]
    out_specs = [tok(w) for w in out_w]
    out_shape = [jax.ShapeDtypeStruct((B, T, w), F32) for w in out_w]
    args = [h3] + ([vfirst] if has_vres else []) + weights
    return pl.pallas_call(
        functools.partial(_mix_in_kernel, has_vres=has_vres),
        grid=(B, T // tt),
        in_specs=in_specs,
        out_specs=out_specs,
        out_shape=out_shape,
        scratch_shapes=[pltpu.VMEM((8, 3 * D_RWKV), F32), pltpu.VMEM((8, SM_W), F32),
                        pltpu.VMEM((8, 2 * D_MLSTM), F32)],
        compiler_params=pltpu.CompilerParams(dimension_semantics=("arbitrary", "arbitrary"),
                                             vmem_limit_bytes=VMEM_LIMIT),
        name="mix_in",
    )(*args)


def _unit_lower_inverse(L, eye):
    n = L.shape[0]
    X = eye + L
    M = _dot_hi(L, L)
    for _ in range(4):
        XM = _dot_hi(jnp.concatenate([X, M], axis=0), M)
        X = X + XM[:n]
        M = XM[n:]
    return X + _dot_hi(X, M)


def _rwkv_kernel(r_ref, k_ref, v_ref, ld_ref, kk_ref, a_ref, g_ref, bonus_ref,
                 gnw_ref, gnb_ref, seg_ref, y_ref, ht_ref):
    @pl.when(pl.program_id(1) == 0)
    def _():
        ht_ref[...] = jnp.zeros_like(ht_ref)

    C, N = CHUNK, RWKV_HEAD
    tt = r_ref.shape[0]
    ti = lax.broadcasted_iota(jnp.int32, (C, C), 0)
    si = lax.broadcasted_iota(jnp.int32, (C, C), 1)
    tri_incl = (si <= ti).astype(F32)
    strict = si < ti
    ti2 = lax.broadcasted_iota(jnp.int32, (C, 2 * C), 0)
    si2 = lax.broadcasted_iota(jnp.int32, (C, 2 * C), 1)
    incl2 = (si2 & (C - 1)) <= ti2
    eye = (si == ti).astype(F32)

    def chunk(c, carry):
        rows = pl.ds(pl.multiple_of(c * C, C), C)
        lw = ld_ref[rows, :]
        cs = _dot_hi(tri_incl, lw)
        kk = kk_ref[rows, :]
        kka = kk * a_ref[rows, :]
        k2 = k_ref[rows, :]
        g_inv = jnp.exp(-cs)
        g_end = jnp.exp(cs[C - 1:C, :] - cs)
        Rt = r_ref[rows, :] * jnp.exp(cs)
        At = -kk * jnp.exp(cs - lw)
        Kb = k2 * g_inv
        Bb = kka * g_inv
        Ke = k2 * g_end
        Be = kka * g_end
        V = v_ref[rows, :]
        g_tot = jnp.exp(cs[C - 1:C, :])
        ys = []
        for h in range(RWKV_HEADS):
            sl = slice(h * N, (h + 1) * N)
            AR = jnp.concatenate([At[:, sl], Rt[:, sl]], axis=0)
            KB = jnp.concatenate([Kb[:, sl], Bb[:, sl]], axis=0)
            P = _dot_nt_hi(AR, KB)
            A_ak = jnp.where(strict, P[:C, :C], 0.0)
            A_ab = jnp.where(strict, P[:C, C:], 0.0)
            A_r = jnp.where(incl2, P[C:, :], 0.0)
            Tm = _unit_lower_inverse(A_ab, eye)
            Ht = ht_ref[h]
            ARH = _dot_nt_hi(AR, Ht)
            Vh = V[:, sl]
            U = _dot_hi(Tm, ARH[:C] + _dot_hi(A_ak, Vh))
            VU = jnp.concatenate([Vh, U], axis=0)
            ys.append(ARH[C:] + _dot_hi(A_r, VU))
            KBe = jnp.concatenate([Ke[:, sl], Be[:, sl]], axis=0)
            ht_ref[h] = Ht * g_tot[:, sl] + _dot_tn_hi(VU, KBe)
        y = jnp.concatenate(ys, axis=1)
        inv_n = 1.0 / N
        mu = _seg_sum(y, seg_ref) * inv_n
        d = y - mu
        var = _seg_sum(d * d, seg_ref) * inv_n
        yn = d * lax.rsqrt(var + RWKV_GN_EPS)
        y_ref[rows, :] = (yn * gnw_ref[...] + gnb_ref[...] + bonus_ref[rows, :]) * g_ref[rows, :]
        return carry

    lax.fori_loop(0, tt // C, chunk, 0)


def _rwkv_call(r, k, v, ld, kk, a, g, bonus, gnw, gnb, seg):
    B, T, _ = r.shape
    tt = min(TT_REC, T)
    tok = pl.BlockSpec((None, tt, D_RWKV), lambda b, t: (b, t, 0))
    return pl.pallas_call(
        _rwkv_kernel,
        grid=(B, T // tt),
        in_specs=[tok] * 8 + [_const_spec((1, D_RWKV)), _const_spec((1, D_RWKV)),
                              _const_spec((D_RWKV, D_RWKV))],
        out_specs=tok,
        out_shape=jax.ShapeDtypeStruct((B, T, D_RWKV), F32),
        scratch_shapes=[pltpu.VMEM((RWKV_HEADS, RWKV_HEAD, RWKV_HEAD), F32)],
        compiler_params=pltpu.CompilerParams(dimension_semantics=("arbitrary", "arbitrary"),
                                             vmem_limit_bytes=VMEM_LIMIT),
        name="rwkv7",
    )(r, k, v, ld, kk, a, g, bonus, gnw, gnb, seg)


def _log_sigmoid(x):
    return jnp.minimum(x, 0.0) - jnp.log(1.0 + jnp.exp(-jnp.abs(x)))


def _mlstm_kernel(q_ref, k_ref, v_ref, o_ref, if_ref, brow_ref, bcol_ref, nw_ref,
                  y_ref, cn_ref, m_ref):
    @pl.when(pl.program_id(1) == 0)
    def _():
        cn_ref[...] = jnp.zeros_like(cn_ref)
        m_ref[...] = jnp.zeros_like(m_ref)

    C, d = CHUNK, MLSTM_HEAD
    tt = q_ref.shape[0]
    ti = lax.broadcasted_iota(jnp.int32, (C, C), 0)
    si = lax.broadcasted_iota(jnp.int32, (C, C), 1)
    causal = si <= ti
    tril = causal.astype(F32)
    triu = (ti <= si).astype(F32)
    ones = jnp.ones((C, d), F32)
    i0 = IF_LANE
    f0 = IF_LANE + MLSTM_HEADS

    def chunk(c, carry):
        rows = pl.ds(pl.multiple_of(c * C, C), C)
        pre = if_ref[rows, :] + brow_ref[...]
        lf_cols = _log_sigmoid(pre)
        bcum_cols = _dot_hi(tril, lf_cols)
        preT = jnp.transpose(if_ref[rows, :])[i0:i0 + 8, :] + bcol_ref[...]
        li_rows = preT[:MLSTM_HEADS]
        bcum_rows = _dot_hi(_log_sigmoid(preT[MLSTM_HEADS:]), triu)
        hs = []
        for h in range(MLSTM_HEADS):
            sl = slice(h * d, (h + 1) * d)
            qh, kh, vh = q_ref[rows, sl], k_ref[rows, sl], v_ref[rows, sl]
            bc = bcum_cols[:, f0 + h:f0 + h + 1]
            li_c = pre[:, i0 + h:i0 + h + 1]
            br = bcum_rows[h:h + 1, :]
            li_r = li_rows[h:h + 1, :]
            m_prev = m_ref[h][0:1, 0:1]
            Dm = jnp.where(causal, bc - br + li_r, -jnp.inf)
            inter = bc + m_prev
            m_t = jnp.maximum(inter, jnp.max(Dm, axis=-1, keepdims=True))
            inter_w = jnp.exp(inter - m_t)
            s = _dot_nt_hi(qh, kh) * jnp.exp(Dm - m_t)
            ve = jnp.concatenate([vh, ones], axis=1)
            cn = cn_ref[h]
            tot = inter_w * _dot_hi(qh, cn) + _dot_hi(s, ve)
            hs.append(tot[:, :d] / jnp.maximum(jnp.abs(tot[:, d:]), jnp.exp(-m_t)))
            bL = bc[C - 1:C, :]
            ws_log = bL - bc + li_c
            m_new = jnp.maximum(bL + m_prev, jnp.max(ws_log, axis=0, keepdims=True))
            sdec = jnp.exp(bL + m_prev - m_new)
            ws = jnp.exp(ws_log - m_new)
            cn_ref[h] = sdec * cn + _dot_tn_hi(ws * kh, ve)
            m_ref[h] = jnp.broadcast_to(m_new, (8, 128))
        for h in range(MLSTM_HEADS):
            sl = slice(h * d, (h + 1) * d)
            hh = hs[h]
            mu = jnp.mean(hh, axis=-1, keepdims=True)
            dd = hh - mu
            var = jnp.mean(dd * dd, axis=-1, keepdims=True)
            y_ref[rows, sl] = o_ref[rows, sl] * (dd * lax.rsqrt(var + MLSTM_NORM_EPS) * nw_ref[:, sl])
        return carry

    lax.fori_loop(0, tt // C, chunk, 0)


def _mlstm_call(q, k, v, o, ifb, brow, bcol, nw):
    B, T, _ = q.shape
    tt = min(TT_REC, T)
    tok = lambda w: pl.BlockSpec((None, tt, w), lambda b, t: (b, t, 0))
    return pl.pallas_call(
        _mlstm_kernel,
        grid=(B, T // tt),
        in_specs=[tok(D_MLSTM)] * 4 + [tok(128), _const_spec((1, 128)), _const_spec((8, 1)),
                                       _const_spec((1, D_MLSTM))],
        out_specs=tok(D_MLSTM),
        out_shape=jax.ShapeDtypeStruct((B, T, D_MLSTM), F32),
        scratch_shapes=[pltpu.VMEM((MLSTM_HEADS, MLSTM_HEAD, 2 * MLSTM_HEAD), F32),
                        pltpu.VMEM((MLSTM_HEADS, 8, 128), F32)],
        compiler_params=pltpu.CompilerParams(dimension_semantics=("arbitrary", "arbitrary"),
                                             vmem_limit_bytes=VMEM_LIMIT),
        name="mlstm",
    )(q, k, v, o, ifb, brow, bcol, nw)


def _pad_rows(w, rows, at=0):
    out = jnp.zeros((rows, w.shape[1]), w.dtype)
    return out.at[at:at + w.shape[0]].set(w)


def _ffn_params(norm, w_in, w_out):
    nc = D_FF // FF_CHUNK
    wg = w_in[:, :D_FF].reshape(D_MODEL, nc, FF_CHUNK).transpose(1, 0, 2).astype(BF16)
    wu = w_in[:, D_FF:].reshape(D_MODEL, nc, FF_CHUNK).transpose(1, 0, 2).astype(BF16)
    wo = w_out.reshape(nc, FF_CHUNK, D_MODEL).astype(BF16)
    return norm.reshape(1, D_MODEL), wg, wu, wo


def _mixer_params(l, mix_norm, w_in, shift_mu, rw_w0, rw_w_up, rw_a0, rw_a_up, rw_g_up, rw_k_k, rw_k_a,
                  rw_r_k, vres_down, vres_up, vres_bias, ml_conv_w, ml_conv_b, ml_i_bias, ml_f_bias):
    w = w_in[l]
    o_ml = N_RW_IN
    o_gate = N_RW_IN + N_ML_IN
    w_rkv = w[:, :3 * D_RWKV]
    w_lora = w[:, 3 * D_RWKV:N_RW_IN]
    w_if = w[:, o_ml + 4 * D_MLSTM:o_gate]
    w_vd = vres_down[l - 1] if l > 0 else jnp.zeros((D_MODEL, LORA_V), F32)
    pad = jnp.zeros((D_MODEL, SM_W - SM_VRES - LORA_V), F32)
    w_sm = jnp.concatenate([w_lora, w_if, w_vd, pad], axis=1)
    mu = shift_mu[l]
    mu_sm = jnp.concatenate([mu[3 * D_RWKV:], jnp.zeros((SM_W - (N_RW_IN - 3 * D_RWKV),), F32)])
    row = lambda x: x.reshape(1, -1)
    p = {
        "mix_norm": row(mix_norm[l]),
        "w_rkv": w_rkv.astype(BF16),
        "w_sm": w_sm.astype(BF16),
        "w_ml": w[:, o_ml:o_ml + 4 * D_MLSTM].astype(BF16),
        "w_gate": w[:, o_gate:].astype(BF16),
        "mu_rkv": row(mu[:3 * D_RWKV]),
        "mu_sm": row(mu_sm),
        "w0": row(rw_w0[l]),
        "wup": _pad_rows(rw_w_up[l], 128, 0).astype(BF16),
        "a0": row(rw_a0[l]),
        "aup": _pad_rows(rw_a_up[l], 128, LORA_W).astype(BF16),
        "gup": _pad_rows(rw_g_up[l], 256, 0).astype(BF16),
        "k_k": row(rw_k_k[l]),
        "k_a": row(rw_k_a[l]),
        "r_k": row(rw_r_k[l]),
        "vbias": row(vres_bias[l - 1]) if l > 0 else jnp.zeros((1, D_RWKV), F32),
        "vup": (_pad_rows(vres_up[l - 1], 256, SM_VRES - 128) if l > 0
                else jnp.zeros((256, D_RWKV), F32)).astype(BF16),
        "convw": ml_conv_w[l],
        "convb": row(ml_conv_b[l]),
    }
    gate_bias = jnp.concatenate([ml_i_bias[l], ml_f_bias[l]])
    p["if_brow"] = jnp.zeros((1, 128), F32).at[0, IF_LANE:IF_LANE + 8].set(gate_bias)
    p["if_bcol"] = gate_bias.reshape(8, 1)
    return p


def kernel(x, ffn1_norm, ffn1_w_in, ffn1_w_out, mix_norm, w_in, shift_mu, rw_w0, rw_w_up, rw_a0, rw_a_up, rw_g_up, rw_k_k, rw_k_a, rw_r_k, rw_gn_w, rw_gn_b, vres_down, vres_up, vres_bias, ml_conv_w, ml_conv_b, ml_i_bias, ml_f_bias, ml_norm_w, br_a, br_b, w_out, ffn2_norm, ffn2_w_in, ffn2_w_out, final_norm):
    B, T, D = x.shape
    assert D == D_MODEL and T % CHUNK == 0
    n = B * T
    lane = jnp.arange(D_RWKV) // RWKV_HEAD
    seg = (lane[:, None] == lane[None, :]).astype(BF16)
    fin = final_norm.reshape(1, D_MODEL)
    h = x.reshape(n, D)
    v_first = None
    for l in range(DEPTH):
        h = _ffn_call(h, *_ffn_params(ffn1_norm[l], ffn1_w_in[l], ffn1_w_out[l]))
        p = _mixer_params(l, mix_norm, w_in, shift_mu, rw_w0, rw_w_up, rw_a0, rw_a_up, rw_g_up, rw_k_k,
                          rw_k_a, rw_r_k, vres_down, vres_up, vres_bias, ml_conv_w, ml_conv_b,
                          ml_i_bias, ml_f_bias)
        p["seg"] = seg
        (r, k, v, ld, kk, a, g, bonus, q, mk, mv, o, ifb, sg) = _mix_in_call(
            h.reshape(B, T, D), v_first, p)
        if l == 0:
            v_first = v
        ya = _rwkv_call(r, k, v, ld, kk, a, g, bonus, rw_gn_w[l].reshape(1, -1),
                        rw_gn_b[l].reshape(1, -1), seg)
        yb = _mlstm_call(q, mk, mv, o, ifb, p["if_brow"], p["if_bcol"], ml_norm_w[l].reshape(1, -1))
        h = _merge_ffn_call(h, ya.reshape(n, -1), yb.reshape(n, -1), sg.reshape(n, -1),
                            br_a[l].astype(BF16), br_b[l].astype(BF16), w_out[l].astype(BF16),
                            *_ffn_params(ffn2_norm[l], ffn2_w_in[l], ffn2_w_out[l]), fin,
                            final=(l == DEPTH - 1))
    return h.reshape(B, T, D)
```

```python
import functools

import jax
import jax.numpy as jnp
from jax import lax
from jax.experimental import pallas as pl
from jax.experimental.pallas import tpu as pltpu

F32 = jnp.float32
BF16 = jnp.bfloat16

D_MODEL = 1024
DEPTH = 4
CHUNK = 64
D_RWKV = 512
RWKV_HEAD = 64
RWKV_HEADS = 8
LORA_W, LORA_A, LORA_V, LORA_G = 64, 64, 32, 160
RWKV_GN_EPS = 64e-5
D_MLSTM = 512
MLSTM_HEADS = 4
MLSTM_HEAD = 128
CONV_W = 4
MLSTM_NORM_EPS = 1e-5
D_FF = 2816
NORM_EPS = 1e-6
N_RW_IN = 3 * D_RWKV + LORA_W + LORA_A + LORA_G
N_ML_IN = 4 * D_MLSTM + 2 * MLSTM_HEADS

SM_W = 384
SM_IF = 288
SM_VRES = 296
IF_LANE = SM_IF - 256

FF_CHUNK = 256
TM_FFN = 512
TT_MIX = 256
TT_REC = 256
VMEM_LIMIT = 56 * 1024 * 1024

HI = lax.Precision.HIGHEST


def _const_spec(shape):
    nd = len(shape)
    return pl.BlockSpec(shape, lambda *_: (0,) * nd, pipeline_mode=pl.Buffered(1))


def _rms(h, g):
    return h * lax.rsqrt(jnp.mean(h * h, axis=-1, keepdims=True) + NORM_EPS) * g


def _sigmoid(x):
    return 1.0 / (1.0 + jnp.exp(-x))


def _dot(a, b):
    return jnp.dot(a, b, preferred_element_type=F32)


def _dot_hi(a, b):
    return jnp.dot(a, b, preferred_element_type=F32, precision=HI)


def _dot_nt_hi(a, b):
    return lax.dot_general(a, b, (((1,), (1,)), ((), ())), preferred_element_type=F32, precision=HI)


def _dot_tn_hi(a, b):
    return lax.dot_general(a, b, (((0,), (0,)), ((), ())), preferred_element_type=F32, precision=HI)


def _seg_sum(x, seg_ref):
    hi = x.astype(BF16)
    lo = (x - hi.astype(F32)).astype(BF16)
    return _dot(hi, seg_ref[...]) + _dot(lo, seg_ref[...])


def _ffn_body(h, g_ref, wg_ref, wu_ref, wo_ref, acc_ref):
    xb = _rms(h, g_ref[...]).astype(BF16)
    acc_ref[...] = jnp.zeros_like(acc_ref)

    def body(c, carry):
        gate = _dot(xb, wg_ref[c])
        up = _dot(xb, wu_ref[c])
        act = (gate * _sigmoid(gate) * up).astype(BF16)
        acc_ref[...] += _dot(act, wo_ref[c])
        return carry

    lax.fori_loop(0, D_FF // FF_CHUNK, body, 0)
    return h + 0.5 * acc_ref[...]


def _ffn_kernel(h_ref, g_ref, wg_ref, wu_ref, wo_ref, out_ref, acc_ref):
    out_ref[...] = _ffn_body(h_ref[...], g_ref, wg_ref, wu_ref, wo_ref, acc_ref)


def _merge_ffn_kernel(h_ref, ya_ref, yb_ref, sg_ref, bra_ref, brb_ref, wout_ref,
                      g_ref, wg_ref, wu_ref, wo_ref, fin_ref, out_ref, acc_ref, *, final):
    sg = sg_ref[...]
    pa = _dot(ya_ref[...].astype(BF16), bra_ref[...])
    pb = _dot(yb_ref[...].astype(BF16), brb_ref[...])
    u = sg[:, :D_MODEL] * pa + sg[:, D_MODEL:] * pb
    h = h_ref[...] + _dot(u.astype(BF16), wout_ref[...])
    h = _ffn_body(h, g_ref, wg_ref, wu_ref, wo_ref, acc_ref)
    if final:
        h = _rms(h, fin_ref[...])
    out_ref[...] = h


def _ffn_weight_specs():
    nc = D_FF // FF_CHUNK
    return [_const_spec((1, D_MODEL)), _const_spec((nc, D_MODEL, FF_CHUNK)),
            _const_spec((nc, D_MODEL, FF_CHUNK)), _const_spec((nc, FF_CHUNK, D_MODEL))]


def _ffn_call(h, g, wg, wu, wo):
    n = h.shape[0]
    tm = min(TM_FFN, n)
    row = pl.BlockSpec((tm, D_MODEL), lambda i: (i, 0))
    return pl.pallas_call(
        _ffn_kernel,
        grid=(n // tm,),
        in_specs=[row] + _ffn_weight_specs(),
        out_specs=row,
        out_shape=jax.ShapeDtypeStruct((n, D_MODEL), F32),
        scratch_shapes=[pltpu.VMEM((tm, D_MODEL), F32)],
        compiler_params=pltpu.CompilerParams(dimension_semantics=("arbitrary",),
                                             vmem_limit_bytes=VMEM_LIMIT),
        name="ffn",
    )(h, g, wg, wu, wo)


def _merge_ffn_call(h, ya, yb, sg, bra, brb, wout, g, wg, wu, wo, fin, final):
    n = h.shape[0]
    tm = min(TM_FFN, n)
    row = lambda w: pl.BlockSpec((tm, w), lambda i: (i, 0))
    return pl.pallas_call(
        functools.partial(_merge_ffn_kernel, final=final),
        grid=(n // tm,),
        in_specs=[row(D_MODEL), row(D_RWKV), row(D_MLSTM), row(2 * D_MODEL),
                  _const_spec((D_RWKV, D_MODEL)), _const_spec((D_MLSTM, D_MODEL)),
                  _const_spec((D_MODEL, D_MODEL))] + _ffn_weight_specs() + [_const_spec((1, D_MODEL))],
        out_specs=row(D_MODEL),
        out_shape=jax.ShapeDtypeStruct((n, D_MODEL), F32),
        scratch_shapes=[pltpu.VMEM((tm, D_MODEL), F32)],
        compiler_params=pltpu.CompilerParams(dimension_semantics=("arbitrary",),
                                             vmem_limit_bytes=VMEM_LIMIT),
        name="merge_ffn",
    )(h, ya, yb, sg, bra, brb, wout, g, wg, wu, wo, fin)


def _shift_rows(z, carry8, s, row8):
    rolled = pltpu.roll(z, s, axis=0)
    top = jnp.where(row8 < s, pltpu.roll(carry8, s, axis=0), rolled[:8])
    return jnp.concatenate([top, rolled[8:]], axis=0)


def _mix_in_kernel(*refs, has_vres):
    if has_vres:
        h_ref, vfirst_ref = refs[0], refs[1]
        refs = refs[2:]
    else:
        h_ref, vfirst_ref = refs[0], None
        refs = refs[1:]
    (norm_ref, w_rkv_ref, w_sm_ref, w_ml_ref, w_gate_ref, mu_rkv_ref, mu_sm_ref,
     w0_ref, wup_ref, a0_ref, aup_ref, gup_ref, kk_ref, ka_ref, rk_ref,
     vbias_ref, vup_ref, convw_ref, convb_ref, seg_ref,
     r_o, k_o, v_o, ld_o, kk_o, a_o, g_o, bonus_o, q_o, mk_o, mv_o, o_o, if_o, sg_o,
     c_rkv, c_sm, c_qk) = refs

    @pl.when(pl.program_id(1) == 0)
    def _():
        c_rkv[...] = jnp.zeros_like(c_rkv)
        c_sm[...] = jnp.zeros_like(c_sm)
        c_qk[...] = jnp.zeros_like(c_qk)

    tt = h_ref.shape[0]
    row8 = lax.broadcasted_iota(jnp.int32, (8, 1), 0)
    xb = _rms(h_ref[...], norm_ref[...]).astype(BF16)

    z_rkv = _dot(xb, w_rkv_ref[...])
    z_sm = _dot(xb, w_sm_ref[...])
    zs_rkv = _shift_rows(z_rkv, c_rkv[...], 1, row8)
    zs_sm = _shift_rows(z_sm, c_sm[...], 1, row8)
    c_rkv[...] = z_rkv[tt - 8:]
    c_sm[...] = z_sm[tt - 8:]
    m_rkv = z_rkv + mu_rkv_ref[...] * (zs_rkv - z_rkv)
    m_sm = z_sm + mu_sm_ref[...] * (zs_sm - z_sm)
    r = m_rkv[:, :D_RWKV]
    k = m_rkv[:, D_RWKV:2 * D_RWKV]
    v = m_rkv[:, 2 * D_RWKV:]
    xwa = m_sm[:, :128]
    xg_blk = m_sm[:, 128:]

    if has_vres:
        vz = _dot(z_sm[:, 128:].astype(BF16), vup_ref[...])
        vg = _sigmoid(vbias_ref[...] + vz)
        v = v + (vfirst_ref[...] - v) * vg

    y = -(w0_ref[...] + _dot(jnp.tanh(xwa).astype(BF16), wup_ref[...]))
    softplus = jnp.maximum(y, 0.0) + jnp.log(1.0 + jnp.exp(-jnp.abs(y)))
    w_log = -softplus - 0.5
    ld_o[...] = -jnp.exp(w_log)
    a = _sigmoid(a0_ref[...] + _dot(xwa.astype(BF16), aup_ref[...]))
    g_o[...] = _dot(_sigmoid(xg_blk).astype(BF16), gup_ref[...])
    kk = k * kk_ref[...]
    kk = kk / jnp.maximum(jnp.sqrt(_seg_sum(kk * kk, seg_ref)), 1e-12)
    k2 = k * (1.0 + (a - 1.0) * ka_ref[...])
    bonus_o[...] = _seg_sum(r * k2 * rk_ref[...], seg_ref) * v
    r_o[...] = r
    k_o[...] = k2
    v_o[...] = v
    kk_o[...] = kk
    a_o[...] = a

    z_ml = _dot(xb, w_ml_ref[...])
    zqk = z_ml[:, :2 * D_MLSTM]
    cw = convw_ref[...]
    conv = convb_ref[...] + cw[CONV_W - 1:CONV_W] * zqk
    carry = c_qk[...]
    for s in range(1, CONV_W):
        conv = conv + cw[CONV_W - 1 - s:CONV_W - s] * _shift_rows(zqk, carry, s, row8)
    c_qk[...] = zqk[tt - 8:]
    qk = conv * _sigmoid(conv)
    q_o[...] = qk[:, :D_MLSTM] * (MLSTM_HEAD ** -0.5)
    mk_o[...] = qk[:, D_MLSTM:]
    mv_o[...] = z_ml[:, 2 * D_MLSTM:3 * D_MLSTM]
    o_o[...] = _sigmoid(z_ml[:, 3 * D_MLSTM:])
    if_o[...] = z_sm[:, 256:]

    sg_o[...] = _sigmoid(_dot(xb, w_gate_ref[...]))


def _mix_in_call(h3, vfirst, p):
    B, T, _ = h3.shape
    tt = min(TT_MIX, T)
    has_vres = vfirst is not None
    tok = lambda w: pl.BlockSpec((None, tt, w), lambda b, t: (b, t, 0))
    in_specs = [tok(D_MODEL)] + ([tok(D_RWKV)] if has_vres else [])
    weights = [p["mix_norm"], p["w_rkv"], p["w_sm"], p["w_ml"], p["w_gate"], p["mu_rkv"], p["mu_sm"],
               p["w0"], p["wup"], p["a0"], p["aup"], p["gup"], p["k_k"], p["k_a"], p["r_k"],
               p["vbias"], p["vup"], p["convw"], p["convb"], p["seg"]]
    in_specs += [_const_spec(w.shape) for w in weights]
    out_w = [D_RWKV] * 8 + [D_MLSTM] * 4 + [128, 2 * D_MODEL]
    out_specs = [tok(w) for w in out_w]
    out_shape = [jax.ShapeDtypeStruct((B, T, w), F32) for w in out_w]
    args = [h3] + ([vfirst] if has_vres else []) + weights
    return pl.pallas_call(
        functools.partial(_mix_in_kernel, has_vres=has_vres),
        grid=(B, T // tt),
        in_specs=in_specs,
        out_specs=out_specs,
        out_shape=out_shape,
        scratch_shapes=[pltpu.VMEM((8, 3 * D_RWKV), F32), pltpu.VMEM((8, SM_W), F32),
                        pltpu.VMEM((8, 2 * D_MLSTM), F32)],
        compiler_params=pltpu.CompilerParams(dimension_semantics=("arbitrary", "arbitrary"),
                                             vmem_limit_bytes=VMEM_LIMIT),
        name="mix_in",
    )(*args)


def _bdot(a, b):
    return jnp.dot(a.astype(BF16), b.astype(BF16), preferred_element_type=F32)


def _bdot_nt(a, b):
    return lax.dot_general(a.astype(BF16), b.astype(BF16), (((1,), (1,)), ((), ())),
                           preferred_element_type=F32)


def _bdot_tn(a, b):
    return lax.dot_general(a.astype(BF16), b.astype(BF16), (((0,), (0,)), ((), ())),
                           preferred_element_type=F32)


def _split3(x):
    hi = x.astype(BF16)
    r1 = x - hi.astype(F32)
    mid = r1.astype(BF16)
    lo = (r1 - mid.astype(F32)).astype(BF16)
    return hi, mid, lo


def _rwkv_kernel(r_ref, k_ref, v_ref, ld_ref, kk_ref, a_ref, g_ref, bonus_ref,
                 gnw_ref, gnb_ref, seg_ref, y_ref, h_ref):
    @pl.when(pl.program_id(1) == 0)
    def _():
        h_ref[...] = jnp.zeros_like(h_ref)

    C, N = CHUNK, RWKV_HEAD
    tt = r_ref.shape[0]
    ti = lax.broadcasted_iota(jnp.int32, (C, C), 0)
    si = lax.broadcasted_iota(jnp.int32, (C, C), 1)
    strict = si < ti
    incl = si <= ti
    eye = (si == ti).astype(F32)

    tr = lax.broadcasted_iota(jnp.int32, (tt, tt), 0)
    tc = lax.broadcasted_iota(jnp.int32, (tt, tt), 1)
    same = (tr // C) == (tc // C)
    tri = (same & (tc <= tr)).astype(BF16)
    blk = same.astype(BF16)
    lw = ld_ref[...]
    parts = _split3(lw)
    cs = sum(_dot(tri, p) for p in parts)
    cs_tot = sum(_dot(blk, p) for p in parts)

    kk = kk_ref[...]
    kka = kk * a_ref[...]
    k2 = k_ref[...]
    g_inv = jnp.exp(-cs)
    g_end = jnp.exp(cs_tot - cs)
    g_tot = jnp.exp(cs_tot)
    Rt = (r_ref[...] * jnp.exp(cs)).astype(BF16)
    At = (-kk * jnp.exp(cs - lw)).astype(BF16)
    Kb = (k2 * g_inv).astype(BF16)
    Bb = (kka * g_inv).astype(BF16)
    Ke = (k2 * g_end).astype(BF16)
    Be = (kka * g_end).astype(BF16)
    V = v_ref[...].astype(BF16)

    nc = tt // C
    inst = [(c, h) for c in range(nc) for h in range(RWKV_HEADS)]
    sl = lambda x, c, h: x[c * C:(c + 1) * C, h * N:(h + 1) * N]
    each = lambda f, *lists: [f(*xs) for xs in zip(*lists)]

    At_i = [sl(At, c, h) for c, h in inst]
    Rt_i = [sl(Rt, c, h) for c, h in inst]
    V_i = [sl(V, c, h) for c, h in inst]
    P = [_bdot_nt(jnp.concatenate([a, r], axis=0),
                  jnp.concatenate([sl(Kb, c, h), sl(Bb, c, h)], axis=0))
         for a, r, (c, h) in zip(At_i, Rt_i, inst)]
    A_ak = [jnp.where(strict, p[:C, :C], 0.0) for p in P]
    L = [jnp.where(strict, p[:C, C:], 0.0) for p in P]
    A_rk = [jnp.where(incl, p[C:, :C], 0.0) for p in P]
    A_rb = [jnp.where(incl, p[C:, C:], 0.0).astype(BF16) for p in P]
    AV = each(lambda ak, rk, v: _bdot(jnp.concatenate([ak, rk], axis=0), v), A_ak, A_rk, V_i)
    X = [eye + l for l in L]
    M = each(lambda l: _bdot(l, l), L)
    for _ in range(4):
        XM = each(lambda x, m: _bdot(jnp.concatenate([x, m], axis=0), m), X, M)
        X = each(lambda x, xm: x + xm[:C], X, XM)
        M = [xm[C:] for xm in XM]
    Tm = each(lambda x, m: x + _bdot(x, m), X, M)
    W = each(lambda t, a, av: _bdot(t, jnp.concatenate([a, av[:C].astype(BF16)], axis=1)),
             Tm, At_i, AV)
    Z = each(_bdot, A_rb, W)
    G = [_bdot_tn(sl(Be, c, h), w) for (c, h), w in zip(inst, W)]
    KV = [_bdot_tn(sl(Ke, c, h), v) for (c, h), v in zip(inst, V_i)]
    RM = [jnp.concatenate([r.astype(F32) + z[:, :N],
                           eye * g_tot[c * C:c * C + 1, h * N:(h + 1) * N] + g[:, :N]], axis=0).astype(BF16)
          for r, z, g, (c, h) in zip(Rt_i, Z, G, inst)]
    Y0 = each(lambda av, z: av[C:] + z[:, N:], AV, Z)
    G0 = each(lambda kv, g: kv + g[:, N:], KV, G)

    Hs = [h_ref[h] for h in range(RWKV_HEADS)]
    y_rows = []
    for c in range(nc):
        H_hi = [x.astype(BF16) for x in Hs]
        H_lo = [(x - hi.astype(F32)).astype(BF16) for x, hi in zip(Hs, H_hi)]
        RMH = [_dot(RM[c * RWKV_HEADS + h], H_hi[h]) + _dot(RM[c * RWKV_HEADS + h], H_lo[h])
               for h in range(RWKV_HEADS)]
        y_rows.append(jnp.concatenate([RMH[h][:C] + Y0[c * RWKV_HEADS + h]
                                       for h in range(RWKV_HEADS)], axis=1))
        Hs = [RMH[h][C:] + G0[c * RWKV_HEADS + h] for h in range(RWKV_HEADS)]
    for h in range(RWKV_HEADS):
        h_ref[h] = Hs[h]
    y = jnp.concatenate(y_rows, axis=0)
    inv_n = 1.0 / N
    mu = _seg_sum(y, seg_ref) * inv_n
    d = y - mu
    var = _seg_sum(d * d, seg_ref) * inv_n
    yn = d * lax.rsqrt(var + RWKV_GN_EPS)
    y_ref[...] = (yn * gnw_ref[...] + gnb_ref[...] + bonus_ref[...]) * g_ref[...]


def _rwkv_call(r, k, v, ld, kk, a, g, bonus, gnw, gnb, seg):
    B, T, _ = r.shape
    tt = min(TT_REC, T)
    tok = pl.BlockSpec((None, tt, D_RWKV), lambda b, t: (b, t, 0))
    return pl.pallas_call(
        _rwkv_kernel,
        grid=(B, T // tt),
        in_specs=[tok] * 8 + [_const_spec((1, D_RWKV)), _const_spec((1, D_RWKV)),
                              _const_spec((D_RWKV, D_RWKV))],
        out_specs=tok,
        out_shape=jax.ShapeDtypeStruct((B, T, D_RWKV), F32),
        scratch_shapes=[pltpu.VMEM((RWKV_HEADS, RWKV_HEAD, RWKV_HEAD), F32)],
        compiler_params=pltpu.CompilerParams(dimension_semantics=("arbitrary", "arbitrary"),
                                             vmem_limit_bytes=VMEM_LIMIT),
        name="rwkv7",
    )(r, k, v, ld, kk, a, g, bonus, gnw, gnb, seg)


def _log_sigmoid(x):
    return jnp.minimum(x, 0.0) - jnp.log(1.0 + jnp.exp(-jnp.abs(x)))


def _mlstm_kernel(q_ref, k_ref, v_ref, o_ref, if_ref, brow_ref, bcol_ref, nw_ref,
                  y_ref, cn_ref, m_ref):
    @pl.when(pl.program_id(1) == 0)
    def _():
        cn_ref[...] = jnp.zeros_like(cn_ref)
        m_ref[...] = jnp.zeros_like(m_ref)

    C, d = CHUNK, MLSTM_HEAD
    tt = q_ref.shape[0]
    nc = tt // C
    NH = MLSTM_HEADS
    ti = lax.broadcasted_iota(jnp.int32, (C, C), 0)
    si = lax.broadcasted_iota(jnp.int32, (C, C), 1)
    causal = si <= ti
    ones = jnp.ones((C, d), BF16)
    i0 = IF_LANE
    f0 = IF_LANE + NH

    tr = lax.broadcasted_iota(jnp.int32, (tt, tt), 0)
    tc = lax.broadcasted_iota(jnp.int32, (tt, tt), 1)
    same = (tr // C) == (tc // C)
    tril = (same & (tc <= tr)).astype(BF16)
    triu = (same & (tr <= tc)).astype(BF16)
    pre = if_ref[...] + brow_ref[...]
    bcum_cols = sum(_dot(tril, p) for p in _split3(_log_sigmoid(pre)))
    preT = jnp.transpose(if_ref[...])[i0:i0 + 8, :] + bcol_ref[...]
    row8 = lax.broadcasted_iota(jnp.int32, (8, 1), 0)
    bcum_rows = sum(_dot(p, triu) for p in _split3(jnp.where(row8 < NH, preT, _log_sigmoid(preT))))

    inst = [(c, h) for c in range(nc) for h in range(NH)]
    rs = lambda c: slice(c * C, (c + 1) * C)
    hs_ = lambda h: slice(h * d, (h + 1) * d)
    q_i = [q_ref[rs(c), hs_(h)].astype(BF16) for c, h in inst]
    k_i = [k_ref[rs(c), hs_(h)] for c, h in inst]
    ve_i = [jnp.concatenate([v_ref[rs(c), hs_(h)].astype(BF16), ones], axis=1) for c, h in inst]
    bc_i = [bcum_cols[rs(c), f0 + h:f0 + h + 1] for c, h in inst]
    e_c = [pre[rs(c), i0 + h:i0 + h + 1] - bc for (c, h), bc in zip(inst, bc_i)]
    e_r = [preT[h:h + 1, rs(c)] - bcum_rows[NH + h:NH + h + 1, rs(c)] for c, h in inst]
    a_col = [jnp.max(jnp.where(causal, e, -jnp.inf), axis=-1, keepdims=True) for e in e_r]
    D0 = [jnp.where(causal, jnp.exp(e - a), 0.0) for e, a in zip(e_r, a_col)]
    S0 = [(_bdot_nt(q, k) * dm).astype(BF16) for q, k, dm in zip(q_i, k_i, D0)]
    a_end = [a[C - 1:C, :] for a in a_col]
    KV0 = [_bdot_tn(jnp.exp(e - ae) * k, ve) for e, ae, k, ve in zip(e_c, a_end, k_i, ve_i)]
    intra = [_dot(s, ve) for s, ve in zip(S0, ve_i)]

    cn = [cn_ref[h] for h in range(NH)]
    m_prev = [m_ref[h][0:1, 0:1] for h in range(NH)]
    for c in range(nc):
        ii = [c * NH + h for h in range(NH)]
        QC = [_bdot(q_i[i], cn[h]) for h, i in enumerate(ii)]
        for h, i in enumerate(ii):
            mu_t = jnp.maximum(m_prev[h], a_col[i])
            tot = jnp.exp(m_prev[h] - mu_t) * QC[h] + jnp.exp(a_col[i] - mu_t) * intra[i]
            hh = tot[:, :d] / jnp.maximum(jnp.abs(tot[:, d:]), jnp.exp(-(bc_i[i] + mu_t)))
            mean = jnp.mean(hh, axis=-1, keepdims=True)
            dd = hh - mean
            var = jnp.mean(dd * dd, axis=-1, keepdims=True)
            y_ref[rs(c), hs_(h)] = o_ref[rs(c), hs_(h)] * (dd * lax.rsqrt(var + MLSTM_NORM_EPS)
                                                           * nw_ref[:, hs_(h)])
            mu_end = jnp.maximum(m_prev[h], a_end[i])
            cn[h] = jnp.exp(m_prev[h] - mu_end) * cn[h] + jnp.exp(a_end[i] - mu_end) * KV0[i]
            m_prev[h] = bc_i[i][C - 1:C, :] + mu_end
    for h in range(NH):
        cn_ref[h] = cn[h]
        m_ref[h] = jnp.broadcast_to(m_prev[h], (8, 128))


def _mlstm_call(q, k, v, o, ifb, brow, bcol, nw):
    B, T, _ = q.shape
    tt = min(TT_REC, T)
    tok = lambda w: pl.BlockSpec((None, tt, w), lambda b, t: (b, t, 0))
    return pl.pallas_call(
        _mlstm_kernel,
        grid=(B, T // tt),
        in_specs=[tok(D_MLSTM)] * 4 + [tok(128), _const_spec((1, 128)), _const_spec((8, 1)),
                                       _const_spec((1, D_MLSTM))],
        out_specs=tok(D_MLSTM),
        out_shape=jax.ShapeDtypeStruct((B, T, D_MLSTM), F32),
        scratch_shapes=[pltpu.VMEM((MLSTM_HEADS, MLSTM_HEAD, 2 * MLSTM_HEAD), F32),
                        pltpu.VMEM((MLSTM_HEADS, 8, 128), F32)],
        compiler_params=pltpu.CompilerParams(dimension_semantics=("arbitrary", "arbitrary"),
                                             vmem_limit_bytes=VMEM_LIMIT),
        name="mlstm",
    )(q, k, v, o, ifb, brow, bcol, nw)


def _pad_rows(w, rows, at=0):
    out = jnp.zeros((rows, w.shape[1]), w.dtype)
    return out.at[at:at + w.shape[0]].set(w)


def _ffn_params(norm, w_in, w_out):
    nc = D_FF // FF_CHUNK
    wg = w_in[:, :D_FF].reshape(D_MODEL, nc, FF_CHUNK).transpose(1, 0, 2).astype(BF16)
    wu = w_in[:, D_FF:].reshape(D_MODEL, nc, FF_CHUNK).transpose(1, 0, 2).astype(BF16)
    wo = w_out.reshape(nc, FF_CHUNK, D_MODEL).astype(BF16)
    return norm.reshape(1, D_MODEL), wg, wu, wo


def _mixer_params(l, mix_norm, w_in, shift_mu, rw_w0, rw_w_up, rw_a0, rw_a_up, rw_g_up, rw_k_k, rw_k_a,
                  rw_r_k, vres_down, vres_up, vres_bias, ml_conv_w, ml_conv_b, ml_i_bias, ml_f_bias):
    w = w_in[l]
    o_ml = N_RW_IN
    o_gate = N_RW_IN + N_ML_IN
    w_rkv = w[:, :3 * D_RWKV]
    w_lora = w[:, 3 * D_RWKV:N_RW_IN]
    w_if = w[:, o_ml + 4 * D_MLSTM:o_gate]
    w_vd = vres_down[l - 1] if l > 0 else jnp.zeros((D_MODEL, LORA_V), F32)
    pad = jnp.zeros((D_MODEL, SM_W - SM_VRES - LORA_V), F32)
    w_sm = jnp.concatenate([w_lora, w_if, w_vd, pad], axis=1)
    mu = shift_mu[l]
    mu_sm = jnp.concatenate([mu[3 * D_RWKV:], jnp.zeros((SM_W - (N_RW_IN - 3 * D_RWKV),), F32)])
    row = lambda x: x.reshape(1, -1)
    p = {
        "mix_norm": row(mix_norm[l]),
        "w_rkv": w_rkv.astype(BF16),
        "w_sm": w_sm.astype(BF16),
        "w_ml": w[:, o_ml:o_ml + 4 * D_MLSTM].astype(BF16),
        "w_gate": w[:, o_gate:].astype(BF16),
        "mu_rkv": row(mu[:3 * D_RWKV]),
        "mu_sm": row(mu_sm),
        "w0": row(rw_w0[l]),
        "wup": _pad_rows(rw_w_up[l], 128, 0).astype(BF16),
        "a0": row(rw_a0[l]),
        "aup": _pad_rows(rw_a_up[l], 128, LORA_W).astype(BF16),
        "gup": _pad_rows(rw_g_up[l], 256, 0).astype(BF16),
        "k_k": row(rw_k_k[l]),
        "k_a": row(rw_k_a[l]),
        "r_k": row(rw_r_k[l]),
        "vbias": row(vres_bias[l - 1]) if l > 0 else jnp.zeros((1, D_RWKV), F32),
        "vup": (_pad_rows(vres_up[l - 1], 256, SM_VRES - 128) if l > 0
                else jnp.zeros((256, D_RWKV), F32)).astype(BF16),
        "convw": ml_conv_w[l],
        "convb": row(ml_conv_b[l]),
    }
    gate_bias = jnp.concatenate([ml_i_bias[l], ml_f_bias[l]])
    p["if_brow"] = jnp.zeros((1, 128), F32).at[0, IF_LANE:IF_LANE + 8].set(gate_bias)
    p["if_bcol"] = gate_bias.reshape(8, 1)
    return p


def kernel(x, ffn1_norm, ffn1_w_in, ffn1_w_out, mix_norm, w_in, shift_mu, rw_w0, rw_w_up, rw_a0, rw_a_up, rw_g_up, rw_k_k, rw_k_a, rw_r_k, rw_gn_w, rw_gn_b, vres_down, vres_up, vres_bias, ml_conv_w, ml_conv_b, ml_i_bias, ml_f_bias, ml_norm_w, br_a, br_b, w_out, ffn2_norm, ffn2_w_in, ffn2_w_out, final_norm):
    B, T, D = x.shape
    assert D == D_MODEL and T % CHUNK == 0
    n = B * T
    lane = jnp.arange(D_RWKV) // RWKV_HEAD
    seg = (lane[:, None] == lane[None, :]).astype(BF16)
    fin = final_norm.reshape(1, D_MODEL)
    h = x.reshape(n, D)
    v_first = None
    for l in range(DEPTH):
        h = _ffn_call(h, *_ffn_params(ffn1_norm[l], ffn1_w_in[l], ffn1_w_out[l]))
        p = _mixer_params(l, mix_norm, w_in, shift_mu, rw_w0, rw_w_up, rw_a0, rw_a_up, rw_g_up, rw_k_k,
                          rw_k_a, rw_r_k, vres_down, vres_up, vres_bias, ml_conv_w, ml_conv_b,
                          ml_i_bias, ml_f_bias)
        p["seg"] = seg
        (r, k, v, ld, kk, a, g, bonus, q, mk, mv, o, ifb, sg) = _mix_in_call(
            h.reshape(B, T, D), v_first, p)
        if l == 0:
            v_first = v
        ya = _rwkv_call(r, k, v, ld, kk, a, g, bonus, rw_gn_w[l].reshape(1, -1),
                        rw_gn_b[l].reshape(1, -1), seg)
        yb = _mlstm_call(q, mk, mv, o, ifb, p["if_brow"], p["if_bcol"], ml_norm_w[l].reshape(1, -1))
        h = _merge_ffn_call(h, ya.reshape(n, -1), yb.reshape(n, -1), sg.reshape(n, -1),
                            br_a[l].astype(BF16), br_b[l].astype(BF16), w_out[l].astype(BF16),
                            *_ffn_params(ffn2_norm[l], ffn2_w_in[l], ffn2_w_out[l]), fin,
                            final=(l == DEPTH - 1))
    return h.reshape(B, T, D)
```

```python
import functools

import jax
import jax.numpy as jnp
from jax import lax
from jax.experimental import pallas as pl
from jax.experimental.pallas import tpu as pltpu

F32 = jnp.float32
BF16 = jnp.bfloat16

D_MODEL = 1024
DEPTH = 4
CHUNK = 64
D_RWKV = 512
RWKV_HEAD = 64
RWKV_HEADS = 8
LORA_W, LORA_A, LORA_V, LORA_G = 64, 64, 32, 160
RWKV_GN_EPS = 64e-5
D_MLSTM = 512
MLSTM_HEADS = 4
MLSTM_HEAD = 128
CONV_W = 4
MLSTM_NORM_EPS = 1e-5
D_FF = 2816
NORM_EPS = 1e-6
N_RW_IN = 3 * D_RWKV + LORA_W + LORA_A + LORA_G
N_ML_IN = 4 * D_MLSTM + 2 * MLSTM_HEADS

SM_W = 384
SM_IF = 288
SM_VRES = 296
IF_LANE = SM_IF - 256

FF_CHUNK = 256
TM_FFN = 512
TT_MIX = 256
TT_REC = 256
VMEM_LIMIT = 56 * 1024 * 1024

HI = lax.Precision.HIGHEST


def _const_spec(shape):
    nd = len(shape)
    return pl.BlockSpec(shape, lambda *_: (0,) * nd, pipeline_mode=pl.Buffered(1))


def _rms(h, g):
    return h * lax.rsqrt(jnp.mean(h * h, axis=-1, keepdims=True) + NORM_EPS) * g


def _sigmoid(x):
    return 1.0 / (1.0 + jnp.exp(-x))


def _dot(a, b):
    return jnp.dot(a, b, preferred_element_type=F32)


def _dot_hi(a, b):
    return jnp.dot(a, b, preferred_element_type=F32, precision=HI)


def _dot_nt_hi(a, b):
    return lax.dot_general(a, b, (((1,), (1,)), ((), ())), preferred_element_type=F32, precision=HI)


def _dot_tn_hi(a, b):
    return lax.dot_general(a, b, (((0,), (0,)), ((), ())), preferred_element_type=F32, precision=HI)


def _seg_sum(x, seg_ref):
    return _dot(x.astype(BF16), seg_ref[...])


def _ffn_body(h, g_ref, wg_ref, wu_ref, wo_ref, acc_ref):
    xb = _rms(h, g_ref[...]).astype(BF16)
    nc = D_FF // FF_CHUNK
    for c in range(nc):
        gate = _dot(xb, wg_ref[c])
        up = _dot(xb, wu_ref[c])
        act = (gate * _sigmoid(gate) * up).astype(BF16)
        part = _dot(act, wo_ref[c])
        if c == 0:
            acc_ref[...] = part
        elif c < nc - 1:
            acc_ref[...] += part
    return h + 0.5 * (acc_ref[...] + part)


def _ffn_kernel(h_ref, g_ref, wg_ref, wu_ref, wo_ref, out_ref, acc_ref):
    out_ref[...] = _ffn_body(h_ref[...], g_ref, wg_ref, wu_ref, wo_ref, acc_ref)


def _merge_ffn_kernel(h_ref, ya_ref, yb_ref, sg_ref, bra_ref, brb_ref, wout_ref,
                      g_ref, wg_ref, wu_ref, wo_ref, fin_ref, out_ref, acc_ref, *, final):
    sg = sg_ref[...]
    pa = _dot(ya_ref[...].astype(BF16), bra_ref[...])
    pb = _dot(yb_ref[...].astype(BF16), brb_ref[...])
    u = sg[:, :D_MODEL] * pa + sg[:, D_MODEL:] * pb
    h = h_ref[...] + _dot(u.astype(BF16), wout_ref[...])
    h = _ffn_body(h, g_ref, wg_ref, wu_ref, wo_ref, acc_ref)
    if final:
        h = _rms(h, fin_ref[...])
    out_ref[...] = h


def _ffn_weight_specs():
    nc = D_FF // FF_CHUNK
    return [_const_spec((1, D_MODEL)), _const_spec((nc, D_MODEL, FF_CHUNK)),
            _const_spec((nc, D_MODEL, FF_CHUNK)), _const_spec((nc, FF_CHUNK, D_MODEL))]


def _ffn_call(h, g, wg, wu, wo):
    n = h.shape[0]
    tm = min(TM_FFN, n)
    row = pl.BlockSpec((tm, D_MODEL), lambda i: (i, 0))
    return pl.pallas_call(
        _ffn_kernel,
        grid=(n // tm,),
        in_specs=[row] + _ffn_weight_specs(),
        out_specs=row,
        out_shape=jax.ShapeDtypeStruct((n, D_MODEL), F32),
        scratch_shapes=[pltpu.VMEM((tm, D_MODEL), F32)],
        compiler_params=pltpu.CompilerParams(dimension_semantics=("arbitrary",),
                                             vmem_limit_bytes=VMEM_LIMIT),
        name="ffn",
    )(h, g, wg, wu, wo)


def _merge_ffn_call(h, ya, yb, sg, bra, brb, wout, g, wg, wu, wo, fin, final):
    n = h.shape[0]
    tm = min(TM_FFN, n)
    row = lambda w: pl.BlockSpec((tm, w), lambda i: (i, 0))
    return pl.pallas_call(
        functools.partial(_merge_ffn_kernel, final=final),
        grid=(n // tm,),
        in_specs=[row(D_MODEL), row(D_RWKV), row(D_MLSTM), row(2 * D_MODEL),
                  _const_spec((D_RWKV, D_MODEL)), _const_spec((D_MLSTM, D_MODEL)),
                  _const_spec((D_MODEL, D_MODEL))] + _ffn_weight_specs() + [_const_spec((1, D_MODEL))],
        out_specs=row(D_MODEL),
        out_shape=jax.ShapeDtypeStruct((n, D_MODEL), F32),
        scratch_shapes=[pltpu.VMEM((tm, D_MODEL), F32)],
        compiler_params=pltpu.CompilerParams(dimension_semantics=("arbitrary",),
                                             vmem_limit_bytes=VMEM_LIMIT),
        name="merge_ffn",
    )(h, ya, yb, sg, bra, brb, wout, g, wg, wu, wo, fin)


def _shift_rows(z, carry8, s, row8):
    rolled = pltpu.roll(z, s, axis=0)
    top = jnp.where(row8 < s, pltpu.roll(carry8, s, axis=0), rolled[:8])
    return jnp.concatenate([top, rolled[8:]], axis=0)


def _mix_in_kernel(*refs, has_vres):
    if has_vres:
        h_ref, vfirst_ref = refs[0], refs[1]
        refs = refs[2:]
    else:
        h_ref, vfirst_ref = refs[0], None
        refs = refs[1:]
    (norm_ref, w_rkv_ref, w_sm_ref, w_ml_ref, w_gate_ref, mu_rkv_ref, mu_sm_ref,
     w0_ref, wup_ref, a0_ref, aup_ref, gup_ref, kk_ref, ka_ref, rk_ref,
     vbias_ref, vup_ref, convw_ref, convb_ref, seg_ref,
     r_o, k_o, v_o, ld_o, kk_o, a_o, g_o, bonus_o, q_o, mk_o, mv_o, o_o, if_o, sg_o,
     c_rkv, c_sm, c_qk) = refs

    @pl.when(pl.program_id(1) == 0)
    def _():
        c_rkv[...] = jnp.zeros_like(c_rkv)
        c_sm[...] = jnp.zeros_like(c_sm)
        c_qk[...] = jnp.zeros_like(c_qk)

    tt = h_ref.shape[0]
    row8 = lax.broadcasted_iota(jnp.int32, (8, 1), 0)
    xb = _rms(h_ref[...], norm_ref[...]).astype(BF16)

    z_rkv = _dot(xb, w_rkv_ref[...])
    z_sm = _dot(xb, w_sm_ref[...])
    z_ml = _dot(xb, w_ml_ref[...])
    z_gate = _dot(xb, w_gate_ref[...])
    zs_rkv = _shift_rows(z_rkv, c_rkv[...], 1, row8)
    zs_sm = _shift_rows(z_sm, c_sm[...], 1, row8)
    c_rkv[...] = z_rkv[tt - 8:]
    c_sm[...] = z_sm[tt - 8:]
    m_rkv = z_rkv + mu_rkv_ref[...] * (zs_rkv - z_rkv)
    m_sm = z_sm + mu_sm_ref[...] * (zs_sm - z_sm)
    r = m_rkv[:, :D_RWKV]
    k = m_rkv[:, D_RWKV:2 * D_RWKV]
    v = m_rkv[:, 2 * D_RWKV:]
    xwa = m_sm[:, :128]
    xg_blk = m_sm[:, 128:]

    if has_vres:
        vz = _dot(z_sm[:, 128:].astype(BF16), vup_ref[...])
        vg = _sigmoid(vbias_ref[...] + vz)
        v = v + (vfirst_ref[...] - v) * vg

    y = -(w0_ref[...] + _dot(jnp.tanh(xwa).astype(BF16), wup_ref[...]))
    softplus = jnp.maximum(y, 0.0) + jnp.log(1.0 + jnp.exp(-jnp.abs(y)))
    w_log = -softplus - 0.5
    ld_o[...] = -jnp.exp(w_log)
    a = _sigmoid(a0_ref[...] + _dot(xwa.astype(BF16), aup_ref[...]))
    g_o[...] = _dot(_sigmoid(xg_blk).astype(BF16), gup_ref[...])
    kk = k * kk_ref[...]
    kk = kk / jnp.maximum(jnp.sqrt(_seg_sum(kk * kk, seg_ref)), 1e-12)
    k2 = k * (1.0 + (a - 1.0) * ka_ref[...])
    bonus_o[...] = _seg_sum(r * k2 * rk_ref[...], seg_ref) * v
    r_o[...] = r
    k_o[...] = k2
    v_o[...] = v
    kk_o[...] = kk
    a_o[...] = a

    zqk = z_ml[:, :2 * D_MLSTM]
    cw = convw_ref[...]
    conv = convb_ref[...] + cw[CONV_W - 1:CONV_W] * zqk
    carry = c_qk[...]
    for s in range(1, CONV_W):
        conv = conv + cw[CONV_W - 1 - s:CONV_W - s] * _shift_rows(zqk, carry, s, row8)
    c_qk[...] = zqk[tt - 8:]
    qk = conv * _sigmoid(conv)
    q_o[...] = qk[:, :D_MLSTM] * (MLSTM_HEAD ** -0.5)
    mk_o[...] = qk[:, D_MLSTM:]
    mv_o[...] = z_ml[:, 2 * D_MLSTM:3 * D_MLSTM]
    o_o[...] = _sigmoid(z_ml[:, 3 * D_MLSTM:])
    if_o[...] = z_sm[:, 256:]

    sg_o[...] = _sigmoid(z_gate)


def _mix_in_call(h3, vfirst, p):
    B, T, _ = h3.shape
    tt = min(TT_MIX, T)
    has_vres = vfirst is not None
    tok = lambda w: pl.BlockSpec((None, tt, w), lambda b, t: (b, t, 0))
    in_specs = [tok(D_MODEL)] + ([tok(D_RWKV)] if has_vres else [])
    weights = [p["mix_norm"], p["w_rkv"], p["w_sm"], p["w_ml"], p["w_gate"], p["mu_rkv"], p["mu_sm"],
               p["w0"], p["wup"], p["a0"], p["aup"], p["gup"], p["k_k"], p["k_a"], p["r_k"],
               p["vbias"], p["vup"], p["convw"], p["convb"], p["seg"]]
    in_specs += [_const_spec(w.shape) for w in weights]
    out_w = [D_RWKV] * 8 + [D_MLSTM] * 4 + [128, 2 * D_MODEL]
    out_specs = [tok(w) for w in out_w]
    out_shape = [jax.ShapeDtypeStruct((B, T, w), F32) for w in out_w]
    args = [h3] + ([vfirst] if has_vres else []) + weights
    return pl.pallas_call(
        functools.partial(_mix_in_kernel, has_vres=has_vres),
        grid=(B, T // tt),
        in_specs=in_specs,
        out_specs=out_specs,
        out_shape=out_shape,
        scratch_shapes=[pltpu.VMEM((8, 3 * D_RWKV), F32), pltpu.VMEM((8, SM_W), F32),
                        pltpu.VMEM((8, 2 * D_MLSTM), F32)],
        compiler_params=pltpu.CompilerParams(dimension_semantics=("arbitrary", "arbitrary"),
                                             vmem_limit_bytes=VMEM_LIMIT),
        name="mix_in",
    )(*args)


def _bdot(a, b):
    return jnp.dot(a.astype(BF16), b.astype(BF16), preferred_element_type=F32)


def _bdot_nt(a, b):
    return lax.dot_general(a.astype(BF16), b.astype(BF16), (((1,), (1,)), ((), ())),
                           preferred_element_type=F32)


def _bdot_tn(a, b):
    return lax.dot_general(a.astype(BF16), b.astype(BF16), (((0,), (0,)), ((), ())),
                           preferred_element_type=F32)


def _split3(x):
    hi = x.astype(BF16)
    r1 = x - hi.astype(F32)
    mid = r1.astype(BF16)
    lo = (r1 - mid.astype(F32)).astype(BF16)
    return hi, mid, lo


def _rwkv_kernel(r_ref, k_ref, v_ref, ld_ref, kk_ref, a_ref, g_ref, bonus_ref,
                 gnw_ref, gnb_ref, seg_ref, y_ref, h_ref):
    @pl.when(pl.program_id(1) == 0)
    def _():
        h_ref[...] = jnp.zeros_like(h_ref)

    C, N = CHUNK, RWKV_HEAD
    tt = r_ref.shape[0]
    ti = lax.broadcasted_iota(jnp.int32, (C, C), 0)
    si = lax.broadcasted_iota(jnp.int32, (C, C), 1)
    strict = si < ti
    incl = si <= ti
    eye = (si == ti).astype(F32)

    tr = lax.broadcasted_iota(jnp.int32, (tt, tt), 0)
    tc = lax.broadcasted_iota(jnp.int32, (tt, tt), 1)
    same = (tr // C) == (tc // C)
    tri = (same & (tc <= tr)).astype(BF16)
    blk = same.astype(BF16)
    lw = ld_ref[...]
    parts = _split3(lw)
    cs = sum(_dot(tri, p) for p in parts)
    cs_tot = sum(_dot(blk, p) for p in parts)

    kk = kk_ref[...]
    kka = kk * a_ref[...]
    k2 = k_ref[...]
    g_inv = jnp.exp(-cs)
    g_end = jnp.exp(cs_tot - cs)
    g_tot = jnp.exp(cs_tot)
    Rt = (r_ref[...] * jnp.exp(cs)).astype(BF16)
    At = (-kk * jnp.exp(cs - lw)).astype(BF16)
    Kb = (k2 * g_inv).astype(BF16)
    Bb = (kka * g_inv).astype(BF16)
    Ke = (k2 * g_end).astype(BF16)
    Be = (kka * g_end).astype(BF16)
    V = v_ref[...].astype(BF16)

    nc = tt // C
    inst = [(c, h) for c in range(nc) for h in range(RWKV_HEADS)]
    sl = lambda x, c, h: x[c * C:(c + 1) * C, h * N:(h + 1) * N]
    each = lambda f, *lists: [f(*xs) for xs in zip(*lists)]

    At_i = [sl(At, c, h) for c, h in inst]
    Rt_i = [sl(Rt, c, h) for c, h in inst]
    V_i = [sl(V, c, h) for c, h in inst]
    P = [_bdot_nt(jnp.concatenate([a, r], axis=0),
                  jnp.concatenate([sl(Kb, c, h), sl(Bb, c, h)], axis=0))
         for a, r, (c, h) in zip(At_i, Rt_i, inst)]
    A_ak = [jnp.where(strict, p[:C, :C], 0.0) for p in P]
    L = [jnp.where(strict, p[:C, C:], 0.0) for p in P]
    A_rk = [jnp.where(incl, p[C:, :C], 0.0) for p in P]
    A_rb = [jnp.where(incl, p[C:, C:], 0.0).astype(BF16) for p in P]
    AV = each(lambda ak, rk, v: _bdot(jnp.concatenate([ak, rk], axis=0), v), A_ak, A_rk, V_i)
    X = [eye + l for l in L]
    M = each(lambda l: _bdot(l, l), L)
    for _ in range(4):
        XM = each(lambda x, m: _bdot(jnp.concatenate([x, m], axis=0), m), X, M)
        X = each(lambda x, xm: x + xm[:C], X, XM)
        M = [xm[C:] for xm in XM]
    Tm = each(lambda x, m: x + _bdot(x, m), X, M)
    W = each(lambda t, a, av: _bdot(t, jnp.concatenate([a, av[:C].astype(BF16)], axis=1)),
             Tm, At_i, AV)
    Z = each(_bdot, A_rb, W)
    G = [_bdot_tn(sl(Be, c, h), w) for (c, h), w in zip(inst, W)]
    KV = [_bdot_tn(sl(Ke, c, h), v) for (c, h), v in zip(inst, V_i)]
    RM = [jnp.concatenate([r.astype(F32) + z[:, :N],
                           eye * g_tot[c * C:c * C + 1, h * N:(h + 1) * N] + g[:, :N]], axis=0).astype(BF16)
          for r, z, g, (c, h) in zip(Rt_i, Z, G, inst)]
    Y0 = each(lambda av, z: av[C:] + z[:, N:], AV, Z)
    G0 = each(lambda kv, g: kv + g[:, N:], KV, G)

    Hs = [h_ref[h] for h in range(RWKV_HEADS)]
    y_rows = []
    for c in range(nc):
        H_hi = [x.astype(BF16) for x in Hs]
        H_lo = [(x - hi.astype(F32)).astype(BF16) for x, hi in zip(Hs, H_hi)]
        RMH = [_dot(RM[c * RWKV_HEADS + h], H_hi[h]) + _dot(RM[c * RWKV_HEADS + h], H_lo[h])
               for h in range(RWKV_HEADS)]
        y_rows.append(jnp.concatenate([RMH[h][:C] + Y0[c * RWKV_HEADS + h]
                                       for h in range(RWKV_HEADS)], axis=1))
        Hs = [RMH[h][C:] + G0[c * RWKV_HEADS + h] for h in range(RWKV_HEADS)]
    for h in range(RWKV_HEADS):
        h_ref[h] = Hs[h]
    y = jnp.concatenate(y_rows, axis=0)
    inv_n = 1.0 / N
    mu = _seg_sum(y, seg_ref) * inv_n
    d = y - mu
    var = _seg_sum(d * d, seg_ref) * inv_n
    yn = d * lax.rsqrt(var + RWKV_GN_EPS)
    y_ref[...] = (yn * gnw_ref[...] + gnb_ref[...] + bonus_ref[...]) * g_ref[...]


def _rwkv_call(r, k, v, ld, kk, a, g, bonus, gnw, gnb, seg):
    B, T, _ = r.shape
    tt = min(TT_REC, T)
    tok = pl.BlockSpec((None, tt, D_RWKV), lambda b, t: (b, t, 0))
    return pl.pallas_call(
        _rwkv_kernel,
        grid=(B, T // tt),
        in_specs=[tok] * 8 + [_const_spec((1, D_RWKV)), _const_spec((1, D_RWKV)),
                              _const_spec((D_RWKV, D_RWKV))],
        out_specs=tok,
        out_shape=jax.ShapeDtypeStruct((B, T, D_RWKV), F32),
        scratch_shapes=[pltpu.VMEM((RWKV_HEADS, RWKV_HEAD, RWKV_HEAD), F32)],
        compiler_params=pltpu.CompilerParams(dimension_semantics=("arbitrary", "arbitrary"),
                                             vmem_limit_bytes=VMEM_LIMIT),
        name="rwkv7",
    )(r, k, v, ld, kk, a, g, bonus, gnw, gnb, seg)


def _log_sigmoid(x):
    return jnp.minimum(x, 0.0) - jnp.log(1.0 + jnp.exp(-jnp.abs(x)))


def _mlstm_kernel(q_ref, k_ref, v_ref, o_ref, if_ref, brow_ref, bcol_ref, nw_ref,
                  y_ref, cn_ref, m_ref):
    @pl.when(pl.program_id(1) == 0)
    def _():
        cn_ref[...] = jnp.zeros_like(cn_ref)
        m_ref[...] = jnp.zeros_like(m_ref)

    C, d = CHUNK, MLSTM_HEAD
    tt = q_ref.shape[0]
    nc = tt // C
    NH = MLSTM_HEADS
    ti = lax.broadcasted_iota(jnp.int32, (C, C), 0)
    si = lax.broadcasted_iota(jnp.int32, (C, C), 1)
    causal = si <= ti
    ones = jnp.ones((C, d), BF16)
    i0 = IF_LANE
    f0 = IF_LANE + NH

    tr = lax.broadcasted_iota(jnp.int32, (tt, tt), 0)
    tc = lax.broadcasted_iota(jnp.int32, (tt, tt), 1)
    same = (tr // C) == (tc // C)
    tril = (same & (tc <= tr)).astype(BF16)
    triu = (same & (tr <= tc)).astype(BF16)
    pre = if_ref[...] + brow_ref[...]
    bcum_cols = sum(_dot(tril, p) for p in _split3(_log_sigmoid(pre)))
    preT = jnp.transpose(if_ref[...])[i0:i0 + 8, :] + bcol_ref[...]
    row8 = lax.broadcasted_iota(jnp.int32, (8, 1), 0)
    bcum_rows = sum(_dot(p, triu) for p in _split3(jnp.where(row8 < NH, preT, _log_sigmoid(preT))))

    inst = [(c, h) for c in range(nc) for h in range(NH)]
    rs = lambda c: slice(c * C, (c + 1) * C)
    hs_ = lambda h: slice(h * d, (h + 1) * d)
    q_i = [q_ref[rs(c), hs_(h)].astype(BF16) for c, h in inst]
    k_i = [k_ref[rs(c), hs_(h)] for c, h in inst]
    ve_i = [jnp.concatenate([v_ref[rs(c), hs_(h)].astype(BF16), ones], axis=1) for c, h in inst]
    bc_i = [bcum_cols[rs(c), f0 + h:f0 + h + 1] for c, h in inst]
    e_c = [pre[rs(c), i0 + h:i0 + h + 1] - bc for (c, h), bc in zip(inst, bc_i)]
    e_r = [preT[h:h + 1, rs(c)] - bcum_rows[NH + h:NH + h + 1, rs(c)] for c, h in inst]
    a_col = [jnp.max(jnp.where(causal, e, -jnp.inf), axis=-1, keepdims=True) for e in e_r]
    D0 = [jnp.where(causal, jnp.exp(e - a), 0.0) for e, a in zip(e_r, a_col)]
    S0 = [(_bdot_nt(q, k) * dm).astype(BF16) for q, k, dm in zip(q_i, k_i, D0)]
    a_end = [a[C - 1:C, :] for a in a_col]
    KV0 = [_bdot_tn(jnp.exp(e - ae) * k, ve) for e, ae, k, ve in zip(e_c, a_end, k_i, ve_i)]
    intra = [_dot(s, ve) for s, ve in zip(S0, ve_i)]

    cn = [cn_ref[h] for h in range(NH)]
    m_prev = [m_ref[h][0:1, 0:1] for h in range(NH)]
    for c in range(nc):
        ii = [c * NH + h for h in range(NH)]
        QC = [_bdot(q_i[i], cn[h]) for h, i in enumerate(ii)]
        for h, i in enumerate(ii):
            mu_t = jnp.maximum(m_prev[h], a_col[i])
            tot = jnp.exp(m_prev[h] - mu_t) * QC[h] + jnp.exp(a_col[i] - mu_t) * intra[i]
            hh = tot[:, :d] / jnp.maximum(jnp.abs(tot[:, d:]), jnp.exp(-(bc_i[i] + mu_t)))
            mean = jnp.mean(hh, axis=-1, keepdims=True)
            dd = hh - mean
            var = jnp.mean(dd * dd, axis=-1, keepdims=True)
            y_ref[rs(c), hs_(h)] = o_ref[rs(c), hs_(h)] * (dd * lax.rsqrt(var + MLSTM_NORM_EPS)
                                                           * nw_ref[:, hs_(h)])
            mu_end = jnp.maximum(m_prev[h], a_end[i])
            cn[h] = jnp.exp(m_prev[h] - mu_end) * cn[h] + jnp.exp(a_end[i] - mu_end) * KV0[i]
            m_prev[h] = bc_i[i][C - 1:C, :] + mu_end
    for h in range(NH):
        cn_ref[h] = cn[h]
        m_ref[h] = jnp.broadcast_to(m_prev[h], (8, 128))


def _mlstm_call(q, k, v, o, ifb, brow, bcol, nw):
    B, T, _ = q.shape
    tt = min(TT_REC, T)
    tok = lambda w: pl.BlockSpec((None, tt, w), lambda b, t: (b, t, 0))
    return pl.pallas_call(
        _mlstm_kernel,
        grid=(B, T // tt),
        in_specs=[tok(D_MLSTM)] * 4 + [tok(128), _const_spec((1, 128)), _const_spec((8, 1)),
                                       _const_spec((1, D_MLSTM))],
        out_specs=tok(D_MLSTM),
        out_shape=jax.ShapeDtypeStruct((B, T, D_MLSTM), F32),
        scratch_shapes=[pltpu.VMEM((MLSTM_HEADS, MLSTM_HEAD, 2 * MLSTM_HEAD), F32),
                        pltpu.VMEM((MLSTM_HEADS, 8, 128), F32)],
        compiler_params=pltpu.CompilerParams(dimension_semantics=("arbitrary", "arbitrary"),
                                             vmem_limit_bytes=VMEM_LIMIT),
        name="mlstm",
    )(q, k, v, o, ifb, brow, bcol, nw)


def _pad_rows(w, rows, at=0):
    out = jnp.zeros((rows, w.shape[1]), w.dtype)
    return out.at[at:at + w.shape[0]].set(w)


def _ffn_params(norm, w_in, w_out):
    nc = D_FF // FF_CHUNK
    wg = w_in[:, :D_FF].reshape(D_MODEL, nc, FF_CHUNK).transpose(1, 0, 2).astype(BF16)
    wu = w_in[:, D_FF:].reshape(D_MODEL, nc, FF_CHUNK).transpose(1, 0, 2).astype(BF16)
    wo = w_out.reshape(nc, FF_CHUNK, D_MODEL).astype(BF16)
    return norm.reshape(1, D_MODEL), wg, wu, wo


def _mixer_params(l, mix_norm, w_in, shift_mu, rw_w0, rw_w_up, rw_a0, rw_a_up, rw_g_up, rw_k_k, rw_k_a,
                  rw_r_k, vres_down, vres_up, vres_bias, ml_conv_w, ml_conv_b, ml_i_bias, ml_f_bias):
    w = w_in[l]
    o_ml = N_RW_IN
    o_gate = N_RW_IN + N_ML_IN
    w_rkv = w[:, :3 * D_RWKV]
    w_lora = w[:, 3 * D_RWKV:N_RW_IN]
    w_if = w[:, o_ml + 4 * D_MLSTM:o_gate]
    w_vd = vres_down[l - 1] if l > 0 else jnp.zeros((D_MODEL, LORA_V), F32)
    pad = jnp.zeros((D_MODEL, SM_W - SM_VRES - LORA_V), F32)
    w_sm = jnp.concatenate([w_lora, w_if, w_vd, pad], axis=1)
    mu = shift_mu[l]
    mu_sm = jnp.concatenate([mu[3 * D_RWKV:], jnp.zeros((SM_W - (N_RW_IN - 3 * D_RWKV),), F32)])
    row = lambda x: x.reshape(1, -1)
    p = {
        "mix_norm": row(mix_norm[l]),
        "w_rkv": w_rkv.astype(BF16),
        "w_sm": w_sm.astype(BF16),
        "w_ml": w[:, o_ml:o_ml + 4 * D_MLSTM].astype(BF16),
        "w_gate": w[:, o_gate:].astype(BF16),
        "mu_rkv": row(mu[:3 * D_RWKV]),
        "mu_sm": row(mu_sm),
        "w0": row(rw_w0[l]),
        "wup": _pad_rows(rw_w_up[l], 128, 0).astype(BF16),
        "a0": row(rw_a0[l]),
        "aup": _pad_rows(rw_a_up[l], 128, LORA_W).astype(BF16),
        "gup": _pad_rows(rw_g_up[l], 256, 0).astype(BF16),
        "k_k": row(rw_k_k[l]),
        "k_a": row(rw_k_a[l]),
        "r_k": row(rw_r_k[l]),
        "vbias": row(vres_bias[l - 1]) if l > 0 else jnp.zeros((1, D_RWKV), F32),
        "vup": (_pad_rows(vres_up[l - 1], 256, SM_VRES - 128) if l > 0
                else jnp.zeros((256, D_RWKV), F32)).astype(BF16),
        "convw": ml_conv_w[l],
        "convb": row(ml_conv_b[l]),
    }
    gate_bias = jnp.concatenate([ml_i_bias[l], ml_f_bias[l]])
    p["if_brow"] = jnp.zeros((1, 128), F32).at[0, IF_LANE:IF_LANE + 8].set(gate_bias)
    p["if_bcol"] = gate_bias.reshape(8, 1)
    return p


def kernel(x, ffn1_norm, ffn1_w_in, ffn1_w_out, mix_norm, w_in, shift_mu, rw_w0, rw_w_up, rw_a0, rw_a_up, rw_g_up, rw_k_k, rw_k_a, rw_r_k, rw_gn_w, rw_gn_b, vres_down, vres_up, vres_bias, ml_conv_w, ml_conv_b, ml_i_bias, ml_f_bias, ml_norm_w, br_a, br_b, w_out, ffn2_norm, ffn2_w_in, ffn2_w_out, final_norm):
    B, T, D = x.shape
    assert D == D_MODEL and T % CHUNK == 0
    n = B * T
    lane = jnp.arange(D_RWKV) // RWKV_HEAD
    seg = (lane[:, None] == lane[None, :]).astype(BF16)
    fin = final_norm.reshape(1, D_MODEL)
    h = x.reshape(n, D)
    v_first = None
    for l in range(DEPTH):
        h = _ffn_call(h, *_ffn_params(ffn1_norm[l], ffn1_w_in[l], ffn1_w_out[l]))
        p = _mixer_params(l, mix_norm, w_in, shift_mu, rw_w0, rw_w_up, rw_a0, rw_a_up, rw_g_up, rw_k_k,
                          rw_k_a, rw_r_k, vres_down, vres_up, vres_bias, ml_conv_w, ml_conv_b,
                          ml_i_bias, ml_f_bias)
        p["seg"] = seg
        (r, k, v, ld, kk, a, g, bonus, q, mk, mv, o, ifb, sg) = _mix_in_call(
            h.reshape(B, T, D), v_first, p)
        if l == 0:
            v_first = v
        ya = _rwkv_call(r, k, v, ld, kk, a, g, bonus, rw_gn_w[l].reshape(1, -1),
                        rw_gn_b[l].reshape(1, -1), seg)
        yb = _mlstm_call(q, mk, mv, o, ifb, p["if_brow"], p["if_bcol"], ml_norm_w[l].reshape(1, -1))
        h = _merge_ffn_call(h, ya.reshape(n, -1), yb.reshape(n, -1), sg.reshape(n, -1),
                            br_a[l].astype(BF16), br_b[l].astype(BF16), w_out[l].astype(BF16),
                            *_ffn_params(ffn2_norm[l], ffn2_w_in[l], ffn2_w_out[l]), fin,
                            final=(l == DEPTH - 1))
    return h.reshape(B, T, D)
```

```python
import functools

import jax
import jax.numpy as jnp
from jax import lax
from jax.experimental import pallas as pl
from jax.experimental.pallas import tpu as pltpu

F32 = jnp.float32
BF16 = jnp.bfloat16

D_MODEL = 1024
DEPTH = 4
CHUNK = 64
D_RWKV = 512
RWKV_HEAD = 64
RWKV_HEADS = 8
LORA_W, LORA_A, LORA_V, LORA_G = 64, 64, 32, 160
RWKV_GN_EPS = 64e-5
D_MLSTM = 512
MLSTM_HEADS = 4
MLSTM_HEAD = 128
CONV_W = 4
MLSTM_NORM_EPS = 1e-5
D_FF = 2816
NORM_EPS = 1e-6
N_RW_IN = 3 * D_RWKV + LORA_W + LORA_A + LORA_G
N_ML_IN = 4 * D_MLSTM + 2 * MLSTM_HEADS

SM_W = 384
SM_IF = 288
SM_VRES = 296
IF_LANE = SM_IF - 256

FF_CHUNK = 256
TM_FFN = 512
TT_MIX = 256
TT_REC = 256
VMEM_LIMIT = 56 * 1024 * 1024

HI = lax.Precision.HIGHEST


def _const_spec(shape):
    nd = len(shape)
    return pl.BlockSpec(shape, lambda *_: (0,) * nd, pipeline_mode=pl.Buffered(1))


def _rms(h, g):
    return h * lax.rsqrt(jnp.mean(h * h, axis=-1, keepdims=True) + NORM_EPS) * g


def _sigmoid(x):
    return 1.0 / (1.0 + jnp.exp(-x))


def _dot(a, b):
    return jnp.dot(a, b, preferred_element_type=F32)


def _dot_hi(a, b):
    return jnp.dot(a, b, preferred_element_type=F32, precision=HI)


def _dot_nt_hi(a, b):
    return lax.dot_general(a, b, (((1,), (1,)), ((), ())), preferred_element_type=F32, precision=HI)


def _dot_tn_hi(a, b):
    return lax.dot_general(a, b, (((0,), (0,)), ((), ())), preferred_element_type=F32, precision=HI)


def _seg_sum(x, seg_ref):
    return _dot(x.astype(BF16), seg_ref[...])


def _ffn_body(h, g_ref, wg_ref, wu_ref, wo_ref, acc_ref):
    xb = _rms(h, g_ref[...]).astype(BF16)
    nc = D_FF // FF_CHUNK
    for c in range(nc):
        gate = _dot(xb, wg_ref[c])
        up = _dot(xb, wu_ref[c])
        act = (gate * _sigmoid(gate) * up).astype(BF16)
        part = _dot(act, wo_ref[c])
        if c == 0:
            acc_ref[...] = part
        elif c < nc - 1:
            acc_ref[...] += part
    return h + 0.5 * (acc_ref[...] + part)


def _ffn_kernel(h_ref, g_ref, wg_ref, wu_ref, wo_ref, out_ref, acc_ref):
    out_ref[...] = _ffn_body(h_ref[...], g_ref, wg_ref, wu_ref, wo_ref, acc_ref)


def _merge_ffn_kernel(h_ref, ya_ref, yb_ref, sg_ref, bra_ref, brb_ref, wout_ref,
                      g_ref, wg_ref, wu_ref, wo_ref, fin_ref, out_ref, acc_ref, *, final):
    sg = sg_ref[...]
    pa = _dot(ya_ref[...].astype(BF16), bra_ref[...])
    pb = _dot(yb_ref[...].astype(BF16), brb_ref[...])
    u = sg[:, :D_MODEL] * pa + sg[:, D_MODEL:] * pb
    h = h_ref[...] + _dot(u.astype(BF16), wout_ref[...])
    h = _ffn_body(h, g_ref, wg_ref, wu_ref, wo_ref, acc_ref)
    if final:
        h = _rms(h, fin_ref[...])
    out_ref[...] = h


def _ffn_weight_specs():
    nc = D_FF // FF_CHUNK
    return [_const_spec((1, D_MODEL)), _const_spec((nc, D_MODEL, FF_CHUNK)),
            _const_spec((nc, D_MODEL, FF_CHUNK)), _const_spec((nc, FF_CHUNK, D_MODEL))]


def _ffn_call(h, g, wg, wu, wo):
    n = h.shape[0]
    tm = min(TM_FFN, n)
    row = pl.BlockSpec((tm, D_MODEL), lambda i: (i, 0))
    return pl.pallas_call(
        _ffn_kernel,
        grid=(n // tm,),
        in_specs=[row] + _ffn_weight_specs(),
        out_specs=row,
        out_shape=jax.ShapeDtypeStruct((n, D_MODEL), F32),
        scratch_shapes=[pltpu.VMEM((tm, D_MODEL), F32)],
        compiler_params=pltpu.CompilerParams(dimension_semantics=("arbitrary",),
                                             vmem_limit_bytes=VMEM_LIMIT),
        name="ffn",
    )(h, g, wg, wu, wo)


def _merge_ffn_call(h, ya, yb, sg, bra, brb, wout, g, wg, wu, wo, fin, final):
    n = h.shape[0]
    tm = min(TM_FFN, n)
    row = lambda w: pl.BlockSpec((tm, w), lambda i: (i, 0))
    return pl.pallas_call(
        functools.partial(_merge_ffn_kernel, final=final),
        grid=(n // tm,),
        in_specs=[row(D_MODEL), row(D_RWKV), row(D_MLSTM), row(2 * D_MODEL),
                  _const_spec((D_RWKV, D_MODEL)), _const_spec((D_MLSTM, D_MODEL)),
                  _const_spec((D_MODEL, D_MODEL))] + _ffn_weight_specs() + [_const_spec((1, D_MODEL))],
        out_specs=row(D_MODEL),
        out_shape=jax.ShapeDtypeStruct((n, D_MODEL), F32),
        scratch_shapes=[pltpu.VMEM((tm, D_MODEL), F32)],
        compiler_params=pltpu.CompilerParams(dimension_semantics=("arbitrary",),
                                             vmem_limit_bytes=VMEM_LIMIT),
        name="merge_ffn",
    )(h, ya, yb, sg, bra, brb, wout, g, wg, wu, wo, fin)


def _stage_rows(buf, z):
    buf[8:, :] = z


def _carry_rows(buf, z):
    buf[:8, :] = z[z.shape[0] - 8:]


def _mix_in_kernel(*refs, has_vres):
    if has_vres:
        h_ref, vfirst_ref = refs[0], refs[1]
        refs = refs[2:]
    else:
        h_ref, vfirst_ref = refs[0], None
        refs = refs[1:]
    (norm_ref, w_rkv_ref, w_sm_ref, w_ml_ref, w_gate_ref, mu_rkv_ref, mu_sm_ref,
     w0_ref, wup_ref, a0_ref, aup_ref, gup_ref, kk_ref, ka_ref, rk_ref,
     vbias_ref, vup_ref, convw_ref, convb_ref, seg_ref,
     r_o, k_o, v_o, ld_o, kk_o, a_o, g_o, bonus_o, q_o, mk_o, mv_o, o_o, if_o, sg_o,
     c_rkv, c_sm, c_qk) = refs

    @pl.when(pl.program_id(1) == 0)
    def _():
        c_rkv[:8, :] = jnp.zeros((8, c_rkv.shape[1]), F32)
        c_sm[:8, :] = jnp.zeros((8, c_sm.shape[1]), F32)
        c_qk[:8, :] = jnp.zeros((8, c_qk.shape[1]), F32)

    tt = h_ref.shape[0]
    xb = _rms(h_ref[...], norm_ref[...]).astype(BF16)

    z_rkv = _dot(xb, w_rkv_ref[...])
    z_sm = _dot(xb, w_sm_ref[...])
    z_ml = _dot(xb, w_ml_ref[...])
    z_gate = _dot(xb, w_gate_ref[...])
    _stage_rows(c_rkv, z_rkv)
    _stage_rows(c_sm, z_sm)
    zs_rkv = c_rkv[7:7 + tt, :]
    zs_sm = c_sm[7:7 + tt, :]
    _carry_rows(c_rkv, z_rkv)
    _carry_rows(c_sm, z_sm)
    m_rkv = z_rkv + mu_rkv_ref[...] * (zs_rkv - z_rkv)
    m_sm = z_sm + mu_sm_ref[...] * (zs_sm - z_sm)
    r = m_rkv[:, :D_RWKV]
    k = m_rkv[:, D_RWKV:2 * D_RWKV]
    v = m_rkv[:, 2 * D_RWKV:]
    xwa = m_sm[:, :128]
    xg_blk = m_sm[:, 128:]

    if has_vres:
        vz = _dot(z_sm[:, 128:].astype(BF16), vup_ref[...])
        vg = _sigmoid(vbias_ref[...] + vz)
        v = v + (vfirst_ref[...] - v) * vg

    y = -(w0_ref[...] + _dot(jnp.tanh(xwa).astype(BF16), wup_ref[...]))
    softplus = jnp.maximum(y, 0.0) + jnp.log(1.0 + jnp.exp(-jnp.abs(y)))
    w_log = -softplus - 0.5
    ld_o[...] = -jnp.exp(w_log)
    a = _sigmoid(a0_ref[...] + _dot(xwa.astype(BF16), aup_ref[...]))
    g_o[...] = _dot(_sigmoid(xg_blk).astype(BF16), gup_ref[...])
    kk = k * kk_ref[...]
    kk = kk / jnp.maximum(jnp.sqrt(_seg_sum(kk * kk, seg_ref)), 1e-12)
    k2 = k * (1.0 + (a - 1.0) * ka_ref[...])
    bonus_o[...] = _seg_sum(r * k2 * rk_ref[...], seg_ref) * v
    r_o[...] = r
    k_o[...] = k2
    v_o[...] = v
    kk_o[...] = kk
    a_o[...] = a

    zqk = z_ml[:, :2 * D_MLSTM]
    cw = convw_ref[...]
    conv = convb_ref[...] + cw[CONV_W - 1:CONV_W] * zqk
    _stage_rows(c_qk, zqk)
    for s in range(1, CONV_W):
        conv = conv + cw[CONV_W - 1 - s:CONV_W - s] * c_qk[8 - s:8 - s + tt, :]
    _carry_rows(c_qk, zqk)
    qk = conv * _sigmoid(conv)
    q_o[...] = qk[:, :D_MLSTM] * (MLSTM_HEAD ** -0.5)
    mk_o[...] = qk[:, D_MLSTM:]
    mv_o[...] = z_ml[:, 2 * D_MLSTM:3 * D_MLSTM]
    o_o[...] = _sigmoid(z_ml[:, 3 * D_MLSTM:])
    if_o[...] = z_sm[:, 256:]

    sg_o[...] = _sigmoid(z_gate)


def _mix_in_call(h3, vfirst, p):
    B, T, _ = h3.shape
    tt = min(TT_MIX, T)
    has_vres = vfirst is not None
    tok = lambda w: pl.BlockSpec((None, tt, w), lambda b, t: (b, t, 0))
    in_specs = [tok(D_MODEL)] + ([tok(D_RWKV)] if has_vres else [])
    weights = [p["mix_norm"], p["w_rkv"], p["w_sm"], p["w_ml"], p["w_gate"], p["mu_rkv"], p["mu_sm"],
               p["w0"], p["wup"], p["a0"], p["aup"], p["gup"], p["k_k"], p["k_a"], p["r_k"],
               p["vbias"], p["vup"], p["convw"], p["convb"], p["seg"]]
    in_specs += [_const_spec(w.shape) for w in weights]
    out_w = [D_RWKV] * 8 + [D_MLSTM] * 4 + [128, 2 * D_MODEL]
    out_specs = [tok(w) for w in out_w]
    out_shape = [jax.ShapeDtypeStruct((B, T, w), F32) for w in out_w]
    args = [h3] + ([vfirst] if has_vres else []) + weights
    return pl.pallas_call(
        functools.partial(_mix_in_kernel, has_vres=has_vres),
        grid=(B, T // tt),
        in_specs=in_specs,
        out_specs=out_specs,
        out_shape=out_shape,
        scratch_shapes=[pltpu.VMEM((tt + 8, 3 * D_RWKV), F32), pltpu.VMEM((tt + 8, SM_W), F32),
                        pltpu.VMEM((tt + 8, 2 * D_MLSTM), F32)],
        compiler_params=pltpu.CompilerParams(dimension_semantics=("arbitrary", "arbitrary"),
                                             vmem_limit_bytes=VMEM_LIMIT),
        name="mix_in",
    )(*args)


def _bdot(a, b):
    return jnp.dot(a.astype(BF16), b.astype(BF16), preferred_element_type=F32)


def _bdot_nt(a, b):
    return lax.dot_general(a.astype(BF16), b.astype(BF16), (((1,), (1,)), ((), ())),
                           preferred_element_type=F32)


def _bdot_tn(a, b):
    return lax.dot_general(a.astype(BF16), b.astype(BF16), (((0,), (0,)), ((), ())),
                           preferred_element_type=F32)


def _split3(x):
    hi = x.astype(BF16)
    r1 = x - hi.astype(F32)
    mid = r1.astype(BF16)
    lo = (r1 - mid.astype(F32)).astype(BF16)
    return hi, mid, lo


def _rwkv_kernel(r_ref, k_ref, v_ref, ld_ref, kk_ref, a_ref, g_ref, bonus_ref,
                 gnw_ref, gnb_ref, seg_ref, y_ref, h_ref):
    @pl.when(pl.program_id(1) == 0)
    def _():
        h_ref[...] = jnp.zeros_like(h_ref)

    C, N = CHUNK, RWKV_HEAD
    tt = r_ref.shape[0]
    ti = lax.broadcasted_iota(jnp.int32, (C, C), 0)
    si = lax.broadcasted_iota(jnp.int32, (C, C), 1)
    strict = si < ti
    incl = si <= ti
    eye = (si == ti).astype(F32)

    tr = lax.broadcasted_iota(jnp.int32, (tt, tt), 0)
    tc = lax.broadcasted_iota(jnp.int32, (tt, tt), 1)
    same = (tr // C) == (tc // C)
    tri = (same & (tc <= tr)).astype(BF16)
    blk = same.astype(BF16)
    lw = ld_ref[...]
    parts = _split3(lw)
    cs = sum(_dot(tri, p) for p in parts)
    cs_tot = sum(_dot(blk, p) for p in parts)

    kk = kk_ref[...]
    kka = kk * a_ref[...]
    k2 = k_ref[...]
    g_inv = jnp.exp(-cs)
    g_end = jnp.exp(cs_tot - cs)
    g_tot = jnp.exp(cs_tot)
    Rt = (r_ref[...] * jnp.exp(cs)).astype(BF16)
    At = (-kk * jnp.exp(cs - lw)).astype(BF16)
    Kb = (k2 * g_inv).astype(BF16)
    Bb = (kka * g_inv).astype(BF16)
    Ke = (k2 * g_end).astype(BF16)
    Be = (kka * g_end).astype(BF16)
    V = v_ref[...].astype(BF16)

    QW = 4 * N
    nq = D_RWKV // QW
    nc = tt // C
    lane = lax.broadcasted_iota(jnp.int32, (C, QW), 1)
    row = lax.broadcasted_iota(jnp.int32, (C, QW), 0)
    head_of = [lane // N == h for h in range(4)]
    strict4 = (lane % N) < row
    incl4 = (lane % N) <= row
    eye4 = ((lane % N) == row).astype(F32)
    br = lax.broadcasted_iota(jnp.int32, (QW, QW), 0)
    bc = lax.broadcasted_iota(jnp.int32, (QW, QW), 1)
    bd_mask = (br // N) == (bc // N)
    eye_q = (br == bc).astype(F32)

    def bd(x):
        x = x.astype(BF16)
        return jnp.concatenate([jnp.where(m, x, jnp.zeros_like(x)) for m in head_of], axis=0)

    inst = [(c, q) for c in range(nc) for q in range(nq)]
    sl = lambda x, c, q: x[c * C:(c + 1) * C, q * QW:(q + 1) * QW]
    each = lambda f, *lists: [f(*xs) for xs in zip(*lists)]

    At_i = [sl(At, c, q) for c, q in inst]
    Rt_i = [sl(Rt, c, q) for c, q in inst]
    V_i = [sl(V, c, q) for c, q in inst]
    P = [_bdot_nt(jnp.concatenate([a, r], axis=0),
                  jnp.concatenate([bd(sl(Kb, c, q)), bd(sl(Bb, c, q))], axis=0))
         for a, r, (c, q) in zip(At_i, Rt_i, inst)]
    A_ak = [jnp.where(strict4, p[:C, :QW], 0.0).astype(BF16) for p in P]
    L = [jnp.where(strict4, p[:C, QW:], 0.0) for p in P]
    A_rk = [jnp.where(incl4, p[C:, :QW], 0.0).astype(BF16) for p in P]
    A_rb = [jnp.where(incl4, p[C:, QW:], 0.0).astype(BF16) for p in P]
    AV = each(lambda ak, rk, v: _bdot(jnp.concatenate([ak, rk], axis=0), bd(v)), A_ak, A_rk, V_i)
    X = [eye4 + l for l in L]
    M = each(lambda l: _bdot(l, bd(l)), L)
    for _ in range(4):
        XM = each(lambda x, m: _bdot(jnp.concatenate([x, m], axis=0), bd(m)), X, M)
        X = each(lambda x, xm: x + xm[:C], X, XM)
        M = [xm[C:] for xm in XM]
    Tm = each(lambda x, m: (x + _bdot(x, bd(m))).astype(BF16), X, M)
    W = each(lambda t, a, av: _bdot(t, jnp.concatenate([bd(a), bd(av[:C])], axis=1)), Tm, At_i, AV)
    Z = each(lambda rb, w: _bdot(rb, jnp.concatenate([bd(w[:, :QW]), bd(w[:, QW:])], axis=1)), A_rb, W)
    GA = [_bdot_tn(sl(Be, c, q), w[:, :QW]) for (c, q), w in zip(inst, W)]
    G0 = [jnp.where(bd_mask, _bdot_tn(jnp.concatenate([sl(Ke, c, q), sl(Be, c, q)], axis=0),
                                      jnp.concatenate([v, w[:, QW:].astype(BF16)], axis=0)), 0.0)
          for (c, q), v, w in zip(inst, V_i, W)]
    RM = [jnp.concatenate([(r.astype(F32) + z[:, :QW]).astype(BF16),
                           (jnp.where(bd_mask, ga, 0.0)
                            + eye_q * g_tot[c * C:c * C + 1, q * QW:(q + 1) * QW]).astype(BF16)], axis=0)
          for r, z, ga, (c, q) in zip(Rt_i, Z, GA, inst)]
    Y0 = each(lambda av, z: av[C:] + z[:, QW:], AV, Z)

    Hs = [h_ref[q] for q in range(nq)]
    y_rows = []
    for c in range(nc):
        RMH = [_bdot(RM[c * nq + q], Hs[q]) for q in range(nq)]
        y_rows.append(jnp.concatenate([RMH[q][:C] + Y0[c * nq + q] for q in range(nq)], axis=1))
        Hs = [RMH[q][C:] + G0[c * nq + q] for q in range(nq)]
    for q in range(nq):
        h_ref[q] = Hs[q]
    y = jnp.concatenate(y_rows, axis=0)
    inv_n = 1.0 / N
    mu = _seg_sum(y, seg_ref) * inv_n
    d = y - mu
    var = _seg_sum(d * d, seg_ref) * inv_n
    yn = d * lax.rsqrt(var + RWKV_GN_EPS)
    y_ref[...] = (yn * gnw_ref[...] + gnb_ref[...] + bonus_ref[...]) * g_ref[...]


def _rwkv_call(r, k, v, ld, kk, a, g, bonus, gnw, gnb, seg):
    B, T, _ = r.shape
    tt = min(TT_REC, T)
    tok = pl.BlockSpec((None, tt, D_RWKV), lambda b, t: (b, t, 0))
    return pl.pallas_call(
        _rwkv_kernel,
        grid=(B, T // tt),
        in_specs=[tok] * 8 + [_const_spec((1, D_RWKV)), _const_spec((1, D_RWKV)),
                              _const_spec((D_RWKV, D_RWKV))],
        out_specs=tok,
        out_shape=jax.ShapeDtypeStruct((B, T, D_RWKV), F32),
        scratch_shapes=[pltpu.VMEM((D_RWKV // (4 * RWKV_HEAD), 4 * RWKV_HEAD, 4 * RWKV_HEAD), F32)],
        compiler_params=pltpu.CompilerParams(dimension_semantics=("arbitrary", "arbitrary"),
                                             vmem_limit_bytes=VMEM_LIMIT),
        name="rwkv7",
    )(r, k, v, ld, kk, a, g, bonus, gnw, gnb, seg)


def _log_sigmoid(x):
    return jnp.minimum(x, 0.0) - jnp.log(1.0 + jnp.exp(-jnp.abs(x)))


def _mlstm_kernel(q_ref, k_ref, v_ref, o_ref, if_ref, brow_ref, bcol_ref, nw_ref,
                  y_ref, cn_ref, m_ref):
    @pl.when(pl.program_id(1) == 0)
    def _():
        cn_ref[...] = jnp.zeros_like(cn_ref)
        m_ref[...] = jnp.zeros_like(m_ref)

    C, d = CHUNK, MLSTM_HEAD
    tt = q_ref.shape[0]
    nc = tt // C
    NH = MLSTM_HEADS
    ti = lax.broadcasted_iota(jnp.int32, (C, C), 0)
    si = lax.broadcasted_iota(jnp.int32, (C, C), 1)
    causal = si <= ti
    ones = jnp.ones((C, d), BF16)
    i0 = IF_LANE
    f0 = IF_LANE + NH

    tr = lax.broadcasted_iota(jnp.int32, (tt, tt), 0)
    tc = lax.broadcasted_iota(jnp.int32, (tt, tt), 1)
    same = (tr // C) == (tc // C)
    tril = (same & (tc <= tr)).astype(BF16)
    triu = (same & (tr <= tc)).astype(BF16)
    pre = if_ref[...] + brow_ref[...]
    bcum_cols = sum(_dot(tril, p) for p in _split3(_log_sigmoid(pre)))
    preT = jnp.transpose(if_ref[...])[i0:i0 + 8, :] + bcol_ref[...]
    row8 = lax.broadcasted_iota(jnp.int32, (8, 1), 0)
    bcum_rows = sum(_dot(p, triu) for p in _split3(jnp.where(row8 < NH, preT, _log_sigmoid(preT))))

    inst = [(c, h) for c in range(nc) for h in range(NH)]
    rs = lambda c: slice(c * C, (c + 1) * C)
    hs_ = lambda h: slice(h * d, (h + 1) * d)
    q_i = [q_ref[rs(c), hs_(h)].astype(BF16) for c, h in inst]
    k_i = [k_ref[rs(c), hs_(h)] for c, h in inst]
    ve_i = [jnp.concatenate([v_ref[rs(c), hs_(h)].astype(BF16), ones], axis=1) for c, h in inst]
    bc_i = [bcum_cols[rs(c), f0 + h:f0 + h + 1] for c, h in inst]
    e_c = [pre[rs(c), i0 + h:i0 + h + 1] - bc for (c, h), bc in zip(inst, bc_i)]
    e_r = [preT[h:h + 1, rs(c)] - bcum_rows[NH + h:NH + h + 1, rs(c)] for c, h in inst]
    a_col = [jnp.max(jnp.where(causal, e, -jnp.inf), axis=-1, keepdims=True) for e in e_r]
    D0 = [jnp.where(causal, jnp.exp(e - a), 0.0) for e, a in zip(e_r, a_col)]
    S0 = [(_bdot_nt(q, k) * dm).astype(BF16) for q, k, dm in zip(q_i, k_i, D0)]
    a_end = [a[C - 1:C, :] for a in a_col]
    KV0 = [_bdot_tn(jnp.exp(e - ae) * k, ve) for e, ae, k, ve in zip(e_c, a_end, k_i, ve_i)]
    intra = [_dot(s, ve) for s, ve in zip(S0, ve_i)]

    cn = [cn_ref[h] for h in range(NH)]
    m_prev = [m_ref[h][0:1, 0:1] for h in range(NH)]
    hh = []
    for c in range(nc):
        ii = [c * NH + h for h in range(NH)]
        QC = [_bdot(q_i[i], cn[h]) for h, i in enumerate(ii)]
        for h, i in enumerate(ii):
            mu_t = jnp.maximum(m_prev[h], a_col[i])
            tot = jnp.exp(m_prev[h] - mu_t) * QC[h] + jnp.exp(a_col[i] - mu_t) * intra[i]
            hh.append(tot[:, :d] / jnp.maximum(jnp.abs(tot[:, d:]), jnp.exp(-(bc_i[i] + mu_t))))
            mu_end = jnp.maximum(m_prev[h], a_end[i])
            cn[h] = jnp.exp(m_prev[h] - mu_end) * cn[h] + jnp.exp(a_end[i] - mu_end) * KV0[i]
            m_prev[h] = bc_i[i][C - 1:C, :] + mu_end
    for h in range(NH):
        cn_ref[h] = cn[h]
        m_ref[h] = jnp.broadcast_to(m_prev[h], (8, 128))
    mean = [jnp.mean(x, axis=-1, keepdims=True) for x in hh]
    dd = [x - m for x, m in zip(hh, mean)]
    var = [jnp.mean(x * x, axis=-1, keepdims=True) for x in dd]
    for (c, h), x, v in zip(inst, dd, var):
        y_ref[rs(c), hs_(h)] = o_ref[rs(c), hs_(h)] * (x * lax.rsqrt(v + MLSTM_NORM_EPS) * nw_ref[:, hs_(h)])


def _mlstm_call(q, k, v, o, ifb, brow, bcol, nw):
    B, T, _ = q.shape
    tt = min(TT_REC, T)
    tok = lambda w: pl.BlockSpec((None, tt, w), lambda b, t: (b, t, 0))
    return pl.pallas_call(
        _mlstm_kernel,
        grid=(B, T // tt),
        in_specs=[tok(D_MLSTM)] * 4 + [tok(128), _const_spec((1, 128)), _const_spec((8, 1)),
                                       _const_spec((1, D_MLSTM))],
        out_specs=tok(D_MLSTM),
        out_shape=jax.ShapeDtypeStruct((B, T, D_MLSTM), F32),
        scratch_shapes=[pltpu.VMEM((MLSTM_HEADS, MLSTM_HEAD, 2 * MLSTM_HEAD), F32),
                        pltpu.VMEM((MLSTM_HEADS, 8, 128), F32)],
        compiler_params=pltpu.CompilerParams(dimension_semantics=("arbitrary", "arbitrary"),
                                             vmem_limit_bytes=VMEM_LIMIT),
        name="mlstm",
    )(q, k, v, o, ifb, brow, bcol, nw)


def _pad_rows(w, rows, at=0):
    out = jnp.zeros((rows, w.shape[1]), w.dtype)
    return out.at[at:at + w.shape[0]].set(w)


def _ffn_params(norm, w_in, w_out):
    nc = D_FF // FF_CHUNK
    wg = w_in[:, :D_FF].reshape(D_MODEL, nc, FF_CHUNK).transpose(1, 0, 2).astype(BF16)
    wu = w_in[:, D_FF:].reshape(D_MODEL, nc, FF_CHUNK).transpose(1, 0, 2).astype(BF16)
    wo = w_out.reshape(nc, FF_CHUNK, D_MODEL).astype(BF16)
    return norm.reshape(1, D_MODEL), wg, wu, wo


def _mixer_params(l, mix_norm, w_in, shift_mu, rw_w0, rw_w_up, rw_a0, rw_a_up, rw_g_up, rw_k_k, rw_k_a,
                  rw_r_k, vres_down, vres_up, vres_bias, ml_conv_w, ml_conv_b, ml_i_bias, ml_f_bias):
    w = w_in[l]
    o_ml = N_RW_IN
    o_gate = N_RW_IN + N_ML_IN
    w_rkv = w[:, :3 * D_RWKV]
    w_lora = w[:, 3 * D_RWKV:N_RW_IN]
    w_if = w[:, o_ml + 4 * D_MLSTM:o_gate]
    w_vd = vres_down[l - 1] if l > 0 else jnp.zeros((D_MODEL, LORA_V), F32)
    pad = jnp.zeros((D_MODEL, SM_W - SM_VRES - LORA_V), F32)
    w_sm = jnp.concatenate([w_lora, w_if, w_vd, pad], axis=1)
    mu = shift_mu[l]
    mu_sm = jnp.concatenate([mu[3 * D_RWKV:], jnp.zeros((SM_W - (N_RW_IN - 3 * D_RWKV),), F32)])
    row = lambda x: x.reshape(1, -1)
    p = {
        "mix_norm": row(mix_norm[l]),
        "w_rkv": w_rkv.astype(BF16),
        "w_sm": w_sm.astype(BF16),
        "w_ml": w[:, o_ml:o_ml + 4 * D_MLSTM].astype(BF16),
        "w_gate": w[:, o_gate:].astype(BF16),
        "mu_rkv": row(mu[:3 * D_RWKV]),
        "mu_sm": row(mu_sm),
        "w0": row(rw_w0[l]),
        "wup": _pad_rows(rw_w_up[l], 128, 0).astype(BF16),
        "a0": row(rw_a0[l]),
        "aup": _pad_rows(rw_a_up[l], 128, LORA_W).astype(BF16),
        "gup": _pad_rows(rw_g_up[l], 256, 0).astype(BF16),
        "k_k": row(rw_k_k[l]),
        "k_a": row(rw_k_a[l]),
        "r_k": row(rw_r_k[l]),
        "vbias": row(vres_bias[l - 1]) if l > 0 else jnp.zeros((1, D_RWKV), F32),
        "vup": (_pad_rows(vres_up[l - 1], 256, SM_VRES - 128) if l > 0
                else jnp.zeros((256, D_RWKV), F32)).astype(BF16),
        "convw": ml_conv_w[l],
        "convb": row(ml_conv_b[l]),
    }
    gate_bias = jnp.concatenate([ml_i_bias[l], ml_f_bias[l]])
    p["if_brow"] = jnp.zeros((1, 128), F32).at[0, IF_LANE:IF_LANE + 8].set(gate_bias)
    p["if_bcol"] = gate_bias.reshape(8, 1)
    return p


def kernel(x, ffn1_norm, ffn1_w_in, ffn1_w_out, mix_norm, w_in, shift_mu, rw_w0, rw_w_up, rw_a0, rw_a_up, rw_g_up, rw_k_k, rw_k_a, rw_r_k, rw_gn_w, rw_gn_b, vres_down, vres_up, vres_bias, ml_conv_w, ml_conv_b, ml_i_bias, ml_f_bias, ml_norm_w, br_a, br_b, w_out, ffn2_norm, ffn2_w_in, ffn2_w_out, final_norm):
    B, T, D = x.shape
    assert D == D_MODEL and T % CHUNK == 0
    n = B * T
    lane = jnp.arange(D_RWKV) // RWKV_HEAD
    seg = (lane[:, None] == lane[None, :]).astype(BF16)
    fin = final_norm.reshape(1, D_MODEL)
    h = x.reshape(n, D)
    v_first = None
    for l in range(DEPTH):
        h = _ffn_call(h, *_ffn_params(ffn1_norm[l], ffn1_w_in[l], ffn1_w_out[l]))
        p = _mixer_params(l, mix_norm, w_in, shift_mu, rw_w0, rw_w_up, rw_a0, rw_a_up, rw_g_up, rw_k_k,
                          rw_k_a, rw_r_k, vres_down, vres_up, vres_bias, ml_conv_w, ml_conv_b,
                          ml_i_bias, ml_f_bias)
        p["seg"] = seg
        (r, k, v, ld, kk, a, g, bonus, q, mk, mv, o, ifb, sg) = _mix_in_call(
            h.reshape(B, T, D), v_first, p)
        if l == 0:
            v_first = v
        ya = _rwkv_call(r, k, v, ld, kk, a, g, bonus, rw_gn_w[l].reshape(1, -1),
                        rw_gn_b[l].reshape(1, -1), seg)
        yb = _mlstm_call(q, mk, mv, o, ifb, p["if_brow"], p["if_bcol"], ml_norm_w[l].reshape(1, -1))
        h = _merge_ffn_call(h, ya.reshape(n, -1), yb.reshape(n, -1), sg.reshape(n, -1),
                            br_a[l].astype(BF16), br_b[l].astype(BF16), w_out[l].astype(BF16),
                            *_ffn_params(ffn2_norm[l], ffn2_w_in[l], ffn2_w_out[l]), fin,
                            final=(l == DEPTH - 1))
    return h.reshape(B, T, D)
```

```python
import functools

import jax
import jax.numpy as jnp
from jax import lax
from jax.experimental import pallas as pl
from jax.experimental.pallas import tpu as pltpu

F32 = jnp.float32
BF16 = jnp.bfloat16

D_MODEL = 1024
DEPTH = 4
CHUNK = 64
D_RWKV = 512
RWKV_HEAD = 64
RWKV_HEADS = 8
LORA_W, LORA_A, LORA_V, LORA_G = 64, 64, 32, 160
RWKV_GN_EPS = 64e-5
D_MLSTM = 512
MLSTM_HEADS = 4
MLSTM_HEAD = 128
CONV_W = 4
MLSTM_NORM_EPS = 1e-5
D_FF = 2816
NORM_EPS = 1e-6
N_RW_IN = 3 * D_RWKV + LORA_W + LORA_A + LORA_G
N_ML_IN = 4 * D_MLSTM + 2 * MLSTM_HEADS

SM_W = 384
SM_IF = 288
SM_VRES = 296
IF_LANE = SM_IF - 256

FF_CHUNK = 256
TM_FFN = 512
TT_MIX = 256
VMEM_LIMIT = 56 * 1024 * 1024


def _const_spec(shape):
    nd = len(shape)
    return pl.BlockSpec(shape, lambda *_: (0,) * nd, pipeline_mode=pl.Buffered(1))


def _rms(h, g):
    return h * lax.rsqrt(jnp.mean(h * h, axis=-1, keepdims=True) + NORM_EPS) * g


def _sigmoid(x):
    return 1.0 / (1.0 + jnp.exp(-x))


def _dot(a, b):
    return jnp.dot(a, b, preferred_element_type=F32)


def _seg_sum(x, seg_ref):
    return _dot(x.astype(BF16), seg_ref[...])


def _ffn_body(h, g_ref, wg_ref, wu_ref, wo_ref, acc_ref):
    xb = _rms(h, g_ref[...]).astype(BF16)
    nc = D_FF // FF_CHUNK
    for c in range(nc):
        gate = _dot(xb, wg_ref[c])
        up = _dot(xb, wu_ref[c])
        act = (gate * _sigmoid(gate) * up).astype(BF16)
        part = _dot(act, wo_ref[c])
        if c == 0:
            acc_ref[...] = part
        elif c < nc - 1:
            acc_ref[...] += part
    return h + 0.5 * (acc_ref[...] + part)


def _ffn_kernel(h_ref, g_ref, wg_ref, wu_ref, wo_ref, out_ref, acc_ref):
    out_ref[...] = _ffn_body(h_ref[...], g_ref, wg_ref, wu_ref, wo_ref, acc_ref)


def _merge_ffn_kernel(h_ref, ya_ref, yb_ref, sg_ref, bra_ref, brb_ref, wout_ref,
                      g_ref, wg_ref, wu_ref, wo_ref, fin_ref, out_ref, acc_ref, *, final):
    sg = sg_ref[...]
    pa = _dot(ya_ref[...].astype(BF16), bra_ref[...])
    pb = _dot(yb_ref[...].astype(BF16), brb_ref[...])
    u = sg[:, :D_MODEL] * pa + sg[:, D_MODEL:] * pb
    h = h_ref[...] + _dot(u.astype(BF16), wout_ref[...])
    h = _ffn_body(h, g_ref, wg_ref, wu_ref, wo_ref, acc_ref)
    if final:
        h = _rms(h, fin_ref[...])
    out_ref[...] = h


def _ffn_weight_specs():
    nc = D_FF // FF_CHUNK
    return [_const_spec((1, D_MODEL)), _const_spec((nc, D_MODEL, FF_CHUNK)),
            _const_spec((nc, D_MODEL, FF_CHUNK)), _const_spec((nc, FF_CHUNK, D_MODEL))]


def _ffn_call(h, g, wg, wu, wo):
    n = h.shape[0]
    tm = min(TM_FFN, n)
    row = pl.BlockSpec((tm, D_MODEL), lambda i: (i, 0))
    return pl.pallas_call(
        _ffn_kernel,
        grid=(n // tm,),
        in_specs=[row] + _ffn_weight_specs(),
        out_specs=row,
        out_shape=jax.ShapeDtypeStruct((n, D_MODEL), F32),
        scratch_shapes=[pltpu.VMEM((tm, D_MODEL), F32)],
        compiler_params=pltpu.CompilerParams(dimension_semantics=("arbitrary",),
                                             vmem_limit_bytes=VMEM_LIMIT),
        name="ffn",
    )(h, g, wg, wu, wo)


def _merge_ffn_call(h, ya, yb, sg, bra, brb, wout, g, wg, wu, wo, fin, final):
    n = h.shape[0]
    tm = min(TM_FFN, n)
    row = lambda w: pl.BlockSpec((tm, w), lambda i: (i, 0))
    return pl.pallas_call(
        functools.partial(_merge_ffn_kernel, final=final),
        grid=(n // tm,),
        in_specs=[row(D_MODEL), row(D_RWKV), row(D_MLSTM), row(2 * D_MODEL),
                  _const_spec((D_RWKV, D_MODEL)), _const_spec((D_MLSTM, D_MODEL)),
                  _const_spec((D_MODEL, D_MODEL))] + _ffn_weight_specs() + [_const_spec((1, D_MODEL))],
        out_specs=row(D_MODEL),
        out_shape=jax.ShapeDtypeStruct((n, D_MODEL), F32),
        scratch_shapes=[pltpu.VMEM((tm, D_MODEL), F32)],
        compiler_params=pltpu.CompilerParams(dimension_semantics=("arbitrary",),
                                             vmem_limit_bytes=VMEM_LIMIT),
        name="merge_ffn",
    )(h, ya, yb, sg, bra, brb, wout, g, wg, wu, wo, fin)


def _bdot(a, b):
    return jnp.dot(a.astype(BF16), b.astype(BF16), preferred_element_type=F32)


def _bdot_nt(a, b):
    return lax.dot_general(a.astype(BF16), b.astype(BF16), (((1,), (1,)), ((), ())),
                           preferred_element_type=F32)


def _bdot_tn(a, b):
    return lax.dot_general(a.astype(BF16), b.astype(BF16), (((0,), (0,)), ((), ())),
                           preferred_element_type=F32)


def _split3(x):
    hi = x.astype(BF16)
    r1 = x - hi.astype(F32)
    mid = r1.astype(BF16)
    lo = (r1 - mid.astype(F32)).astype(BF16)
    return hi, mid, lo


def _log_sigmoid(x):
    return jnp.minimum(x, 0.0) - jnp.log(1.0 + jnp.exp(-jnp.abs(x)))


def _prep_rwkv(zr, zs, vfirst_ref, w, s, tt):
    z_rkv = zr[8:, :]
    z_sm = zs[8:, :]
    m_rkv = z_rkv + w["mu_rkv"][...] * (zr[7:7 + tt, :] - z_rkv)
    m_sm = z_sm + w["mu_sm"][...] * (zs[7:7 + tt, :] - z_sm)
    zr[:8, :] = zr[tt:, :]
    zs[:8, :] = zs[tt:, :]
    r = m_rkv[:, :D_RWKV]
    k = m_rkv[:, D_RWKV:2 * D_RWKV]
    v = m_rkv[:, 2 * D_RWKV:]
    xwa = m_sm[:, :128]
    xg_blk = m_sm[:, 128:]
    s["ifb"][...] = z_sm[:, 256:]
    if vfirst_ref is not None:
        vz = _dot(z_sm[:, 128:].astype(BF16), w["vup"][...])
        vg = _sigmoid(w["vbias"][...] + vz)
        v = v + (vfirst_ref[...] - v) * vg
    y = -(w["w0"][...] + _dot(jnp.tanh(xwa).astype(BF16), w["wup"][...]))
    softplus = jnp.maximum(y, 0.0) + jnp.log(1.0 + jnp.exp(-jnp.abs(y)))
    s["ld"][...] = -jnp.exp(-softplus - 0.5)
    a = _sigmoid(w["a0"][...] + _dot(xwa.astype(BF16), w["aup"][...]))
    s["g"][...] = _dot(_sigmoid(xg_blk).astype(BF16), w["gup"][...])
    kk = k * w["k_k"][...]
    kk = kk / jnp.maximum(jnp.sqrt(_seg_sum(kk * kk, w["seg"])), 1e-12)
    k2 = k * (1.0 + (a - 1.0) * w["k_a"][...])
    s["bonus"][...] = _seg_sum(r * k2 * w["r_k"][...], w["seg"]) * v
    s["r"][...] = r
    s["k"][...] = k2
    s["v"][...] = v
    s["kk"][...] = kk
    s["a"][...] = a


def _prep_mlstm(zm, zg, w, s, sg_o, tt):
    cw = w["convw"][...]
    nqk = 2 * D_MLSTM
    conv = w["convb"][...] + cw[CONV_W - 1:CONV_W] * zm[8:, :nqk]
    for sft in range(1, CONV_W):
        conv = conv + cw[CONV_W - 1 - sft:CONV_W - sft] * zm[8 - sft:8 - sft + tt, :nqk]
    qk = conv * _sigmoid(conv)
    s["q"][...] = qk[:, :D_MLSTM] * (MLSTM_HEAD ** -0.5)
    s["mk"][...] = qk[:, D_MLSTM:]
    s["mv"][...] = zm[8:, nqk:nqk + D_MLSTM]
    s["o"][...] = _sigmoid(zm[8:, nqk + D_MLSTM:])
    zm[:8, :] = zm[tt:, :]
    sg_o[...] = _sigmoid(zg[...])


def _rwkv_body(s, w, y_ref, h_ref, fillers):
    C, N = CHUNK, RWKV_HEAD
    tt = y_ref.shape[0]
    ti = lax.broadcasted_iota(jnp.int32, (C, C), 0)
    si = lax.broadcasted_iota(jnp.int32, (C, C), 1)
    strict = si < ti
    incl = si <= ti
    eye = (si == ti).astype(F32)

    tr = lax.broadcasted_iota(jnp.int32, (tt, tt), 0)
    tc = lax.broadcasted_iota(jnp.int32, (tt, tt), 1)
    same = (tr // C) == (tc // C)
    tri = (same & (tc <= tr)).astype(BF16)
    blk = same.astype(BF16)
    lw = s["ld"][...]
    parts = _split3(lw)
    cs = sum(_dot(tri, p) for p in parts)
    cs_tot = sum(_dot(blk, p) for p in parts)

    kk = s["kk"][...]
    kka = kk * s["a"][...]
    k2 = s["k"][...]
    g_inv = jnp.exp(-cs)
    g_end = jnp.exp(cs_tot - cs)
    g_tot = jnp.exp(cs_tot)
    Rt = (s["r"][...] * jnp.exp(cs)).astype(BF16)
    At = (-kk * jnp.exp(cs - lw)).astype(BF16)
    Kb = (k2 * g_inv).astype(BF16)
    Bb = (kka * g_inv).astype(BF16)
    Ke = (k2 * g_end).astype(BF16)
    Be = (kka * g_end).astype(BF16)
    V = s["v"][...].astype(BF16)

    nc = tt // C
    inst = [(c, h) for c in range(nc) for h in range(RWKV_HEADS)]
    sl = lambda x, c, h: x[c * C:(c + 1) * C, h * N:(h + 1) * N]
    each = lambda f, *lists: [f(*xs) for xs in zip(*lists)]

    At_i = [sl(At, c, h) for c, h in inst]
    Rt_i = [sl(Rt, c, h) for c, h in inst]
    V_i = [sl(V, c, h) for c, h in inst]
    P = [_bdot_nt(jnp.concatenate([a, r], axis=0),
                  jnp.concatenate([sl(Kb, c, h), sl(Bb, c, h)], axis=0))
         for a, r, (c, h) in zip(At_i, Rt_i, inst)]
    A_ak = [jnp.where(strict, p[:C, :C], 0.0) for p in P]
    L = [jnp.where(strict, p[:C, C:], 0.0) for p in P]
    A_rk = [jnp.where(incl, p[C:, :C], 0.0) for p in P]
    A_rb = [jnp.where(incl, p[C:, C:], 0.0).astype(BF16) for p in P]
    AV = each(lambda ak, rk, v: _bdot(jnp.concatenate([ak, rk], axis=0), v), A_ak, A_rk, V_i)
    X = [eye + l for l in L]
    M = each(lambda l: _bdot(l, l), L)
    for _ in range(4):
        XM = each(lambda x, m: _bdot(jnp.concatenate([x, m], axis=0), m), X, M)
        X = each(lambda x, xm: x + xm[:C], X, XM)
        M = [xm[C:] for xm in XM]
    Tm = each(lambda x, m: x + _bdot(x, m), X, M)
    W = each(lambda t, a, av: _bdot(t, jnp.concatenate([a, av[:C].astype(BF16)], axis=1)),
             Tm, At_i, AV)
    Z = each(_bdot, A_rb, W)
    G = [_bdot_tn(sl(Be, c, h), w_) for (c, h), w_ in zip(inst, W)]
    KV = [_bdot_tn(sl(Ke, c, h), v) for (c, h), v in zip(inst, V_i)]
    RM = [jnp.concatenate([r.astype(F32) + z[:, :N],
                           eye * g_tot[c * C:c * C + 1, h * N:(h + 1) * N] + g[:, :N]], axis=0).astype(BF16)
          for r, z, g, (c, h) in zip(Rt_i, Z, G, inst)]
    Y0 = each(lambda av, z: av[C:] + z[:, N:], AV, Z)
    G0 = each(lambda kv, g: kv + g[:, N:], KV, G)

    Hs = [h_ref[h] for h in range(RWKV_HEADS)]
    y_rows = []
    for c in range(nc):
        RMH = [_bdot(RM[c * RWKV_HEADS + h], Hs[h]) for h in range(RWKV_HEADS)]
        if c < len(fillers):
            fillers[c]()
        y_rows.append(jnp.concatenate([RMH[h][:C] + Y0[c * RWKV_HEADS + h]
                                       for h in range(RWKV_HEADS)], axis=1))
        Hs = [RMH[h][C:] + G0[c * RWKV_HEADS + h] for h in range(RWKV_HEADS)]
    for h in range(RWKV_HEADS):
        h_ref[h] = Hs[h]
    y = jnp.concatenate(y_rows, axis=0)
    inv_n = 1.0 / N
    mu = _seg_sum(y, w["seg"]) * inv_n
    d = y - mu
    var = _seg_sum(d * d, w["seg"]) * inv_n
    yn = d * lax.rsqrt(var + RWKV_GN_EPS)
    y_ref[...] = (yn * w["gnw"][...] + w["gnb"][...] + s["bonus"][...]) * s["g"][...]


def _mlstm_body(s, w, y_ref, cn_ref, m_ref):
    q_ref, k_ref, v_ref, o_ref, if_ref = s["q"], s["mk"], s["mv"], s["o"], s["ifb"]
    brow_ref, bcol_ref, nw_ref = w["if_brow"], w["if_bcol"], w["ml_nw"]
    C, d = CHUNK, MLSTM_HEAD
    tt = q_ref.shape[0]
    nc = tt // C
    NH = MLSTM_HEADS
    ti = lax.broadcasted_iota(jnp.int32, (C, C), 0)
    si = lax.broadcasted_iota(jnp.int32, (C, C), 1)
    causal = si <= ti
    ones = jnp.ones((C, d), BF16)
    i0 = IF_LANE
    f0 = IF_LANE + NH

    tr = lax.broadcasted_iota(jnp.int32, (tt, tt), 0)
    tc = lax.broadcasted_iota(jnp.int32, (tt, tt), 1)
    same = (tr // C) == (tc // C)
    tril = (same & (tc <= tr)).astype(BF16)
    triu = (same & (tr <= tc)).astype(BF16)
    pre = if_ref[...] + brow_ref[...]
    bcum_cols = sum(_dot(tril, p) for p in _split3(_log_sigmoid(pre)))
    preT = jnp.transpose(if_ref[...])[i0:i0 + 8, :] + bcol_ref[...]
    row8 = lax.broadcasted_iota(jnp.int32, (8, 1), 0)
    bcum_rows = sum(_dot(p, triu) for p in _split3(jnp.where(row8 < NH, preT, _log_sigmoid(preT))))

    inst = [(c, h) for c in range(nc) for h in range(NH)]
    rs = lambda c: slice(c * C, (c + 1) * C)
    hs_ = lambda h: slice(h * d, (h + 1) * d)
    q_i = [q_ref[rs(c), hs_(h)].astype(BF16) for c, h in inst]
    k_i = [k_ref[rs(c), hs_(h)] for c, h in inst]
    ve_i = [jnp.concatenate([v_ref[rs(c), hs_(h)].astype(BF16), ones], axis=1) for c, h in inst]
    bc_i = [bcum_cols[rs(c), f0 + h:f0 + h + 1] for c, h in inst]
    e_c = [pre[rs(c), i0 + h:i0 + h + 1] - bc for (c, h), bc in zip(inst, bc_i)]
    e_r = [preT[h:h + 1, rs(c)] - bcum_rows[NH + h:NH + h + 1, rs(c)] for c, h in inst]
    a_col = [jnp.max(jnp.where(causal, e, -jnp.inf), axis=-1, keepdims=True) for e in e_r]
    D0 = [jnp.where(causal, jnp.exp(e - a), 0.0) for e, a in zip(e_r, a_col)]
    S0 = [(_bdot_nt(q, k) * dm).astype(BF16) for q, k, dm in zip(q_i, k_i, D0)]
    a_end = [a[C - 1:C, :] for a in a_col]
    KV0 = [_bdot_tn(jnp.exp(e - ae) * k, ve) for e, ae, k, ve in zip(e_c, a_end, k_i, ve_i)]
    intra = [_dot(s0, ve) for s0, ve in zip(S0, ve_i)]

    cn = [cn_ref[h] for h in range(NH)]
    m_prev = [m_ref[h][0:1, 0:1] for h in range(NH)]
    hh = []
    for c in range(nc):
        ii = [c * NH + h for h in range(NH)]
        QC = [_bdot(q_i[i], cn[h]) for h, i in enumerate(ii)]
        for h, i in enumerate(ii):
            mu_t = jnp.maximum(m_prev[h], a_col[i])
            tot = jnp.exp(m_prev[h] - mu_t) * QC[h] + jnp.exp(a_col[i] - mu_t) * intra[i]
            hh.append(tot[:, :d] / jnp.maximum(jnp.abs(tot[:, d:]), jnp.exp(-(bc_i[i] + mu_t))))
            mu_end = jnp.maximum(m_prev[h], a_end[i])
            cn[h] = jnp.exp(m_prev[h] - mu_end) * cn[h] + jnp.exp(a_end[i] - mu_end) * KV0[i]
            m_prev[h] = bc_i[i][C - 1:C, :] + mu_end
    for h in range(NH):
        cn_ref[h] = cn[h]
        m_ref[h] = jnp.broadcast_to(m_prev[h], (8, 128))
    mean = [jnp.mean(x, axis=-1, keepdims=True) for x in hh]
    dd = [x - m for x, m in zip(hh, mean)]
    var = [jnp.mean(x * x, axis=-1, keepdims=True) for x in dd]
    for (c, h), x, v in zip(inst, dd, var):
        y_ref[rs(c), hs_(h)] = o_ref[rs(c), hs_(h)] * (x * lax.rsqrt(v + MLSTM_NORM_EPS) * nw_ref[:, hs_(h)])


def _mixer_kernel(*refs, has_vres, nt, wnames, snames):
    refs = list(refs)
    h_ref = refs.pop(0)
    vfirst_ref = refs.pop(0) if has_vres else None
    w = {n: refs.pop(0) for n in wnames}
    ya_o, yb_o, sg_o = refs.pop(0), refs.pop(0), refs.pop(0)
    vfirst_o = None if has_vres else refs.pop(0)
    zr, zs, zm, zg, xb_ref = (refs.pop(0) for _ in range(5))
    s = {n: refs.pop(0) for n in snames}
    h_state, cn_state, m_state = refs
    tt = h_ref.shape[0]
    i = pl.program_id(0)

    @pl.when(i == 0)
    def _():
        for ref in (zr, zs, zm, zg, h_state, cn_state, m_state):
            ref[...] = jnp.zeros_like(ref)

    @pl.when(lax.rem(i + nt - 1, nt) == 0)
    def _():
        for ref in (zr, zs, zm):
            ref[:8, :] = jnp.zeros((8, ref.shape[1]), F32)
        h_state[...] = jnp.zeros_like(h_state)
        cn_state[...] = jnp.zeros_like(cn_state)
        m_state[...] = jnp.zeros_like(m_state)

    xb_ref[...] = _rms(h_ref[...], w["mix_norm"][...]).astype(BF16)

    def project(z_ref, w_ref, lo, hi, row0):
        def run():
            z_ref[row0:row0 + tt, lo:hi] = _dot(xb_ref[...], w_ref[:, lo:hi])
        return run

    _prep_rwkv(zr, zs, vfirst_ref, w, s, tt)
    if vfirst_o is not None:
        vfirst_o[...] = s["v"][...]
    project(zr, w["w_rkv"], 0, 3 * D_RWKV, 8)()
    project(zs, w["w_sm"], 0, SM_W, 8)()
    _prep_mlstm(zm, zg, w, s, sg_o, tt)
    project(zm, w["w_ml"], 0, 4 * D_MLSTM, 8)()
    gate_w = 2 * D_MODEL // 4
    fillers = [project(zg, w["w_gate"], j * gate_w, (j + 1) * gate_w, 0) for j in range(4)]
    _rwkv_body(s, w, ya_o, h_state, fillers)
    _mlstm_body(s, w, yb_o, cn_state, m_state)


def _mixer_call(h3, vfirst, p):
    B, T, _ = h3.shape
    tt = min(TT_MIX, T)
    nt = T // tt
    n = B * nt
    has_vres = vfirst is not None
    cur = lambda wd: pl.BlockSpec((None, tt, wd), lambda i: (jnp.minimum(i, n - 1) // nt,
                                                            jnp.minimum(i, n - 1) % nt, 0))
    lag = lambda wd: pl.BlockSpec((None, tt, wd), lambda i: (jnp.maximum(i - 1, 0) // nt,
                                                            jnp.maximum(i - 1, 0) % nt, 0))
    wnames = ("mix_norm", "w_rkv", "w_sm", "w_ml", "w_gate", "mu_rkv", "mu_sm", "w0", "wup", "a0", "aup",
              "gup", "k_k", "k_a", "r_k", "vbias", "vup", "convw", "convb", "seg", "gnw", "gnb",
              "if_brow", "if_bcol", "ml_nw")
    snames = ("r", "k", "v", "ld", "kk", "a", "g", "bonus", "q", "mk", "mv", "o", "ifb")
    weights = [p[nm] for nm in wnames]
    in_specs = [cur(D_MODEL)] + ([lag(D_RWKV)] if has_vres else []) + [_const_spec(x.shape) for x in weights]
    out_w = [D_RWKV, D_MLSTM, 2 * D_MODEL] + ([] if has_vres else [D_RWKV])
    scratch = [pltpu.VMEM((tt + 8, 3 * D_RWKV), F32), pltpu.VMEM((tt + 8, SM_W), F32),
               pltpu.VMEM((tt + 8, 4 * D_MLSTM), F32), pltpu.VMEM((tt, 2 * D_MODEL), F32),
               pltpu.VMEM((tt, D_MODEL), BF16)]
    scratch += [pltpu.VMEM((tt, 128 if nm == "ifb" else D_RWKV), F32) for nm in snames]
    scratch += [pltpu.VMEM((RWKV_HEADS, RWKV_HEAD, RWKV_HEAD), F32),
                pltpu.VMEM((MLSTM_HEADS, MLSTM_HEAD, 2 * MLSTM_HEAD), F32),
                pltpu.VMEM((MLSTM_HEADS, 8, 128), F32)]
    return pl.pallas_call(
        functools.partial(_mixer_kernel, has_vres=has_vres, nt=nt, wnames=wnames, snames=snames),
        grid=(n + 1,),
        in_specs=in_specs,
        out_specs=[lag(wd) for wd in out_w],
        out_shape=[jax.ShapeDtypeStruct((B, T, wd), F32) for wd in out_w],
        scratch_shapes=scratch,
        compiler_params=pltpu.CompilerParams(dimension_semantics=("arbitrary",),
                                             vmem_limit_bytes=VMEM_LIMIT),
        name="mixer",
    )(*([h3] + ([vfirst] if has_vres else []) + weights))


def _pad_rows(w, rows, at=0):
    out = jnp.zeros((rows, w.shape[1]), w.dtype)
    return out.at[at:at + w.shape[0]].set(w)


def _ffn_params(norm, w_in, w_out):
    nc = D_FF // FF_CHUNK
    wg = w_in[:, :D_FF].reshape(D_MODEL, nc, FF_CHUNK).transpose(1, 0, 2).astype(BF16)
    wu = w_in[:, D_FF:].reshape(D_MODEL, nc, FF_CHUNK).transpose(1, 0, 2).astype(BF16)
    wo = w_out.reshape(nc, FF_CHUNK, D_MODEL).astype(BF16)
    return norm.reshape(1, D_MODEL), wg, wu, wo


def _mixer_params(l, mix_norm, w_in, shift_mu, rw_w0, rw_w_up, rw_a0, rw_a_up, rw_g_up, rw_k_k, rw_k_a,
                  rw_r_k, vres_down, vres_up, vres_bias, ml_conv_w, ml_conv_b, ml_i_bias, ml_f_bias):
    w = w_in[l]
    o_ml = N_RW_IN
    o_gate = N_RW_IN + N_ML_IN
    w_rkv = w[:, :3 * D_RWKV]
    w_lora = w[:, 3 * D_RWKV:N_RW_IN]
    w_if = w[:, o_ml + 4 * D_MLSTM:o_gate]
    w_vd = vres_down[l - 1] if l > 0 else jnp.zeros((D_MODEL, LORA_V), F32)
    pad = jnp.zeros((D_MODEL, SM_W - SM_VRES - LORA_V), F32)
    w_sm = jnp.concatenate([w_lora, w_if, w_vd, pad], axis=1)
    mu = shift_mu[l]
    mu_sm = jnp.concatenate([mu[3 * D_RWKV:], jnp.zeros((SM_W - (N_RW_IN - 3 * D_RWKV),), F32)])
    row = lambda x: x.reshape(1, -1)
    p = {
        "mix_norm": row(mix_norm[l]),
        "w_rkv": w_rkv.astype(BF16),
        "w_sm": w_sm.astype(BF16),
        "w_ml": w[:, o_ml:o_ml + 4 * D_MLSTM].astype(BF16),
        "w_gate": w[:, o_gate:].astype(BF16),
        "mu_rkv": row(mu[:3 * D_RWKV]),
        "mu_sm": row(mu_sm),
        "w0": row(rw_w0[l]),
        "wup": _pad_rows(rw_w_up[l], 128, 0).astype(BF16),
        "a0": row(rw_a0[l]),
        "aup": _pad_rows(rw_a_up[l], 128, LORA_W).astype(BF16),
        "gup": _pad_rows(rw_g_up[l], 256, 0).astype(BF16),
        "k_k": row(rw_k_k[l]),
        "k_a": row(rw_k_a[l]),
        "r_k": row(rw_r_k[l]),
        "vbias": row(vres_bias[l - 1]) if l > 0 else jnp.zeros((1, D_RWKV), F32),
        "vup": (_pad_rows(vres_up[l - 1], 256, SM_VRES - 128) if l > 0
                else jnp.zeros((256, D_RWKV), F32)).astype(BF16),
        "convw": ml_conv_w[l],
        "convb": row(ml_conv_b[l]),
    }
    gate_bias = jnp.concatenate([ml_i_bias[l], ml_f_bias[l]])
    p["if_brow"] = jnp.zeros((1, 128), F32).at[0, IF_LANE:IF_LANE + 8].set(gate_bias)
    p["if_bcol"] = gate_bias.reshape(8, 1)
    return p


def kernel(x, ffn1_norm, ffn1_w_in, ffn1_w_out, mix_norm, w_in, shift_mu, rw_w0, rw_w_up, rw_a0, rw_a_up, rw_g_up, rw_k_k, rw_k_a, rw_r_k, rw_gn_w, rw_gn_b, vres_down, vres_up, vres_bias, ml_conv_w, ml_conv_b, ml_i_bias, ml_f_bias, ml_norm_w, br_a, br_b, w_out, ffn2_norm, ffn2_w_in, ffn2_w_out, final_norm):
    B, T, D = x.shape
    n = B * T
    assert D == D_MODEL and T % min(TT_MIX, T) == 0 and min(TT_MIX, T) % CHUNK == 0
    assert n % min(TM_FFN, n) == 0
    lane = jnp.arange(D_RWKV) // RWKV_HEAD
    seg = (lane[:, None] == lane[None, :]).astype(BF16)
    fin = final_norm.reshape(1, D_MODEL)
    h = x.reshape(n, D)
    v_first = None
    for l in range(DEPTH):
        h = _ffn_call(h, *_ffn_params(ffn1_norm[l], ffn1_w_in[l], ffn1_w_out[l]))
        p = _mixer_params(l, mix_norm, w_in, shift_mu, rw_w0, rw_w_up, rw_a0, rw_a_up, rw_g_up, rw_k_k,
                          rw_k_a, rw_r_k, vres_down, vres_up, vres_bias, ml_conv_w, ml_conv_b,
                          ml_i_bias, ml_f_bias)
        p.update(seg=seg, gnw=rw_gn_w[l].reshape(1, -1), gnb=rw_gn_b[l].reshape(1, -1),
                 ml_nw=ml_norm_w[l].reshape(1, -1))
        outs = _mixer_call(h.reshape(B, T, D), v_first, p)
        ya, yb, sg = outs[:3]
        if l == 0:
            v_first = outs[3]
        h = _merge_ffn_call(h, ya.reshape(n, -1), yb.reshape(n, -1), sg.reshape(n, -1),
                            br_a[l].astype(BF16), br_b[l].astype(BF16), w_out[l].astype(BF16),
                            *_ffn_params(ffn2_norm[l], ffn2_w_in[l], ffn2_w_out[l]), fin,
                            final=(l == DEPTH - 1))
    return h.reshape(B, T, D)
```

```python
import functools

import jax
import jax.numpy as jnp
from jax import lax
from jax.experimental import pallas as pl
from jax.experimental.pallas import tpu as pltpu

F32 = jnp.float32
BF16 = jnp.bfloat16

D_MODEL = 1024
DEPTH = 4
CHUNK = 64
D_RWKV = 512
RWKV_HEAD = 64
RWKV_HEADS = 8
LORA_W, LORA_A, LORA_V, LORA_G = 64, 64, 32, 160
RWKV_GN_EPS = 64e-5
D_MLSTM = 512
MLSTM_HEADS = 4
MLSTM_HEAD = 128
CONV_W = 4
MLSTM_NORM_EPS = 1e-5
D_FF = 2816
NORM_EPS = 1e-6
N_RW_IN = 3 * D_RWKV + LORA_W + LORA_A + LORA_G
N_ML_IN = 4 * D_MLSTM + 2 * MLSTM_HEADS

SM_W = 384
SM_IF = 288
SM_VRES = 296
IF_LANE = SM_IF - 256

FF_CHUNK = 256
TM_FFN = 512
TT_MIX = 256
VMEM_LIMIT = 56 * 1024 * 1024


def _const_spec(shape):
    nd = len(shape)
    return pl.BlockSpec(shape, lambda *_: (0,) * nd, pipeline_mode=pl.Buffered(1))


def _rms(h, g):
    return h * lax.rsqrt(jnp.mean(h * h, axis=-1, keepdims=True) + NORM_EPS) * g


def _sigmoid(x):
    return 0.5 * jnp.tanh(0.5 * x) + 0.5


def _dot(a, b):
    return jnp.dot(a, b, preferred_element_type=F32)


def _seg_sum(x, seg_ref):
    return _dot(x.astype(BF16), seg_ref[...])


def _ffn_body(h, g_ref, wg_ref, wu_ref, wo_ref, acc_ref):
    xb = _rms(h, g_ref[...]).astype(BF16)
    nc = D_FF // FF_CHUNK
    for c in range(nc):
        gate = _dot(xb, wg_ref[c])
        up = _dot(xb, wu_ref[c])
        act = (gate * _sigmoid(gate) * up).astype(BF16)
        part = _dot(act, wo_ref[c])
        if c == 0:
            acc_ref[...] = part
        elif c < nc - 1:
            acc_ref[...] += part
    return h + 0.5 * (acc_ref[...] + part)


def _ffn_kernel(h_ref, g_ref, wg_ref, wu_ref, wo_ref, out_ref, acc_ref):
    out_ref[...] = _ffn_body(h_ref[...], g_ref, wg_ref, wu_ref, wo_ref, acc_ref)


def _merge_ffn_kernel(h_ref, ya_ref, yb_ref, sg_ref, bra_ref, brb_ref, wout_ref,
                      g_ref, wg_ref, wu_ref, wo_ref, fin_ref, out_ref, acc_ref, *, final):
    sg = sg_ref[...]
    pa = _dot(ya_ref[...].astype(BF16), bra_ref[...])
    pb = _dot(yb_ref[...].astype(BF16), brb_ref[...])
    u = sg[:, :D_MODEL] * pa + sg[:, D_MODEL:] * pb
    h = h_ref[...] + _dot(u.astype(BF16), wout_ref[...])
    h = _ffn_body(h, g_ref, wg_ref, wu_ref, wo_ref, acc_ref)
    if final:
        h = _rms(h, fin_ref[...])
    out_ref[...] = h


def _ffn_weight_specs():
    nc = D_FF // FF_CHUNK
    return [_const_spec((1, D_MODEL)), _const_spec((nc, D_MODEL, FF_CHUNK)),
            _const_spec((nc, D_MODEL, FF_CHUNK)), _const_spec((nc, FF_CHUNK, D_MODEL))]


def _ffn_call(h, g, wg, wu, wo):
    n = h.shape[0]
    tm = min(TM_FFN, n)
    row = pl.BlockSpec((tm, D_MODEL), lambda i: (i, 0))
    return pl.pallas_call(
        _ffn_kernel,
        grid=(n // tm,),
        in_specs=[row] + _ffn_weight_specs(),
        out_specs=row,
        out_shape=jax.ShapeDtypeStruct((n, D_MODEL), F32),
        scratch_shapes=[pltpu.VMEM((tm, D_MODEL), F32)],
        compiler_params=pltpu.CompilerParams(dimension_semantics=("arbitrary",),
                                             vmem_limit_bytes=VMEM_LIMIT),
        name="ffn",
    )(h, g, wg, wu, wo)


def _merge_ffn_call(h, ya, yb, sg, bra, brb, wout, g, wg, wu, wo, fin, final):
    n = h.shape[0]
    tm = min(TM_FFN, n)
    row = lambda w: pl.BlockSpec((tm, w), lambda i: (i, 0))
    return pl.pallas_call(
        functools.partial(_merge_ffn_kernel, final=final),
        grid=(n // tm,),
        in_specs=[row(D_MODEL), row(D_RWKV), row(D_MLSTM), row(2 * D_MODEL),
                  _const_spec((D_RWKV, D_MODEL)), _const_spec((D_MLSTM, D_MODEL)),
                  _const_spec((D_MODEL, D_MODEL))] + _ffn_weight_specs() + [_const_spec((1, D_MODEL))],
        out_specs=row(D_MODEL),
        out_shape=jax.ShapeDtypeStruct((n, D_MODEL), F32),
        scratch_shapes=[pltpu.VMEM((tm, D_MODEL), F32)],
        compiler_params=pltpu.CompilerParams(dimension_semantics=("arbitrary",),
                                             vmem_limit_bytes=VMEM_LIMIT),
        name="merge_ffn",
    )(h, ya, yb, sg, bra, brb, wout, g, wg, wu, wo, fin)


def _bdot(a, b):
    return jnp.dot(a.astype(BF16), b.astype(BF16), preferred_element_type=F32)


def _bdot_nt(a, b):
    return lax.dot_general(a.astype(BF16), b.astype(BF16), (((1,), (1,)), ((), ())),
                           preferred_element_type=F32)


def _bdot_tn(a, b):
    return lax.dot_general(a.astype(BF16), b.astype(BF16), (((0,), (0,)), ((), ())),
                           preferred_element_type=F32)


def _split3(x):
    hi = x.astype(BF16)
    r1 = x - hi.astype(F32)
    mid = r1.astype(BF16)
    lo = (r1 - mid.astype(F32)).astype(BF16)
    return hi, mid, lo


def _log_sigmoid(x):
    return jnp.minimum(x, 0.0) - jnp.log(1.0 + jnp.exp(-jnp.abs(x)))


def _prep_rwkv(zr, zs, vfirst_ref, w, s, tt):
    z_rkv = zr[8:, :]
    z_sm = zs[8:, :]
    m_rkv = z_rkv + w["mu_rkv"][...] * (zr[7:7 + tt, :] - z_rkv)
    m_sm = z_sm + w["mu_sm"][...] * (zs[7:7 + tt, :] - z_sm)
    zr[:8, :] = zr[tt:, :]
    zs[:8, :] = zs[tt:, :]
    r = m_rkv[:, :D_RWKV]
    k = m_rkv[:, D_RWKV:2 * D_RWKV]
    v = m_rkv[:, 2 * D_RWKV:]
    xwa = m_sm[:, :128]
    xg_blk = m_sm[:, 128:]
    s["ifb"][...] = z_sm[:, 256:]
    if vfirst_ref is not None:
        vz = _dot(z_sm[:, 128:].astype(BF16), w["vup"][...])
        vg = _sigmoid(w["vbias"][...] + vz)
        v = v + (vfirst_ref[...] - v) * vg
    y = -(w["w0"][...] + _dot(jnp.tanh(xwa).astype(BF16), w["wup"][...]))
    softplus = jnp.maximum(y, 0.0) + jnp.log(1.0 + jnp.exp(-jnp.abs(y)))
    s["ld"][...] = -jnp.exp(-softplus - 0.5)
    a = _sigmoid(w["a0"][...] + _dot(xwa.astype(BF16), w["aup"][...]))
    s["g"][...] = _dot(_sigmoid(xg_blk).astype(BF16), w["gup"][...])
    kk = k * w["k_k"][...]
    kk = kk * lax.rsqrt(jnp.maximum(_seg_sum(kk * kk, w["seg"]), 1e-24))
    k2 = k * (1.0 + (a - 1.0) * w["k_a"][...])
    s["bonus"][...] = _seg_sum(r * k2 * w["r_k"][...], w["seg"]) * v
    s["r"][...] = r
    s["k"][...] = k2
    s["v"][...] = v
    s["kk"][...] = kk
    s["a"][...] = a


def _prep_mlstm(zm, w, s, tt):
    cw = w["convw"][...]
    nqk = 2 * D_MLSTM
    conv = w["convb"][...] + cw[CONV_W - 1:CONV_W] * zm[8:, :nqk]
    for sft in range(1, CONV_W):
        conv = conv + cw[CONV_W - 1 - sft:CONV_W - sft] * zm[8 - sft:8 - sft + tt, :nqk]
    qk = conv * _sigmoid(conv)
    s["q"][...] = qk[:, :D_MLSTM] * (MLSTM_HEAD ** -0.5)
    s["mk"][...] = qk[:, D_MLSTM:]
    s["mv"][...] = zm[8:, nqk:nqk + D_MLSTM]
    s["o"][...] = _sigmoid(zm[8:, nqk + D_MLSTM:])
    zm[:8, :] = zm[tt:, :]


def _rwkv_stages(s, w, y_ref, h_ref):
    C, N = CHUNK, RWKV_HEAD
    tt = y_ref.shape[0]
    ti = lax.broadcasted_iota(jnp.int32, (C, C), 0)
    si = lax.broadcasted_iota(jnp.int32, (C, C), 1)
    strict = si < ti
    incl = si <= ti
    eye = (si == ti).astype(F32)

    tr = lax.broadcasted_iota(jnp.int32, (tt, tt), 0)
    tc = lax.broadcasted_iota(jnp.int32, (tt, tt), 1)
    same = (tr // C) == (tc // C)
    tri = (same & (tc <= tr)).astype(BF16)
    blk = same.astype(BF16)
    lw = s["ld"][...]
    parts = _split3(lw)
    cs = sum(_dot(tri, p) for p in parts)
    cs_tot = sum(_dot(blk, p) for p in parts)
    yield

    kk = s["kk"][...]
    kka = kk * s["a"][...]
    k2 = s["k"][...]
    g_inv = jnp.exp(-cs)
    g_end = jnp.exp(cs_tot - cs)
    g_tot = jnp.exp(cs_tot)
    Rt = (s["r"][...] * jnp.exp(cs)).astype(BF16)
    At = (-kk * jnp.exp(cs - lw)).astype(BF16)
    Kb = (k2 * g_inv).astype(BF16)
    Bb = (kka * g_inv).astype(BF16)
    Ke = (k2 * g_end).astype(BF16)
    Be = (kka * g_end).astype(BF16)
    V = s["v"][...].astype(BF16)

    nc = tt // C
    inst = [(c, h) for c in range(nc) for h in range(RWKV_HEADS)]
    sl = lambda x, c, h: x[c * C:(c + 1) * C, h * N:(h + 1) * N]
    each = lambda f, *lists: [f(*xs) for xs in zip(*lists)]

    At_i = [sl(At, c, h) for c, h in inst]
    Rt_i = [sl(Rt, c, h) for c, h in inst]
    V_i = [sl(V, c, h) for c, h in inst]
    P = [_bdot_nt(jnp.concatenate([a, r], axis=0),
                  jnp.concatenate([sl(Kb, c, h), sl(Bb, c, h)], axis=0))
         for a, r, (c, h) in zip(At_i, Rt_i, inst)]
    yield
    A_ak = [jnp.where(strict, p[:C, :C], 0.0) for p in P]
    L = [jnp.where(strict, p[:C, C:], 0.0) for p in P]
    A_rk = [jnp.where(incl, p[C:, :C], 0.0) for p in P]
    A_rb = [jnp.where(incl, p[C:, C:], 0.0).astype(BF16) for p in P]
    AV = each(lambda ak, rk, v: _bdot(jnp.concatenate([ak, rk], axis=0), v), A_ak, A_rk, V_i)
    yield
    X = [eye + l for l in L]
    M = each(lambda l: _bdot(l, l), L)
    yield
    for _ in range(4):
        XM = each(lambda x, m: _bdot(jnp.concatenate([x, m], axis=0), m), X, M)
        yield
        X = each(lambda x, xm: x + xm[:C], X, XM)
        M = [xm[C:] for xm in XM]
    Tm = each(lambda x, m: x + _bdot(x, m), X, M)
    yield
    W = each(lambda t, a, av: _bdot(t, jnp.concatenate([a, av[:C].astype(BF16)], axis=1)),
             Tm, At_i, AV)
    yield
    Z = each(_bdot, A_rb, W)
    yield
    G = [_bdot_tn(sl(Be, c, h), w_) for (c, h), w_ in zip(inst, W)]
    KV = [_bdot_tn(sl(Ke, c, h), v) for (c, h), v in zip(inst, V_i)]
    yield
    RM = [jnp.concatenate([r.astype(F32) + z[:, :N],
                           eye * g_tot[c * C:c * C + 1, h * N:(h + 1) * N] + g[:, :N]], axis=0).astype(BF16)
          for r, z, g, (c, h) in zip(Rt_i, Z, G, inst)]
    Y0 = each(lambda av, z: av[C:] + z[:, N:], AV, Z)
    G0 = each(lambda kv, g: kv + g[:, N:], KV, G)

    Hs = [h_ref[h] for h in range(RWKV_HEADS)]
    y_rows = []
    for c in range(nc):
        RMH = [_bdot(RM[c * RWKV_HEADS + h], Hs[h]) for h in range(RWKV_HEADS)]
        yield
        y_rows.append(jnp.concatenate([RMH[h][:C] + Y0[c * RWKV_HEADS + h]
                                       for h in range(RWKV_HEADS)], axis=1))
        Hs = [RMH[h][C:] + G0[c * RWKV_HEADS + h] for h in range(RWKV_HEADS)]
    for h in range(RWKV_HEADS):
        h_ref[h] = Hs[h]
    y = jnp.concatenate(y_rows, axis=0)
    inv_n = 1.0 / N
    mu = _seg_sum(y, w["seg"]) * inv_n
    yield
    d = y - mu
    var = _seg_sum(d * d, w["seg"]) * inv_n
    yield
    yn = d * lax.rsqrt(var + RWKV_GN_EPS)
    y_ref[...] = (yn * w["gnw"][...] + w["gnb"][...] + s["bonus"][...]) * s["g"][...]


def _mlstm_stages(s, w, y_ref, cn_ref, m_ref):
    q_ref, k_ref, v_ref, o_ref, if_ref = s["q"], s["mk"], s["mv"], s["o"], s["ifb"]
    brow_ref, bcol_ref, nw_ref = w["if_brow"], w["if_bcol"], w["ml_nw"]
    C, d = CHUNK, MLSTM_HEAD
    tt = q_ref.shape[0]
    nc = tt // C
    NH = MLSTM_HEADS
    ti = lax.broadcasted_iota(jnp.int32, (C, C), 0)
    si = lax.broadcasted_iota(jnp.int32, (C, C), 1)
    causal = si <= ti
    ones = jnp.ones((C, d), BF16)
    i0 = IF_LANE
    f0 = IF_LANE + NH

    tr = lax.broadcasted_iota(jnp.int32, (tt, tt), 0)
    tc = lax.broadcasted_iota(jnp.int32, (tt, tt), 1)
    same = (tr // C) == (tc // C)
    tril = (same & (tc <= tr)).astype(BF16)
    triu = (same & (tr <= tc)).astype(BF16)
    pre = if_ref[...] + brow_ref[...]
    bcum_cols = sum(_dot(tril, p) for p in _split3(_log_sigmoid(pre)))
    preT = jnp.transpose(if_ref[...])[i0:i0 + 8, :] + bcol_ref[...]
    row8 = lax.broadcasted_iota(jnp.int32, (8, 1), 0)
    bcum_rows = sum(_dot(p, triu) for p in _split3(jnp.where(row8 < NH, preT, _log_sigmoid(preT))))
    yield

    inst = [(c, h) for c in range(nc) for h in range(NH)]
    rs = lambda c: slice(c * C, (c + 1) * C)
    hs_ = lambda h: slice(h * d, (h + 1) * d)
    q_i = [q_ref[rs(c), hs_(h)].astype(BF16) for c, h in inst]
    k_i = [k_ref[rs(c), hs_(h)] for c, h in inst]
    ve_i = [jnp.concatenate([v_ref[rs(c), hs_(h)].astype(BF16), ones], axis=1) for c, h in inst]
    bc_i = [bcum_cols[rs(c), f0 + h:f0 + h + 1] for c, h in inst]
    e_c = [pre[rs(c), i0 + h:i0 + h + 1] - bc for (c, h), bc in zip(inst, bc_i)]
    e_r = [preT[h:h + 1, rs(c)] - bcum_rows[NH + h:NH + h + 1, rs(c)] for c, h in inst]
    a_col = [jnp.max(jnp.where(causal, e, -jnp.inf), axis=-1, keepdims=True) for e in e_r]
    yield
    D0 = [jnp.where(causal, jnp.exp(e - a), 0.0) for e, a in zip(e_r, a_col)]
    QK = [_bdot_nt(q, k) for q, k in zip(q_i, k_i)]
    yield
    S0 = [(qk * dm).astype(BF16) for qk, dm in zip(QK, D0)]
    a_end = [a[C - 1:C, :] for a in a_col]
    KV0 = [_bdot_tn(jnp.exp(e - ae) * k, ve) for e, ae, k, ve in zip(e_c, a_end, k_i, ve_i)]
    yield
    intra = [_dot(s0, ve) for s0, ve in zip(S0, ve_i)]
    yield

    cn = [cn_ref[h] for h in range(NH)]
    m_prev = [m_ref[h][0:1, 0:1] for h in range(NH)]
    hh = []
    for c in range(nc):
        ii = [c * NH + h for h in range(NH)]
        QC = [_bdot(q_i[i], cn[h]) for h, i in enumerate(ii)]
        yield
        for h, i in enumerate(ii):
            mu_t = jnp.maximum(m_prev[h], a_col[i])
            tot = jnp.exp(m_prev[h] - mu_t) * QC[h] + jnp.exp(a_col[i] - mu_t) * intra[i]
            hh.append(tot[:, :d] / jnp.maximum(jnp.abs(tot[:, d:]), jnp.exp(-(bc_i[i] + mu_t))))
            mu_end = jnp.maximum(m_prev[h], a_end[i])
            cn[h] = jnp.exp(m_prev[h] - mu_end) * cn[h] + jnp.exp(a_end[i] - mu_end) * KV0[i]
            m_prev[h] = bc_i[i][C - 1:C, :] + mu_end
    for h in range(NH):
        cn_ref[h] = cn[h]
        m_ref[h] = jnp.broadcast_to(m_prev[h], (8, 128))
    mean = [jnp.mean(x, axis=-1, keepdims=True) for x in hh]
    yield
    dd = [x - m for x, m in zip(hh, mean)]
    var = [jnp.mean(x * x, axis=-1, keepdims=True) for x in dd]
    yield
    for (c, h), x, v in zip(inst, dd, var):
        y_ref[rs(c), hs_(h)] = o_ref[rs(c), hs_(h)] * (x * lax.rsqrt(v + MLSTM_NORM_EPS) * nw_ref[:, hs_(h)])


def _run_interleaved(streams, head):
    live = list(streams)
    for k in head:
        next(streams[k], None)
    while live:
        live = [g for g in live if next(g, StopIteration) is not StopIteration]


def _mixer_kernel(*refs, has_vres, nt, wnames, snames):
    refs = list(refs)
    h_ref = refs.pop(0)
    vfirst_ref = refs.pop(0) if has_vres else None
    w = {n: refs.pop(0) for n in wnames}
    ya_o, yb_o, sg_o = refs.pop(0), refs.pop(0), refs.pop(0)
    vfirst_o = None if has_vres else refs.pop(0)
    zr, zs, zm, xb_ref = (refs.pop(0) for _ in range(4))
    s = {n: refs.pop(0) for n in snames}
    h_state, cn_state, m_state = refs
    tt = h_ref.shape[0]
    i = pl.program_id(0)

    @pl.when(i == 0)
    def _():
        for ref in (zr, zs, zm, h_state, cn_state, m_state):
            ref[...] = jnp.zeros_like(ref)

    @pl.when(lax.rem(i + nt - 1, nt) == 0)
    def _():
        for ref in (zr, zs, zm):
            ref[:8, :] = jnp.zeros((8, ref.shape[1]), F32)
        h_state[...] = jnp.zeros_like(h_state)
        cn_state[...] = jnp.zeros_like(cn_state)
        m_state[...] = jnp.zeros_like(m_state)

    xb_ref[...] = _rms(h_ref[...], w["mix_norm"][...]).astype(BF16)

    def project(w_ref, lo, hi):
        return _dot(xb_ref[...], w_ref[:, lo:hi])

    def gate_stages(pieces):
        for lo, hi in pieces:
            sg_o[:, lo:hi] = _sigmoid(project(w["w_gate"], lo, hi))
            yield

    def ml_stages(pieces):
        for lo, hi in pieces:
            zm[8:, lo:hi] = project(w["w_ml"], lo, hi)
            yield

    PW = 512
    cols = lambda width: [(lo, lo + PW) for lo in range(0, width, PW)]
    for _ in gate_stages(cols(2 * D_MODEL)[:3]):
        pass
    _prep_rwkv(zr, zs, vfirst_ref, w, s, tt)
    if vfirst_o is not None:
        vfirst_o[...] = s["v"][...]
    zr[8:, :] = project(w["w_rkv"], 0, 3 * D_RWKV)
    zs[8:, :] = project(w["w_sm"], 0, SM_W)
    _prep_mlstm(zm, w, s, tt)
    streams = [_rwkv_stages(s, w, ya_o, h_state), _mlstm_stages(s, w, yb_o, cn_state, m_state),
               ml_stages(cols(4 * D_MLSTM)), gate_stages(cols(2 * D_MODEL)[3:])]
    _run_interleaved(streams, head=(0, 1, 2, 2, 2))


def _mixer_call(h3, vfirst, p):
    B, T, _ = h3.shape
    tt = min(TT_MIX, T)
    nt = T // tt
    n = B * nt
    has_vres = vfirst is not None
    cur = lambda wd: pl.BlockSpec((None, tt, wd), lambda i: (jnp.minimum(i, n - 1) // nt,
                                                            jnp.minimum(i, n - 1) % nt, 0))
    lag = lambda wd: pl.BlockSpec((None, tt, wd), lambda i: (jnp.maximum(i - 1, 0) // nt,
                                                            jnp.maximum(i - 1, 0) % nt, 0))
    wnames = ("mix_norm", "w_rkv", "w_sm", "w_ml", "w_gate", "mu_rkv", "mu_sm", "w0", "wup", "a0", "aup",
              "gup", "k_k", "k_a", "r_k", "vbias", "vup", "convw", "convb", "seg", "gnw", "gnb",
              "if_brow", "if_bcol", "ml_nw")
    snames = ("r", "k", "v", "ld", "kk", "a", "g", "bonus", "q", "mk", "mv", "o", "ifb")
    weights = [p[nm] for nm in wnames]
    in_specs = [cur(D_MODEL)] + ([lag(D_RWKV)] if has_vres else []) + [_const_spec(x.shape) for x in weights]
    out_specs = [lag(D_RWKV), lag(D_MLSTM), cur(2 * D_MODEL)] + ([] if has_vres else [lag(D_RWKV)])
    out_w = [D_RWKV, D_MLSTM, 2 * D_MODEL] + ([] if has_vres else [D_RWKV])
    scratch = [pltpu.VMEM((tt + 8, 3 * D_RWKV), F32), pltpu.VMEM((tt + 8, SM_W), F32),
               pltpu.VMEM((tt + 8, 4 * D_MLSTM), F32), pltpu.VMEM((tt, D_MODEL), BF16)]
    scratch += [pltpu.VMEM((tt, 128 if nm == "ifb" else D_RWKV), F32) for nm in snames]
    scratch += [pltpu.VMEM((RWKV_HEADS, RWKV_HEAD, RWKV_HEAD), F32),
                pltpu.VMEM((MLSTM_HEADS, MLSTM_HEAD, 2 * MLSTM_HEAD), F32),
                pltpu.VMEM((MLSTM_HEADS, 8, 128), F32)]
    return pl.pallas_call(
        functools.partial(_mixer_kernel, has_vres=has_vres, nt=nt, wnames=wnames, snames=snames),
        grid=(n + 1,),
        in_specs=in_specs,
        out_specs=out_specs,
        out_shape=[jax.ShapeDtypeStruct((B, T, wd), F32) for wd in out_w],
        scratch_shapes=scratch,
        compiler_params=pltpu.CompilerParams(dimension_semantics=("arbitrary",),
                                             vmem_limit_bytes=VMEM_LIMIT),
        name="mixer",
    )(*([h3] + ([vfirst] if has_vres else []) + weights))


def _pad_rows(w, rows, at=0):
    out = jnp.zeros((rows, w.shape[1]), w.dtype)
    return out.at[at:at + w.shape[0]].set(w)


def _ffn_params(norm, w_in, w_out):
    nc = D_FF // FF_CHUNK
    wg = w_in[:, :D_FF].reshape(D_MODEL, nc, FF_CHUNK).transpose(1, 0, 2).astype(BF16)
    wu = w_in[:, D_FF:].reshape(D_MODEL, nc, FF_CHUNK).transpose(1, 0, 2).astype(BF16)
    wo = w_out.reshape(nc, FF_CHUNK, D_MODEL).astype(BF16)
    return norm.reshape(1, D_MODEL), wg, wu, wo


def _mixer_params(l, mix_norm, w_in, shift_mu, rw_w0, rw_w_up, rw_a0, rw_a_up, rw_g_up, rw_k_k, rw_k_a,
                  rw_r_k, vres_down, vres_up, vres_bias, ml_conv_w, ml_conv_b, ml_i_bias, ml_f_bias):
    w = w_in[l]
    o_ml = N_RW_IN
    o_gate = N_RW_IN + N_ML_IN
    w_rkv = w[:, :3 * D_RWKV]
    w_lora = w[:, 3 * D_RWKV:N_RW_IN]
    w_if = w[:, o_ml + 4 * D_MLSTM:o_gate]
    w_vd = vres_down[l - 1] if l > 0 else jnp.zeros((D_MODEL, LORA_V), F32)
    pad = jnp.zeros((D_MODEL, SM_W - SM_VRES - LORA_V), F32)
    w_sm = jnp.concatenate([w_lora, w_if, w_vd, pad], axis=1)
    mu = shift_mu[l]
    mu_sm = jnp.concatenate([mu[3 * D_RWKV:], jnp.zeros((SM_W - (N_RW_IN - 3 * D_RWKV),), F32)])
    row = lambda x: x.reshape(1, -1)
    p = {
        "mix_norm": row(mix_norm[l]),
        "w_rkv": w_rkv.astype(BF16),
        "w_sm": w_sm.astype(BF16),
        "w_ml": w[:, o_ml:o_ml + 4 * D_MLSTM].astype(BF16),
        "w_gate": w[:, o_gate:].astype(BF16),
        "mu_rkv": row(mu[:3 * D_RWKV]),
        "mu_sm": row(mu_sm),
        "w0": row(rw_w0[l]),
        "wup": _pad_rows(rw_w_up[l], 128, 0).astype(BF16),
        "a0": row(rw_a0[l]),
        "aup": _pad_rows(rw_a_up[l], 128, LORA_W).astype(BF16),
        "gup": _pad_rows(rw_g_up[l], 256, 0).astype(BF16),
        "k_k": row(rw_k_k[l]),
        "k_a": row(rw_k_a[l]),
        "r_k": row(rw_r_k[l]),
        "vbias": row(vres_bias[l - 1]) if l > 0 else jnp.zeros((1, D_RWKV), F32),
        "vup": (_pad_rows(vres_up[l - 1], 256, SM_VRES - 128) if l > 0
                else jnp.zeros((256, D_RWKV), F32)).astype(BF16),
        "convw": ml_conv_w[l],
        "convb": row(ml_conv_b[l]),
    }
    gate_bias = jnp.concatenate([ml_i_bias[l], ml_f_bias[l]])
    p["if_brow"] = jnp.zeros((1, 128), F32).at[0, IF_LANE:IF_LANE + 8].set(gate_bias)
    p["if_bcol"] = gate_bias.reshape(8, 1)
    return p


def kernel(x, ffn1_norm, ffn1_w_in, ffn1_w_out, mix_norm, w_in, shift_mu, rw_w0, rw_w_up, rw_a0, rw_a_up, rw_g_up, rw_k_k, rw_k_a, rw_r_k, rw_gn_w, rw_gn_b, vres_down, vres_up, vres_bias, ml_conv_w, ml_conv_b, ml_i_bias, ml_f_bias, ml_norm_w, br_a, br_b, w_out, ffn2_norm, ffn2_w_in, ffn2_w_out, final_norm):
    B, T, D = x.shape
    n = B * T
    assert D == D_MODEL and T % min(TT_MIX, T) == 0 and min(TT_MIX, T) % CHUNK == 0
    assert n % min(TM_FFN, n) == 0
    lane = jnp.arange(D_RWKV) // RWKV_HEAD
    seg = (lane[:, None] == lane[None, :]).astype(BF16)
    fin = final_norm.reshape(1, D_MODEL)
    h = x.reshape(n, D)
    v_first = None
    for l in range(DEPTH):
        h = _ffn_call(h, *_ffn_params(ffn1_norm[l], ffn1_w_in[l], ffn1_w_out[l]))
        p = _mixer_params(l, mix_norm, w_in, shift_mu, rw_w0, rw_w_up, rw_a0, rw_a_up, rw_g_up, rw_k_k,
                          rw_k_a, rw_r_k, vres_down, vres_up, vres_bias, ml_conv_w, ml_conv_b,
                          ml_i_bias, ml_f_bias)
        p.update(seg=seg, gnw=rw_gn_w[l].reshape(1, -1), gnb=rw_gn_b[l].reshape(1, -1),
                 ml_nw=ml_norm_w[l].reshape(1, -1))
        outs = _mixer_call(h.reshape(B, T, D), v_first, p)
        ya, yb, sg = outs[:3]
        if l == 0:
            v_first = outs[3]
        h = _merge_ffn_call(h, ya.reshape(n, -1), yb.reshape(n, -1), sg.reshape(n, -1),
                            br_a[l].astype(BF16), br_b[l].astype(BF16), w_out[l].astype(BF16),
                            *_ffn_params(ffn2_norm[l], ffn2_w_in[l], ffn2_w_out[l]), fin,
                            final=(l == DEPTH - 1))
    return h.reshape(B, T, D)
```

```python
import functools

import jax
import jax.numpy as jnp
from jax import lax
from jax.experimental import pallas as pl
from jax.experimental.pallas import tpu as pltpu

F32 = jnp.float32
BF16 = jnp.bfloat16

D_MODEL = 1024
DEPTH = 4
CHUNK = 64
D_RWKV = 512
RWKV_HEAD = 64
RWKV_HEADS = 8
LORA_W, LORA_A, LORA_V, LORA_G = 64, 64, 32, 160
RWKV_GN_EPS = 64e-5
D_MLSTM = 512
MLSTM_HEADS = 4
MLSTM_HEAD = 128
CONV_W = 4
MLSTM_NORM_EPS = 1e-5
D_FF = 2816
NORM_EPS = 1e-6
N_RW_IN = 3 * D_RWKV + LORA_W + LORA_A + LORA_G
N_ML_IN = 4 * D_MLSTM + 2 * MLSTM_HEADS

SM_W = 384
SM_IF = 288
SM_VRES = 296
IF_LANE = SM_IF - 256

FF_CHUNK = 256
TM_FFN = 512
TT_MIX = 256
VMEM_LIMIT = 56 * 1024 * 1024


def _const_spec(shape):
    nd = len(shape)
    return pl.BlockSpec(shape, lambda *_: (0,) * nd, pipeline_mode=pl.Buffered(1))


def _rms(h, g):
    return h * lax.rsqrt(jnp.mean(h * h, axis=-1, keepdims=True) + NORM_EPS) * g


def _sigmoid(x):
    return 0.5 * jnp.tanh(0.5 * x) + 0.5


def _dot(a, b):
    return jnp.dot(a, b, preferred_element_type=F32)


def _seg_sum(x, seg_ref):
    xb = x.astype(BF16)
    gw = seg_ref.shape[0]
    return jnp.concatenate([_dot(xb[:, lo:lo + gw], seg_ref[...]) for lo in range(0, x.shape[1], gw)], axis=1)


def _ffn_body(h, g_ref, wg_ref, wu_ref, wo_ref, acc_ref):
    xb = _rms(h, g_ref[...]).astype(BF16)
    nc = D_FF // FF_CHUNK
    for c in range(nc):
        gate = _dot(xb, wg_ref[c])
        up = _dot(xb, wu_ref[c])
        act = (gate * _sigmoid(gate) * up).astype(BF16)
        part = _dot(act, wo_ref[c])
        if c == 0:
            acc_ref[...] = part
        elif c < nc - 1:
            acc_ref[...] += part
    return h + 0.5 * (acc_ref[...] + part)


def _ffn_kernel(h_ref, g_ref, wg_ref, wu_ref, wo_ref, out_ref, acc_ref):
    out_ref[...] = _ffn_body(h_ref[...], g_ref, wg_ref, wu_ref, wo_ref, acc_ref)


def _merge_ffn_kernel(h_ref, ya_ref, yb_ref, sg_ref, bra_ref, brb_ref, wout_ref,
                      g_ref, wg_ref, wu_ref, wo_ref, fin_ref, out_ref, acc_ref, *, final):
    sg = sg_ref[...]
    pa = _dot(ya_ref[...].astype(BF16), bra_ref[...])
    pb = _dot(yb_ref[...].astype(BF16), brb_ref[...])
    u = sg[:, :D_MODEL] * pa + sg[:, D_MODEL:] * pb
    h = h_ref[...] + _dot(u.astype(BF16), wout_ref[...])
    h = _ffn_body(h, g_ref, wg_ref, wu_ref, wo_ref, acc_ref)
    if final:
        h = _rms(h, fin_ref[...])
    out_ref[...] = h


def _ffn_weight_specs():
    nc = D_FF // FF_CHUNK
    return [_const_spec((1, D_MODEL)), _const_spec((nc, D_MODEL, FF_CHUNK)),
            _const_spec((nc, D_MODEL, FF_CHUNK)), _const_spec((nc, FF_CHUNK, D_MODEL))]


def _ffn_call(h, g, wg, wu, wo):
    n = h.shape[0]
    tm = min(TM_FFN, n)
    row = pl.BlockSpec((tm, D_MODEL), lambda i: (i, 0))
    return pl.pallas_call(
        _ffn_kernel,
        grid=(n // tm,),
        in_specs=[row] + _ffn_weight_specs(),
        out_specs=row,
        out_shape=jax.ShapeDtypeStruct((n, D_MODEL), F32),
        scratch_shapes=[pltpu.VMEM((tm, D_MODEL), F32)],
        compiler_params=pltpu.CompilerParams(dimension_semantics=("arbitrary",),
                                             vmem_limit_bytes=VMEM_LIMIT),
        name="ffn",
    )(h, g, wg, wu, wo)


def _merge_ffn_call(h, ya, yb, sg, bra, brb, wout, g, wg, wu, wo, fin, final):
    n = h.shape[0]
    tm = min(TM_FFN, n)
    row = lambda w: pl.BlockSpec((tm, w), lambda i: (i, 0))
    return pl.pallas_call(
        functools.partial(_merge_ffn_kernel, final=final),
        grid=(n // tm,),
        in_specs=[row(D_MODEL), row(D_RWKV), row(D_MLSTM), row(2 * D_MODEL),
                  _const_spec((D_RWKV, D_MODEL)), _const_spec((D_MLSTM, D_MODEL)),
                  _const_spec((D_MODEL, D_MODEL))] + _ffn_weight_specs() + [_const_spec((1, D_MODEL))],
        out_specs=row(D_MODEL),
        out_shape=jax.ShapeDtypeStruct((n, D_MODEL), F32),
        scratch_shapes=[pltpu.VMEM((tm, D_MODEL), F32)],
        compiler_params=pltpu.CompilerParams(dimension_semantics=("arbitrary",),
                                             vmem_limit_bytes=VMEM_LIMIT),
        name="merge_ffn",
    )(h, ya, yb, sg, bra, brb, wout, g, wg, wu, wo, fin)


def _bdot(a, b):
    return jnp.dot(a.astype(BF16), b.astype(BF16), preferred_element_type=F32)


def _bdot_nt(a, b):
    return lax.dot_general(a.astype(BF16), b.astype(BF16), (((1,), (1,)), ((), ())),
                           preferred_element_type=F32)


def _bdot_tn(a, b):
    return lax.dot_general(a.astype(BF16), b.astype(BF16), (((0,), (0,)), ((), ())),
                           preferred_element_type=F32)


def _split3(x):
    hi = x.astype(BF16)
    r1 = x - hi.astype(F32)
    mid = r1.astype(BF16)
    lo = (r1 - mid.astype(F32)).astype(BF16)
    return hi, mid, lo


def _log_sigmoid(x):
    return jnp.minimum(x, 0.0) - jnp.log(1.0 + jnp.exp(-jnp.abs(x)))


def _prep_stages(zr, zs, zm, vfirst_ref, vfirst_o, keep_vfirst, w, s, xb_ref, sg_o, tt):
    D = D_RWKV

    def shifted(buf, lo, hi, mu):
        z = buf[8:, lo:hi]
        return z + mu * (buf[7:7 + tt, lo:hi] - z)

    def refill(buf, w_ref, lo, hi):
        buf[:8, lo:hi] = buf[tt:, lo:hi]
        buf[8:, lo:hi] = _dot(xb_ref[...], w_ref[:, lo:hi])

    z_sm = zs[8:, :]
    m_sm = shifted(zs, 0, SM_W, w["mu_sm"][...])
    s["ifb"][...] = z_sm[:, 256:]
    xwa = m_sm[:, :128]
    y = -(w["w0"][...] + _dot(jnp.tanh(xwa).astype(BF16), w["wup"][...]))
    softplus = jnp.maximum(y, 0.0) + jnp.log(1.0 + jnp.exp(-jnp.abs(y)))
    s["ld"][...] = -jnp.exp(-softplus - 0.5)
    a = _sigmoid(w["a0"][...] + _dot(xwa.astype(BF16), w["aup"][...]))
    s["a"][...] = a
    s["g"][...] = _dot(_sigmoid(m_sm[:, 128:]).astype(BF16), w["gup"][...])
    if vfirst_ref is not None:
        vg = _sigmoid(w["vbias"][...] + _dot(z_sm[:, 128:].astype(BF16), w["vup"][...]))
    refill(zs, w["w_sm"], 0, SM_W)
    yield

    mu = w["mu_rkv"]
    k = shifted(zr, D, 2 * D, mu[:, D:2 * D])
    refill(zr, w["w_rkv"], D, 2 * D)
    kk = k * w["k_k"][...]
    s["kk"][...] = kk * lax.rsqrt(jnp.maximum(_seg_sum(kk * kk, w["seg"]), 1e-24))
    k2 = k * (1.0 + (a - 1.0) * w["k_a"][...])
    s["k"][...] = k2
    yield

    v = shifted(zr, 2 * D, 3 * D, mu[:, 2 * D:])
    refill(zr, w["w_rkv"], 2 * D, 3 * D)
    if vfirst_ref is not None:
        v = v + (vfirst_ref[...] - v) * vg
    s["v"][...] = v
    if vfirst_o is not None:
        @pl.when(keep_vfirst)
        def _():
            vfirst_o[...] = v
    yield

    r = shifted(zr, 0, D, mu[:, :D])
    refill(zr, w["w_rkv"], 0, D)
    s["r"][...] = r
    s["bonus"][...] = _seg_sum(r * k2 * w["r_k"][...], w["seg"]) * v
    yield

    cw = w["convw"]
    for name, lo, scale in (("q", 0, MLSTM_HEAD ** -0.5), ("mk", D_MLSTM, 1.0)):
        hi = lo + D_MLSTM
        conv = w["convb"][:, lo:hi] + cw[CONV_W - 1:CONV_W, lo:hi] * zm[8:, lo:hi]
        for sft in range(1, CONV_W):
            conv = conv + cw[CONV_W - 1 - sft:CONV_W - sft, lo:hi] * zm[8 - sft:8 - sft + tt, lo:hi]
        refill(zm, w["w_ml"], lo, hi)
        s[name][...] = conv * _sigmoid(conv) * scale
        yield
    s["mv"][...] = zm[8:, 2 * D_MLSTM:3 * D_MLSTM]
    refill(zm, w["w_ml"], 2 * D_MLSTM, 3 * D_MLSTM)
    yield
    s["o"][...] = _sigmoid(zm[8:, 3 * D_MLSTM:])
    refill(zm, w["w_ml"], 3 * D_MLSTM, 4 * D_MLSTM)
    yield

    for lo in range(0, 2 * D_MODEL, D_MLSTM):
        sg_o[:, lo:lo + D_MLSTM] = _sigmoid(_dot(xb_ref[...], w["w_gate"][:, lo:lo + D_MLSTM]))
        yield


def _rwkv_stages(s, w, y_ref, h_ref):
    C, N = CHUNK, RWKV_HEAD
    tt = y_ref.shape[0]
    ti = lax.broadcasted_iota(jnp.int32, (C, C), 0)
    si = lax.broadcasted_iota(jnp.int32, (C, C), 1)
    strict = si < ti
    incl = si <= ti
    eye = (si == ti).astype(F32)

    tr = lax.broadcasted_iota(jnp.int32, (tt, tt), 0)
    tc = lax.broadcasted_iota(jnp.int32, (tt, tt), 1)
    same = (tr // C) == (tc // C)
    tri = (same & (tc <= tr)).astype(BF16)
    lw = s["ld"][...]
    kk = s["kk"][...]
    kka = kk * s["a"][...]
    k2 = s["k"][...]
    r_in = s["r"][...]
    V = s["v"][...].astype(BF16)
    bonus = s["bonus"][...]
    g_out = s["g"][...]
    parts = _split3(lw)
    cs = sum(_dot(tri, p) for p in parts)
    nc = tt // C
    cs_tot = jnp.concatenate([jnp.broadcast_to(cs[(c + 1) * C - 1:(c + 1) * C, :], (C, D_RWKV))
                              for c in range(nc)], axis=0)
    yield

    g_inv = jnp.exp(-cs)
    g_end = jnp.exp(cs_tot - cs)
    g_tot = jnp.exp(cs_tot)
    Rt = (r_in * jnp.exp(cs)).astype(BF16)
    At = (-kk * jnp.exp(cs - lw)).astype(BF16)
    Kb = (k2 * g_inv).astype(BF16)
    Bb = (kka * g_inv).astype(BF16)
    Ke = (k2 * g_end).astype(BF16)
    Be = (kka * g_end).astype(BF16)

    nc = tt // C
    inst = [(c, h) for c in range(nc) for h in range(RWKV_HEADS)]
    sl = lambda x, c, h: x[c * C:(c + 1) * C, h * N:(h + 1) * N]
    each = lambda f, *lists: [f(*xs) for xs in zip(*lists)]

    At_i = [sl(At, c, h) for c, h in inst]
    Rt_i = [sl(Rt, c, h) for c, h in inst]
    V_i = [sl(V, c, h) for c, h in inst]
    P = [_bdot_nt(jnp.concatenate([a, r], axis=0),
                  jnp.concatenate([sl(Kb, c, h), sl(Bb, c, h)], axis=0))
         for a, r, (c, h) in zip(At_i, Rt_i, inst)]
    yield
    A_ak = [jnp.where(strict, p[:C, :C], 0.0) for p in P]
    L = [jnp.where(strict, p[:C, C:], 0.0) for p in P]
    A_rk = [jnp.where(incl, p[C:, :C], 0.0) for p in P]
    A_rb = [jnp.where(incl, p[C:, C:], 0.0).astype(BF16) for p in P]
    AV = each(lambda ak, rk, v: _bdot(jnp.concatenate([ak, rk], axis=0), v), A_ak, A_rk, V_i)
    yield
    X = [eye + l for l in L]
    M = each(lambda l: _bdot(l, l), L)
    yield
    for _ in range(4):
        XM = each(lambda x, m: _bdot(jnp.concatenate([x, m], axis=0), m), X, M)
        yield
        X = each(lambda x, xm: x + xm[:C], X, XM)
        M = [xm[C:] for xm in XM]
    Tm = each(lambda x, m: x + _bdot(x, m), X, M)
    yield
    W = each(lambda t, a, av: _bdot(t, jnp.concatenate([a, av[:C].astype(BF16)], axis=1)),
             Tm, At_i, AV)
    yield
    Z = each(_bdot, A_rb, W)
    yield
    G = [_bdot_tn(sl(Be, c, h), w_) for (c, h), w_ in zip(inst, W)]
    KV = [_bdot_tn(sl(Ke, c, h), v) for (c, h), v in zip(inst, V_i)]
    yield
    RM = [jnp.concatenate([r.astype(F32) + z[:, :N],
                           eye * g_tot[c * C:c * C + 1, h * N:(h + 1) * N] + g[:, :N]], axis=0).astype(BF16)
          for r, z, g, (c, h) in zip(Rt_i, Z, G, inst)]
    Y0 = each(lambda av, z: av[C:] + z[:, N:], AV, Z)
    G0 = each(lambda kv, g: kv + g[:, N:], KV, G)

    Hs = [h_ref[h] for h in range(RWKV_HEADS)]
    y_rows = []
    for c in range(nc):
        RMH = [_bdot(RM[c * RWKV_HEADS + h], Hs[h]) for h in range(RWKV_HEADS)]
        yield
        y_rows.append(jnp.concatenate([RMH[h][:C] + Y0[c * RWKV_HEADS + h]
                                       for h in range(RWKV_HEADS)], axis=1))
        Hs = [RMH[h][C:] + G0[c * RWKV_HEADS + h] for h in range(RWKV_HEADS)]
    for h in range(RWKV_HEADS):
        h_ref[h] = Hs[h]
    y = jnp.concatenate(y_rows, axis=0)
    inv_n = 1.0 / N
    mu = _seg_sum(y, w["seg"]) * inv_n
    yield
    d = y - mu
    var = _seg_sum(d * d, w["seg"]) * inv_n
    yield
    yn = d * lax.rsqrt(var + RWKV_GN_EPS)
    y_ref[...] = (yn * w["gnw"][...] + w["gnb"][...] + bonus) * g_out


def _mlstm_stages(s, w, y_ref, cn_ref, m_ref):
    q_ref, k_ref, v_ref, o_ref, if_ref = s["q"], s["mk"], s["mv"], s["o"], s["ifb"]
    brow_ref, bcol_ref, nw_ref = w["if_brow"], w["if_bcol"], w["ml_nw"]
    C, d = CHUNK, MLSTM_HEAD
    tt = q_ref.shape[0]
    nc = tt // C
    NH = MLSTM_HEADS
    ti = lax.broadcasted_iota(jnp.int32, (C, C), 0)
    si = lax.broadcasted_iota(jnp.int32, (C, C), 1)
    causal = si <= ti
    ones = jnp.ones((C, d), BF16)
    i0 = IF_LANE
    f0 = IF_LANE + NH

    tr = lax.broadcasted_iota(jnp.int32, (tt, tt), 0)
    tc = lax.broadcasted_iota(jnp.int32, (tt, tt), 1)
    same = (tr // C) == (tc // C)
    tril = (same & (tc <= tr)).astype(BF16)
    triu = (same & (tr <= tc)).astype(BF16)
    pre = if_ref[...] + brow_ref[...]
    bcum_cols = sum(_dot(tril, p) for p in _split3(_log_sigmoid(pre)))
    preT = jnp.transpose(if_ref[...])[i0:i0 + 8, :] + bcol_ref[...]
    row8 = lax.broadcasted_iota(jnp.int32, (8, 1), 0)
    bcum_rows = sum(_dot(p, triu) for p in _split3(jnp.where(row8 < NH, preT, _log_sigmoid(preT))))
    inst = [(c, h) for c in range(nc) for h in range(NH)]
    rs = lambda c: slice(c * C, (c + 1) * C)
    hs_ = lambda h: slice(h * d, (h + 1) * d)
    q_i = [q_ref[rs(c), hs_(h)].astype(BF16) for c, h in inst]
    k_i = [k_ref[rs(c), hs_(h)] for c, h in inst]
    ve_i = [jnp.concatenate([v_ref[rs(c), hs_(h)].astype(BF16), ones], axis=1) for c, h in inst]
    o_i = [o_ref[rs(c), hs_(h)] for c, h in inst]
    yield

    bc_i = [bcum_cols[rs(c), f0 + h:f0 + h + 1] for c, h in inst]
    e_c = [pre[rs(c), i0 + h:i0 + h + 1] - bc for (c, h), bc in zip(inst, bc_i)]
    e_r = [preT[h:h + 1, rs(c)] - bcum_rows[NH + h:NH + h + 1, rs(c)] for c, h in inst]
    a_col = [jnp.max(jnp.where(causal, e, -jnp.inf), axis=-1, keepdims=True) for e in e_r]
    yield
    D0 = [jnp.where(causal, jnp.exp(e - a), 0.0) for e, a in zip(e_r, a_col)]
    QK = [_bdot_nt(q, k) for q, k in zip(q_i, k_i)]
    yield
    S0 = [(qk * dm).astype(BF16) for qk, dm in zip(QK, D0)]
    a_end = [a[C - 1:C, :] for a in a_col]
    KV0 = [_bdot_tn(jnp.exp(e - ae) * k, ve) for e, ae, k, ve in zip(e_c, a_end, k_i, ve_i)]
    yield
    intra = [_dot(s0, ve) for s0, ve in zip(S0, ve_i)]
    yield

    cn = [cn_ref[h] for h in range(NH)]
    m_prev = [m_ref[h][0:1, 0:1] for h in range(NH)]
    hh = []
    for c in range(nc):
        ii = [c * NH + h for h in range(NH)]
        QC = [_bdot(q_i[i], cn[h]) for h, i in enumerate(ii)]
        yield
        for h, i in enumerate(ii):
            mu_t = jnp.maximum(m_prev[h], a_col[i])
            tot = jnp.exp(m_prev[h] - mu_t) * QC[h] + jnp.exp(a_col[i] - mu_t) * intra[i]
            hh.append(tot[:, :d] / jnp.maximum(jnp.abs(tot[:, d:]), jnp.exp(-(bc_i[i] + mu_t))))
            mu_end = jnp.maximum(m_prev[h], a_end[i])
            cn[h] = jnp.exp(m_prev[h] - mu_end) * cn[h] + jnp.exp(a_end[i] - mu_end) * KV0[i]
            m_prev[h] = bc_i[i][C - 1:C, :] + mu_end
    for h in range(NH):
        cn_ref[h] = cn[h]
        m_ref[h] = jnp.broadcast_to(m_prev[h], (8, 128))
    mean = [jnp.mean(x, axis=-1, keepdims=True) for x in hh]
    yield
    dd = [x - m for x, m in zip(hh, mean)]
    var = [jnp.mean(x * x, axis=-1, keepdims=True) for x in dd]
    yield
    for (c, h), x, v, o in zip(inst, dd, var, o_i):
        y_ref[rs(c), hs_(h)] = o * (x * lax.rsqrt(v + MLSTM_NORM_EPS) * nw_ref[:, hs_(h)])


def _run_interleaved(streams, head, every):
    for k in head:
        next(streams[k], None)
    live = dict(enumerate(streams))
    rnd = 0
    while live:
        for k in [k for k in live if rnd % every[k] == 0]:
            if next(live[k], StopIteration) is StopIteration:
                del live[k]
        rnd += 1


def _mixer_kernel(*refs, has_vres, nt, n_tiles, wnames, snames):
    refs = list(refs)
    h_ref = refs.pop(0)
    vfirst_ref = refs.pop(0) if has_vres else None
    w = {n: refs.pop(0) for n in wnames}
    ya_o, yb_o, sg_o = refs.pop(0), refs.pop(0), refs.pop(0)
    vfirst_o = None if has_vres else refs.pop(0)
    zr, zs, zm, xb_ref = (refs.pop(0) for _ in range(4))
    s = {n: refs.pop(0) for n in snames}
    h_state, cn_state, m_state = refs
    tt = h_ref.shape[0]
    i = pl.program_id(0)

    @pl.when(i == 0)
    def _():
        for ref in (zr, zs, zm, h_state, cn_state, m_state) + tuple(s.values()):
            ref[...] = jnp.zeros_like(ref)

    @pl.when(lax.rem(i + nt - 1, nt) == 0)
    def _():
        for ref in (zr, zs, zm):
            ref[:8, :] = jnp.zeros((8, ref.shape[1]), F32)

    @pl.when(lax.rem(i + 2 * nt - 2, nt) == 0)
    def _():
        h_state[...] = jnp.zeros_like(h_state)
        cn_state[...] = jnp.zeros_like(cn_state)
        m_state[...] = jnp.zeros_like(m_state)

    xb_ref[...] = _rms(h_ref[...], w["mix_norm"][...]).astype(BF16)
    streams = [_rwkv_stages(s, w, ya_o, h_state), _mlstm_stages(s, w, yb_o, cn_state, m_state),
               _prep_stages(zr, zs, zm, vfirst_ref, vfirst_o, i <= n_tiles, w, s, xb_ref, sg_o, tt)]
    _run_interleaved(streams, head=(0, 1), every=(1, 1, 1))


def _mixer_call(h3, vfirst, p):
    B, T, _ = h3.shape
    tt = min(TT_MIX, T)
    nt = T // tt
    n = B * nt
    has_vres = vfirst is not None
    cur = lambda wd: pl.BlockSpec((None, tt, wd), lambda i: (jnp.minimum(i, n - 1) // nt,
                                                            jnp.minimum(i, n - 1) % nt, 0))
    def lag(wd, by):
        tile = lambda i: jnp.clip(i - by, 0, n - 1)
        return pl.BlockSpec((None, tt, wd), lambda i: (tile(i) // nt, tile(i) % nt, 0))
    wnames = ("mix_norm", "w_rkv", "w_sm", "w_ml", "w_gate", "mu_rkv", "mu_sm", "w0", "wup", "a0", "aup",
              "gup", "k_k", "k_a", "r_k", "vbias", "vup", "convw", "convb", "seg", "gnw", "gnb",
              "if_brow", "if_bcol", "ml_nw")
    snames = ("r", "k", "v", "ld", "kk", "a", "g", "bonus", "q", "mk", "mv", "o", "ifb")
    weights = [p[nm] for nm in wnames]
    in_specs = [cur(D_MODEL)] + ([lag(D_RWKV, 1)] if has_vres else []) + [_const_spec(x.shape) for x in weights]
    out_specs = [lag(D_RWKV, 2), lag(D_MLSTM, 2), cur(2 * D_MODEL)] + ([] if has_vres else [lag(D_RWKV, 1)])
    out_w = [D_RWKV, D_MLSTM, 2 * D_MODEL] + ([] if has_vres else [D_RWKV])
    scratch = [pltpu.VMEM((tt + 8, 3 * D_RWKV), F32), pltpu.VMEM((tt + 8, SM_W), F32),
               pltpu.VMEM((tt + 8, 4 * D_MLSTM), F32), pltpu.VMEM((tt, D_MODEL), BF16)]
    scratch += [pltpu.VMEM((tt, 128 if nm == "ifb" else D_RWKV), F32) for nm in snames]
    scratch += [pltpu.VMEM((RWKV_HEADS, RWKV_HEAD, RWKV_HEAD), F32),
                pltpu.VMEM((MLSTM_HEADS, MLSTM_HEAD, 2 * MLSTM_HEAD), F32),
                pltpu.VMEM((MLSTM_HEADS, 8, 128), F32)]
    return pl.pallas_call(
        functools.partial(_mixer_kernel, has_vres=has_vres, nt=nt, n_tiles=n, wnames=wnames, snames=snames),
        grid=(n + 2,),
        in_specs=in_specs,
        out_specs=out_specs,
        out_shape=[jax.ShapeDtypeStruct((B, T, wd), F32) for wd in out_w],
        scratch_shapes=scratch,
        compiler_params=pltpu.CompilerParams(dimension_semantics=("arbitrary",),
                                             vmem_limit_bytes=VMEM_LIMIT),
        name="mixer",
    )(*([h3] + ([vfirst] if has_vres else []) + weights))


def _pad_rows(w, rows, at=0):
    out = jnp.zeros((rows, w.shape[1]), w.dtype)
    return out.at[at:at + w.shape[0]].set(w)


def _ffn_params(norm, w_in, w_out):
    nc = D_FF // FF_CHUNK
    wg = w_in[:, :D_FF].reshape(D_MODEL, nc, FF_CHUNK).transpose(1, 0, 2).astype(BF16)
    wu = w_in[:, D_FF:].reshape(D_MODEL, nc, FF_CHUNK).transpose(1, 0, 2).astype(BF16)
    wo = w_out.reshape(nc, FF_CHUNK, D_MODEL).astype(BF16)
    return norm.reshape(1, D_MODEL), wg, wu, wo


def _mixer_params(l, mix_norm, w_in, shift_mu, rw_w0, rw_w_up, rw_a0, rw_a_up, rw_g_up, rw_k_k, rw_k_a,
                  rw_r_k, vres_down, vres_up, vres_bias, ml_conv_w, ml_conv_b, ml_i_bias, ml_f_bias):
    w = w_in[l]
    o_ml = N_RW_IN
    o_gate = N_RW_IN + N_ML_IN
    w_rkv = w[:, :3 * D_RWKV]
    w_lora = w[:, 3 * D_RWKV:N_RW_IN]
    w_if = w[:, o_ml + 4 * D_MLSTM:o_gate]
    w_vd = vres_down[l - 1] if l > 0 else jnp.zeros((D_MODEL, LORA_V), F32)
    pad = jnp.zeros((D_MODEL, SM_W - SM_VRES - LORA_V), F32)
    w_sm = jnp.concatenate([w_lora, w_if, w_vd, pad], axis=1)
    mu = shift_mu[l]
    mu_sm = jnp.concatenate([mu[3 * D_RWKV:], jnp.zeros((SM_W - (N_RW_IN - 3 * D_RWKV),), F32)])
    row = lambda x: x.reshape(1, -1)
    p = {
        "mix_norm": row(mix_norm[l]),
        "w_rkv": w_rkv.astype(BF16),
        "w_sm": w_sm.astype(BF16),
        "w_ml": w[:, o_ml:o_ml + 4 * D_MLSTM].astype(BF16),
        "w_gate": w[:, o_gate:].astype(BF16),
        "mu_rkv": row(mu[:3 * D_RWKV]),
        "mu_sm": row(mu_sm),
        "w0": row(rw_w0[l]),
        "wup": _pad_rows(rw_w_up[l], 128, 0).astype(BF16),
        "a0": row(rw_a0[l]),
        "aup": _pad_rows(rw_a_up[l], 128, LORA_W).astype(BF16),
        "gup": _pad_rows(rw_g_up[l], 256, 0).astype(BF16),
        "k_k": row(rw_k_k[l]),
        "k_a": row(rw_k_a[l]),
        "r_k": row(rw_r_k[l]),
        "vbias": row(vres_bias[l - 1]) if l > 0 else jnp.zeros((1, D_RWKV), F32),
        "vup": (_pad_rows(vres_up[l - 1], 256, SM_VRES - 128) if l > 0
                else jnp.zeros((256, D_RWKV), F32)).astype(BF16),
        "convw": ml_conv_w[l],
        "convb": row(ml_conv_b[l]),
    }
    gate_bias = jnp.concatenate([ml_i_bias[l], ml_f_bias[l]])
    p["if_brow"] = jnp.zeros((1, 128), F32).at[0, IF_LANE:IF_LANE + 8].set(gate_bias)
    p["if_bcol"] = gate_bias.reshape(8, 1)
    return p


def kernel(x, ffn1_norm, ffn1_w_in, ffn1_w_out, mix_norm, w_in, shift_mu, rw_w0, rw_w_up, rw_a0, rw_a_up, rw_g_up, rw_k_k, rw_k_a, rw_r_k, rw_gn_w, rw_gn_b, vres_down, vres_up, vres_bias, ml_conv_w, ml_conv_b, ml_i_bias, ml_f_bias, ml_norm_w, br_a, br_b, w_out, ffn2_norm, ffn2_w_in, ffn2_w_out, final_norm):
    B, T, D = x.shape
    n = B * T
    assert D == D_MODEL and T % min(TT_MIX, T) == 0 and min(TT_MIX, T) % CHUNK == 0
    assert n % min(TM_FFN, n) == 0
    lane = jnp.arange(256) // RWKV_HEAD
    seg = (lane[:, None] == lane[None, :]).astype(BF16)
    fin = final_norm.reshape(1, D_MODEL)
    h = x.reshape(n, D)
    v_first = None
    for l in range(DEPTH):
        h = _ffn_call(h, *_ffn_params(ffn1_norm[l], ffn1_w_in[l], ffn1_w_out[l]))
        p = _mixer_params(l, mix_norm, w_in, shift_mu, rw_w0, rw_w_up, rw_a0, rw_a_up, rw_g_up, rw_k_k,
                          rw_k_a, rw_r_k, vres_down, vres_up, vres_bias, ml_conv_w, ml_conv_b,
                          ml_i_bias, ml_f_bias)
        p.update(seg=seg, gnw=rw_gn_w[l].reshape(1, -1), gnb=rw_gn_b[l].reshape(1, -1),
                 ml_nw=ml_norm_w[l].reshape(1, -1))
        outs = _mixer_call(h.reshape(B, T, D), v_first, p)
        ya, yb, sg = outs[:3]
        if l == 0:
            v_first = outs[3]
        h = _merge_ffn_call(h, ya.reshape(n, -1), yb.reshape(n, -1), sg.reshape(n, -1),
                            br_a[l].astype(BF16), br_b[l].astype(BF16), w_out[l].astype(BF16),
                            *_ffn_params(ffn2_norm[l], ffn2_w_in[l], ffn2_w_out[l]), fin,
                            final=(l == DEPTH - 1))
    return h.reshape(B, T, D)
```

```python
import functools

import jax
import jax.numpy as jnp
from jax import lax
from jax.experimental import pallas as pl
from jax.experimental.pallas import tpu as pltpu

F32 = jnp.float32
BF16 = jnp.bfloat16

D_MODEL = 1024
DEPTH = 4
CHUNK = 64
D_RWKV = 512
RWKV_HEAD = 64
RWKV_HEADS = 8
LORA_W, LORA_A, LORA_V, LORA_G = 64, 64, 32, 160
RWKV_GN_EPS = 64e-5
D_MLSTM = 512
MLSTM_HEADS = 4
MLSTM_HEAD = 128
CONV_W = 4
MLSTM_NORM_EPS = 1e-5
D_FF = 2816
NORM_EPS = 1e-6
N_RW_IN = 3 * D_RWKV + LORA_W + LORA_A + LORA_G
N_ML_IN = 4 * D_MLSTM + 2 * MLSTM_HEADS

SM_W = 384
SM_IF = 288
SM_VRES = 296
IF_LANE = SM_IF - 256

FF_CHUNK = 256
TM_FFN = 1024
TM_MERGE = 512
TT_MIX = 256
VMEM_LIMIT = 56 * 1024 * 1024


def _const_spec(shape):
    nd = len(shape)
    return pl.BlockSpec(shape, lambda *_: (0,) * nd, pipeline_mode=pl.Buffered(1))


def _rms(h, g):
    return h * lax.rsqrt(jnp.mean(h * h, axis=-1, keepdims=True) + NORM_EPS) * g


def _sigmoid(x):
    return 0.5 * jnp.tanh(0.5 * x) + 0.5


def _dot(a, b):
    return jnp.dot(a, b, preferred_element_type=F32)


def _seg_sum(x, seg_ref):
    xb = x.astype(BF16)
    gw = seg_ref.shape[0]
    return jnp.concatenate([_dot(xb[:, lo:lo + gw], seg_ref[...]) for lo in range(0, x.shape[1], gw)], axis=1)


def _ffn_body(h, g_ref, wg_ref, wu_ref, wo_ref, acc_ref):
    xb = _rms(h, g_ref[...]).astype(BF16)
    nc = D_FF // FF_CHUNK
    for c in range(nc):
        gate = _dot(xb, wg_ref[c])
        up = _dot(xb, wu_ref[c])
        act = (gate * _sigmoid(gate) * up).astype(BF16)
        part = _dot(act, wo_ref[c])
        if c == 0:
            acc_ref[...] = part
        elif c < nc - 1:
            acc_ref[...] += part
    return h + 0.5 * (acc_ref[...] + part)


def _ffn_kernel(h_ref, g_ref, wg_ref, wu_ref, wo_ref, out_ref, acc_ref):
    out_ref[...] = _ffn_body(h_ref[...], g_ref, wg_ref, wu_ref, wo_ref, acc_ref)


def _merge_ffn_kernel(h_ref, ya_ref, yb_ref, sg_ref, bra_ref, brb_ref, wout_ref,
                      g_ref, wg_ref, wu_ref, wo_ref, fin_ref, out_ref, acc_ref, *, final):
    sg = sg_ref[...]
    pa = _dot(ya_ref[...].astype(BF16), bra_ref[...])
    pb = _dot(yb_ref[...].astype(BF16), brb_ref[...])
    u = sg[:, :D_MODEL] * pa + sg[:, D_MODEL:] * pb
    h = h_ref[...] + _dot(u.astype(BF16), wout_ref[...])
    h = _ffn_body(h, g_ref, wg_ref, wu_ref, wo_ref, acc_ref)
    if final:
        h = _rms(h, fin_ref[...])
    out_ref[...] = h


def _ffn_weight_specs():
    nc = D_FF // FF_CHUNK
    return [_const_spec((1, D_MODEL)), _const_spec((nc, D_MODEL, FF_CHUNK)),
            _const_spec((nc, D_MODEL, FF_CHUNK)), _const_spec((nc, FF_CHUNK, D_MODEL))]


def _ffn_call(h, g, wg, wu, wo):
    n = h.shape[0]
    tm = min(TM_FFN, n)
    row = pl.BlockSpec((tm, D_MODEL), lambda i: (i, 0))
    return pl.pallas_call(
        _ffn_kernel,
        grid=(n // tm,),
        in_specs=[row] + _ffn_weight_specs(),
        out_specs=row,
        out_shape=jax.ShapeDtypeStruct((n, D_MODEL), F32),
        scratch_shapes=[pltpu.VMEM((tm, D_MODEL), F32)],
        compiler_params=pltpu.CompilerParams(dimension_semantics=("arbitrary",),
                                             vmem_limit_bytes=VMEM_LIMIT),
        name="ffn",
    )(h, g, wg, wu, wo)


def _merge_ffn_call(h, ya, yb, sg, bra, brb, wout, g, wg, wu, wo, fin, final):
    n = h.shape[0]
    tm = min(TM_MERGE, n)
    row = lambda w: pl.BlockSpec((tm, w), lambda i: (i, 0))
    return pl.pallas_call(
        functools.partial(_merge_ffn_kernel, final=final),
        grid=(n // tm,),
        in_specs=[row(D_MODEL), row(D_RWKV), row(D_MLSTM), row(2 * D_MODEL),
                  _const_spec((D_RWKV, D_MODEL)), _const_spec((D_MLSTM, D_MODEL)),
                  _const_spec((D_MODEL, D_MODEL))] + _ffn_weight_specs() + [_const_spec((1, D_MODEL))],
        out_specs=row(D_MODEL),
        out_shape=jax.ShapeDtypeStruct((n, D_MODEL), F32),
        scratch_shapes=[pltpu.VMEM((tm, D_MODEL), F32)],
        compiler_params=pltpu.CompilerParams(dimension_semantics=("arbitrary",),
                                             vmem_limit_bytes=VMEM_LIMIT),
        name="merge_ffn",
    )(h, ya, yb, sg, bra, brb, wout, g, wg, wu, wo, fin)


def _bdot(a, b):
    return jnp.dot(a.astype(BF16), b.astype(BF16), preferred_element_type=F32)


def _bdot_nt(a, b):
    return lax.dot_general(a.astype(BF16), b.astype(BF16), (((1,), (1,)), ((), ())),
                           preferred_element_type=F32)


def _bdot_tn(a, b):
    return lax.dot_general(a.astype(BF16), b.astype(BF16), (((0,), (0,)), ((), ())),
                           preferred_element_type=F32)


def _split3(x):
    hi = x.astype(BF16)
    r1 = x - hi.astype(F32)
    mid = r1.astype(BF16)
    lo = (r1 - mid.astype(F32)).astype(BF16)
    return hi, mid, lo


def _log_sigmoid(x):
    return jnp.minimum(x, 0.0) - jnp.log(1.0 + jnp.exp(-jnp.abs(x)))


def _prep_stages(zr, zs, zm, vfirst_ref, vfirst_o, keep_vfirst, w, s, xb_ref, sg_o, tt):
    D = D_RWKV

    def shifted(buf, lo, hi, mu):
        z = buf[8:, lo:hi]
        return z + mu * (buf[7:7 + tt, lo:hi] - z)

    def refill(buf, w_ref, lo, hi):
        buf[:8, lo:hi] = buf[tt:, lo:hi]
        buf[8:, lo:hi] = _dot(xb_ref[...], w_ref[:, lo:hi])

    z_sm = zs[8:, :]
    m_sm = shifted(zs, 0, SM_W, w["mu_sm"][...])
    s["ifb"][...] = z_sm[:, 256:]
    xwa = m_sm[:, :128]
    y = -(w["w0"][...] + _dot(jnp.tanh(xwa).astype(BF16), w["wup"][...]))
    softplus = jnp.maximum(y, 0.0) + jnp.log(1.0 + jnp.exp(-jnp.abs(y)))
    s["ld"][...] = -jnp.exp(-softplus - 0.5)
    a = _sigmoid(w["a0"][...] + _dot(xwa.astype(BF16), w["aup"][...]))
    s["a"][...] = a
    s["g"][...] = _dot(_sigmoid(m_sm[:, 128:]).astype(BF16), w["gup"][...])
    if vfirst_ref is not None:
        vg = _sigmoid(w["vbias"][...] + _dot(z_sm[:, 128:].astype(BF16), w["vup"][...]))
    refill(zs, w["w_sm"], 0, SM_W)
    yield

    mu = w["mu_rkv"]
    k = shifted(zr, D, 2 * D, mu[:, D:2 * D])
    refill(zr, w["w_rkv"], D, 2 * D)
    kk = k * w["k_k"][...]
    s["kk"][...] = kk * lax.rsqrt(jnp.maximum(_seg_sum(kk * kk, w["seg"]), 1e-24))
    k2 = k * (1.0 + (a - 1.0) * w["k_a"][...])
    s["k"][...] = k2
    yield

    v = shifted(zr, 2 * D, 3 * D, mu[:, 2 * D:])
    refill(zr, w["w_rkv"], 2 * D, 3 * D)
    if vfirst_ref is not None:
        v = v + (vfirst_ref[...] - v) * vg
    s["v"][...] = v
    if vfirst_o is not None:
        @pl.when(keep_vfirst)
        def _():
            vfirst_o[...] = v
    yield

    r = shifted(zr, 0, D, mu[:, :D])
    refill(zr, w["w_rkv"], 0, D)
    s["r"][...] = r
    s["bonus"][...] = _seg_sum(r * k2 * w["r_k"][...], w["seg"]) * v
    yield

    cw = w["convw"]
    for name, lo, scale in (("q", 0, MLSTM_HEAD ** -0.5), ("mk", D_MLSTM, 1.0)):
        hi = lo + D_MLSTM
        conv = w["convb"][:, lo:hi] + cw[CONV_W - 1:CONV_W, lo:hi] * zm[8:, lo:hi]
        for sft in range(1, CONV_W):
            conv = conv + cw[CONV_W - 1 - sft:CONV_W - sft, lo:hi] * zm[8 - sft:8 - sft + tt, lo:hi]
        refill(zm, w["w_ml"], lo, hi)
        s[name][...] = conv * _sigmoid(conv) * scale
        yield
    s["mv"][...] = zm[8:, 2 * D_MLSTM:3 * D_MLSTM]
    refill(zm, w["w_ml"], 2 * D_MLSTM, 3 * D_MLSTM)
    yield
    s["o"][...] = _sigmoid(zm[8:, 3 * D_MLSTM:])
    refill(zm, w["w_ml"], 3 * D_MLSTM, 4 * D_MLSTM)
    yield

    for lo in range(0, 2 * D_MODEL, D_MLSTM):
        sg_o[:, lo:lo + D_MLSTM] = _sigmoid(_dot(xb_ref[...], w["w_gate"][:, lo:lo + D_MLSTM]))
        yield


def _rwkv_stages(s, w, y_ref, h_ref):
    C, N = CHUNK, RWKV_HEAD
    tt = y_ref.shape[0]
    ti = lax.broadcasted_iota(jnp.int32, (C, C), 0)
    si = lax.broadcasted_iota(jnp.int32, (C, C), 1)
    strict = si < ti
    incl = si <= ti
    eye = (si == ti).astype(F32)

    tr = lax.broadcasted_iota(jnp.int32, (tt, tt), 0)
    tc = lax.broadcasted_iota(jnp.int32, (tt, tt), 1)
    same = (tr // C) == (tc // C)
    tri = (same & (tc <= tr)).astype(BF16)
    lw = s["ld"][...]
    kk = s["kk"][...]
    kka = kk * s["a"][...]
    k2 = s["k"][...]
    r_in = s["r"][...]
    V = s["v"][...].astype(BF16)
    bonus = s["bonus"][...]
    g_out = s["g"][...]
    parts = _split3(lw)
    cs = sum(_dot(tri, p) for p in parts)
    nc = tt // C
    cs_tot = jnp.concatenate([jnp.broadcast_to(cs[(c + 1) * C - 1:(c + 1) * C, :], (C, D_RWKV))
                              for c in range(nc)], axis=0)
    yield

    g_inv = jnp.exp(-cs)
    g_end = jnp.exp(cs_tot - cs)
    g_tot = jnp.exp(cs_tot)
    Rt = (r_in * jnp.exp(cs)).astype(BF16)
    At = (-kk * jnp.exp(cs - lw)).astype(BF16)
    Kb = (k2 * g_inv).astype(BF16)
    Bb = (kka * g_inv).astype(BF16)
    Ke = (k2 * g_end).astype(BF16)
    Be = (kka * g_end).astype(BF16)

    nc = tt // C
    inst = [(c, h) for c in range(nc) for h in range(RWKV_HEADS)]
    sl = lambda x, c, h: x[c * C:(c + 1) * C, h * N:(h + 1) * N]
    each = lambda f, *lists: [f(*xs) for xs in zip(*lists)]

    At_i = [sl(At, c, h) for c, h in inst]
    Rt_i = [sl(Rt, c, h) for c, h in inst]
    V_i = [sl(V, c, h) for c, h in inst]
    P = [_bdot_nt(jnp.concatenate([a, r], axis=0),
                  jnp.concatenate([sl(Kb, c, h), sl(Bb, c, h)], axis=0))
         for a, r, (c, h) in zip(At_i, Rt_i, inst)]
    yield
    A_ak = [jnp.where(strict, p[:C, :C], 0.0) for p in P]
    L = [jnp.where(strict, p[:C, C:], 0.0) for p in P]
    A_rk = [jnp.where(incl, p[C:, :C], 0.0) for p in P]
    A_rb = [jnp.where(incl, p[C:, C:], 0.0).astype(BF16) for p in P]
    AV = each(lambda ak, rk, v: _bdot(jnp.concatenate([ak, rk], axis=0), v), A_ak, A_rk, V_i)
    yield
    X = [eye + l for l in L]
    M = each(lambda l: _bdot(l, l), L)
    yield
    for _ in range(4):
        XM = each(lambda x, m: _bdot(jnp.concatenate([x, m], axis=0), m), X, M)
        yield
        X = each(lambda x, xm: x + xm[:C], X, XM)
        M = [xm[C:] for xm in XM]
    Tm = each(lambda x, m: x + _bdot(x, m), X, M)
    yield
    W = each(lambda t, a, av: _bdot(t, jnp.concatenate([a, av[:C].astype(BF16)], axis=1)),
             Tm, At_i, AV)
    yield
    Z = each(_bdot, A_rb, W)
    yield
    G = [_bdot_tn(sl(Be, c, h), w_) for (c, h), w_ in zip(inst, W)]
    KV = [_bdot_tn(sl(Ke, c, h), v) for (c, h), v in zip(inst, V_i)]
    yield
    RM = [jnp.concatenate([r.astype(F32) + z[:, :N],
                           eye * g_tot[c * C:c * C + 1, h * N:(h + 1) * N] + g[:, :N]], axis=0).astype(BF16)
          for r, z, g, (c, h) in zip(Rt_i, Z, G, inst)]
    Y0 = each(lambda av, z: av[C:] + z[:, N:], AV, Z)
    G0 = each(lambda kv, g: kv + g[:, N:], KV, G)

    Hs = [h_ref[h] for h in range(RWKV_HEADS)]
    y_rows = []
    for c in range(nc):
        RMH = [_bdot(RM[c * RWKV_HEADS + h], Hs[h]) for h in range(RWKV_HEADS)]
        yield
        y_rows.append(jnp.concatenate([RMH[h][:C] + Y0[c * RWKV_HEADS + h]
                                       for h in range(RWKV_HEADS)], axis=1))
        Hs = [RMH[h][C:] + G0[c * RWKV_HEADS + h] for h in range(RWKV_HEADS)]
    for h in range(RWKV_HEADS):
        h_ref[h] = Hs[h]
    y = jnp.concatenate(y_rows, axis=0)
    inv_n = 1.0 / N
    mu = _seg_sum(y, w["seg"]) * inv_n
    yield
    d = y - mu
    var = _seg_sum(d * d, w["seg"]) * inv_n
    yield
    yn = d * lax.rsqrt(var + RWKV_GN_EPS)
    y_ref[...] = (yn * w["gnw"][...] + w["gnb"][...] + bonus) * g_out


def _mlstm_stages(s, w, y_ref, cn_ref, m_ref):
    q_ref, k_ref, v_ref, o_ref, if_ref = s["q"], s["mk"], s["mv"], s["o"], s["ifb"]
    brow_ref, bcol_ref, nw_ref = w["if_brow"], w["if_bcol"], w["ml_nw"]
    C, d = CHUNK, MLSTM_HEAD
    tt = q_ref.shape[0]
    nc = tt // C
    NH = MLSTM_HEADS
    ti = lax.broadcasted_iota(jnp.int32, (C, C), 0)
    si = lax.broadcasted_iota(jnp.int32, (C, C), 1)
    causal = si <= ti
    ones = jnp.ones((C, d), BF16)
    i0 = IF_LANE
    f0 = IF_LANE + NH

    tr = lax.broadcasted_iota(jnp.int32, (tt, tt), 0)
    tc = lax.broadcasted_iota(jnp.int32, (tt, tt), 1)
    same = (tr // C) == (tc // C)
    tril = (same & (tc <= tr)).astype(BF16)
    triu = (same & (tr <= tc)).astype(BF16)
    pre = if_ref[...] + brow_ref[...]
    bcum_cols = sum(_dot(tril, p) for p in _split3(_log_sigmoid(pre)))
    preT = jnp.transpose(if_ref[...])[i0:i0 + 8, :] + bcol_ref[...]
    row8 = lax.broadcasted_iota(jnp.int32, (8, 1), 0)
    bcum_rows = sum(_dot(p, triu) for p in _split3(jnp.where(row8 < NH, preT, _log_sigmoid(preT))))
    inst = [(c, h) for c in range(nc) for h in range(NH)]
    rs = lambda c: slice(c * C, (c + 1) * C)
    hs_ = lambda h: slice(h * d, (h + 1) * d)
    q_i = [q_ref[rs(c), hs_(h)].astype(BF16) for c, h in inst]
    k_i = [k_ref[rs(c), hs_(h)] for c, h in inst]
    ve_i = [jnp.concatenate([v_ref[rs(c), hs_(h)].astype(BF16), ones], axis=1) for c, h in inst]
    o_i = [o_ref[rs(c), hs_(h)] for c, h in inst]
    yield

    bc_i = [bcum_cols[rs(c), f0 + h:f0 + h + 1] for c, h in inst]
    e_c = [pre[rs(c), i0 + h:i0 + h + 1] - bc for (c, h), bc in zip(inst, bc_i)]
    e_r = [preT[h:h + 1, rs(c)] - bcum_rows[NH + h:NH + h + 1, rs(c)] for c, h in inst]
    a_col = [jnp.max(jnp.where(causal, e, -jnp.inf), axis=-1, keepdims=True) for e in e_r]
    yield
    D0 = [jnp.where(causal, jnp.exp(e - a), 0.0) for e, a in zip(e_r, a_col)]
    QK = [_bdot_nt(q, k) for q, k in zip(q_i, k_i)]
    yield
    S0 = [(qk * dm).astype(BF16) for qk, dm in zip(QK, D0)]
    a_end = [a[C - 1:C, :] for a in a_col]
    KV0 = [_bdot_tn(jnp.exp(e - ae) * k, ve) for e, ae, k, ve in zip(e_c, a_end, k_i, ve_i)]
    yield
    intra = [_dot(s0, ve) for s0, ve in zip(S0, ve_i)]
    yield

    cn = [cn_ref[h] for h in range(NH)]
    m_prev = [m_ref[h][0:1, 0:1] for h in range(NH)]
    hh = []
    for c in range(nc):
        ii = [c * NH + h for h in range(NH)]
        QC = [_bdot(q_i[i], cn[h]) for h, i in enumerate(ii)]
        yield
        for h, i in enumerate(ii):
            mu_t = jnp.maximum(m_prev[h], a_col[i])
            tot = jnp.exp(m_prev[h] - mu_t) * QC[h] + jnp.exp(a_col[i] - mu_t) * intra[i]
            hh.append(tot[:, :d] / jnp.maximum(jnp.abs(tot[:, d:]), jnp.exp(-(bc_i[i] + mu_t))))
            mu_end = jnp.maximum(m_prev[h], a_end[i])
            cn[h] = jnp.exp(m_prev[h] - mu_end) * cn[h] + jnp.exp(a_end[i] - mu_end) * KV0[i]
            m_prev[h] = bc_i[i][C - 1:C, :] + mu_end
    for h in range(NH):
        cn_ref[h] = cn[h]
        m_ref[h] = jnp.broadcast_to(m_prev[h], (8, 128))
    mean = [jnp.mean(x, axis=-1, keepdims=True) for x in hh]
    yield
    dd = [x - m for x, m in zip(hh, mean)]
    var = [jnp.mean(x * x, axis=-1, keepdims=True) for x in dd]
    yield
    for (c, h), x, v, o in zip(inst, dd, var, o_i):
        y_ref[rs(c), hs_(h)] = o * (x * lax.rsqrt(v + MLSTM_NORM_EPS) * nw_ref[:, hs_(h)])


def _run_interleaved(streams, head, every):
    for k in head:
        next(streams[k], None)
    live = dict(enumerate(streams))
    rnd = 0
    while live:
        for k in [k for k in live if rnd % every[k] == 0]:
            if next(live[k], StopIteration) is StopIteration:
                del live[k]
        rnd += 1


def _mixer_kernel(*refs, has_vres, nt, n_tiles, wnames, snames):
    refs = list(refs)
    h_ref = refs.pop(0)
    vfirst_ref = refs.pop(0) if has_vres else None
    w = {n: refs.pop(0) for n in wnames}
    ya_o, yb_o, sg_o = refs.pop(0), refs.pop(0), refs.pop(0)
    vfirst_o = None if has_vres else refs.pop(0)
    zr, zs, zm, xb_ref = (refs.pop(0) for _ in range(4))
    s = {n: refs.pop(0) for n in snames}
    h_state, cn_state, m_state = refs
    tt = h_ref.shape[0]
    i = pl.program_id(0)

    @pl.when(i == 0)
    def _():
        for ref in (zr, zs, zm, h_state, cn_state, m_state) + tuple(s.values()):
            ref[...] = jnp.zeros_like(ref)

    @pl.when(lax.rem(i + nt - 1, nt) == 0)
    def _():
        for ref in (zr, zs, zm):
            ref[:8, :] = jnp.zeros((8, ref.shape[1]), F32)

    @pl.when(lax.rem(i + 2 * nt - 2, nt) == 0)
    def _():
        h_state[...] = jnp.zeros_like(h_state)
        cn_state[...] = jnp.zeros_like(cn_state)
        m_state[...] = jnp.zeros_like(m_state)

    xb_ref[...] = _rms(h_ref[...], w["mix_norm"][...]).astype(BF16)
    streams = [_rwkv_stages(s, w, ya_o, h_state), _mlstm_stages(s, w, yb_o, cn_state, m_state),
               _prep_stages(zr, zs, zm, vfirst_ref, vfirst_o, i <= n_tiles, w, s, xb_ref, sg_o, tt)]
    _run_interleaved(streams, head=(0, 1), every=(1, 1, 1))


def _mixer_call(h3, vfirst, p):
    B, T, _ = h3.shape
    tt = min(TT_MIX, T)
    nt = T // tt
    n = B * nt
    has_vres = vfirst is not None
    cur = lambda wd: pl.BlockSpec((None, tt, wd), lambda i: (jnp.minimum(i, n - 1) // nt,
                                                            jnp.minimum(i, n - 1) % nt, 0))
    def lag(wd, by):
        tile = lambda i: jnp.clip(i - by, 0, n - 1)
        return pl.BlockSpec((None, tt, wd), lambda i: (tile(i) // nt, tile(i) % nt, 0))
    wnames = ("mix_norm", "w_rkv", "w_sm", "w_ml", "w_gate", "mu_rkv", "mu_sm", "w0", "wup", "a0", "aup",
              "gup", "k_k", "k_a", "r_k", "vbias", "vup", "convw", "convb", "seg", "gnw", "gnb",
              "if_brow", "if_bcol", "ml_nw")
    snames = ("r", "k", "v", "ld", "kk", "a", "g", "bonus", "q", "mk", "mv", "o", "ifb")
    weights = [p[nm] for nm in wnames]
    in_specs = [cur(D_MODEL)] + ([lag(D_RWKV, 1)] if has_vres else []) + [_const_spec(x.shape) for x in weights]
    out_specs = [lag(D_RWKV, 2), lag(D_MLSTM, 2), cur(2 * D_MODEL)] + ([] if has_vres else [lag(D_RWKV, 1)])
    out_w = [D_RWKV, D_MLSTM, 2 * D_MODEL] + ([] if has_vres else [D_RWKV])
    scratch = [pltpu.VMEM((tt + 8, 3 * D_RWKV), F32), pltpu.VMEM((tt + 8, SM_W), F32),
               pltpu.VMEM((tt + 8, 4 * D_MLSTM), F32), pltpu.VMEM((tt, D_MODEL), BF16)]
    scratch += [pltpu.VMEM((tt, 128 if nm == "ifb" else D_RWKV), F32) for nm in snames]
    scratch += [pltpu.VMEM((RWKV_HEADS, RWKV_HEAD, RWKV_HEAD), F32),
                pltpu.VMEM((MLSTM_HEADS, MLSTM_HEAD, 2 * MLSTM_HEAD), F32),
                pltpu.VMEM((MLSTM_HEADS, 8, 128), F32)]
    return pl.pallas_call(
        functools.partial(_mixer_kernel, has_vres=has_vres, nt=nt, n_tiles=n, wnames=wnames, snames=snames),
        grid=(n + 2,),
        in_specs=in_specs,
        out_specs=out_specs,
        out_shape=[jax.ShapeDtypeStruct((B, T, wd), F32) for wd in out_w],
        scratch_shapes=scratch,
        compiler_params=pltpu.CompilerParams(dimension_semantics=("arbitrary",),
                                             vmem_limit_bytes=VMEM_LIMIT),
        name="mixer",
    )(*([h3] + ([vfirst] if has_vres else []) + weights))


def _pad_rows(w, rows, at=0):
    out = jnp.zeros((rows, w.shape[1]), w.dtype)
    return out.at[at:at + w.shape[0]].set(w)


def _ffn_params(norm, w_in, w_out):
    nc = D_FF // FF_CHUNK
    wg = w_in[:, :D_FF].reshape(D_MODEL, nc, FF_CHUNK).transpose(1, 0, 2).astype(BF16)
    wu = w_in[:, D_FF:].reshape(D_MODEL, nc, FF_CHUNK).transpose(1, 0, 2).astype(BF16)
    wo = w_out.reshape(nc, FF_CHUNK, D_MODEL).astype(BF16)
    return norm.reshape(1, D_MODEL), wg, wu, wo


def _mixer_params(l, mix_norm, w_in, shift_mu, rw_w0, rw_w_up, rw_a0, rw_a_up, rw_g_up, rw_k_k, rw_k_a,
                  rw_r_k, vres_down, vres_up, vres_bias, ml_conv_w, ml_conv_b, ml_i_bias, ml_f_bias):
    w = w_in[l]
    o_ml = N_RW_IN
    o_gate = N_RW_IN + N_ML_IN
    w_rkv = w[:, :3 * D_RWKV]
    w_lora = w[:, 3 * D_RWKV:N_RW_IN]
    w_if = w[:, o_ml + 4 * D_MLSTM:o_gate]
    w_vd = vres_down[l - 1] if l > 0 else jnp.zeros((D_MODEL, LORA_V), F32)
    pad = jnp.zeros((D_MODEL, SM_W - SM_VRES - LORA_V), F32)
    w_sm = jnp.concatenate([w_lora, w_if, w_vd, pad], axis=1)
    mu = shift_mu[l]
    mu_sm = jnp.concatenate([mu[3 * D_RWKV:], jnp.zeros((SM_W - (N_RW_IN - 3 * D_RWKV),), F32)])
    row = lambda x: x.reshape(1, -1)
    p = {
        "mix_norm": row(mix_norm[l]),
        "w_rkv": w_rkv.astype(BF16),
        "w_sm": w_sm.astype(BF16),
        "w_ml": w[:, o_ml:o_ml + 4 * D_MLSTM].astype(BF16),
        "w_gate": w[:, o_gate:].astype(BF16),
        "mu_rkv": row(mu[:3 * D_RWKV]),
        "mu_sm": row(mu_sm),
        "w0": row(rw_w0[l]),
        "wup": _pad_rows(rw_w_up[l], 128, 0).astype(BF16),
        "a0": row(rw_a0[l]),
        "aup": _pad_rows(rw_a_up[l], 128, LORA_W).astype(BF16),
        "gup": _pad_rows(rw_g_up[l], 256, 0).astype(BF16),
        "k_k": row(rw_k_k[l]),
        "k_a": row(rw_k_a[l]),
        "r_k": row(rw_r_k[l]),
        "vbias": row(vres_bias[l - 1]) if l > 0 else jnp.zeros((1, D_RWKV), F32),
        "vup": (_pad_rows(vres_up[l - 1], 256, SM_VRES - 128) if l > 0
                else jnp.zeros((256, D_RWKV), F32)).astype(BF16),
        "convw": ml_conv_w[l],
        "convb": row(ml_conv_b[l]),
    }
    gate_bias = jnp.concatenate([ml_i_bias[l], ml_f_bias[l]])
    p["if_brow"] = jnp.zeros((1, 128), F32).at[0, IF_LANE:IF_LANE + 8].set(gate_bias)
    p["if_bcol"] = gate_bias.reshape(8, 1)
    return p


def kernel(x, ffn1_norm, ffn1_w_in, ffn1_w_out, mix_norm, w_in, shift_mu, rw_w0, rw_w_up, rw_a0, rw_a_up, rw_g_up, rw_k_k, rw_k_a, rw_r_k, rw_gn_w, rw_gn_b, vres_down, vres_up, vres_bias, ml_conv_w, ml_conv_b, ml_i_bias, ml_f_bias, ml_norm_w, br_a, br_b, w_out, ffn2_norm, ffn2_w_in, ffn2_w_out, final_norm):
    B, T, D = x.shape
    n = B * T
    assert D == D_MODEL and T % min(TT_MIX, T) == 0 and min(TT_MIX, T) % CHUNK == 0
    assert n % min(TM_FFN, n) == 0 and n % min(TM_MERGE, n) == 0
    lane = jnp.arange(256) // RWKV_HEAD
    seg = (lane[:, None] == lane[None, :]).astype(BF16)
    fin = final_norm.reshape(1, D_MODEL)
    h = x.reshape(n, D)
    v_first = None
    for l in range(DEPTH):
        h = _ffn_call(h, *_ffn_params(ffn1_norm[l], ffn1_w_in[l], ffn1_w_out[l]))
        p = _mixer_params(l, mix_norm, w_in, shift_mu, rw_w0, rw_w_up, rw_a0, rw_a_up, rw_g_up, rw_k_k,
                          rw_k_a, rw_r_k, vres_down, vres_up, vres_bias, ml_conv_w, ml_conv_b,
                          ml_i_bias, ml_f_bias)
        p.update(seg=seg, gnw=rw_gn_w[l].reshape(1, -1), gnb=rw_gn_b[l].reshape(1, -1),
                 ml_nw=ml_norm_w[l].reshape(1, -1))
        outs = _mixer_call(h.reshape(B, T, D), v_first, p)
        ya, yb, sg = outs[:3]
        if l == 0:
            v_first = outs[3]
        h = _merge_ffn_call(h, ya.reshape(n, -1), yb.reshape(n, -1), sg.reshape(n, -1),
                            br_a[l].astype(BF16), br_b[l].astype(BF16), w_out[l].astype(BF16),
                            *_ffn_params(ffn2_norm[l], ffn2_w_in[l], ffn2_w_out[l]), fin,
                            final=(l == DEPTH - 1))
    return h.reshape(B, T, D)
```

```python
import functools

import jax
import jax.numpy as jnp
from jax import lax
from jax.experimental import pallas as pl
from jax.experimental.pallas import tpu as pltpu

F32 = jnp.float32
BF16 = jnp.bfloat16

D_MODEL = 1024
DEPTH = 4
CHUNK = 64
D_RWKV = 512
RWKV_HEAD = 64
RWKV_HEADS = 8
LORA_W, LORA_A, LORA_V, LORA_G = 64, 64, 32, 160
RWKV_GN_EPS = 64e-5
D_MLSTM = 512
MLSTM_HEADS = 4
MLSTM_HEAD = 128
CONV_W = 4
MLSTM_NORM_EPS = 1e-5
D_FF = 2816
NORM_EPS = 1e-6
N_RW_IN = 3 * D_RWKV + LORA_W + LORA_A + LORA_G
N_ML_IN = 4 * D_MLSTM + 2 * MLSTM_HEADS

SM_W = 384
SM_IF = 288
SM_VRES = 296
IF_LANE = SM_IF - 256

FF_CHUNK = 256
TM_FFN = 1024
TM_MERGE = 512
TT_MIX = 256
VMEM_LIMIT = 56 * 1024 * 1024


def _const_spec(shape):
    nd = len(shape)
    return pl.BlockSpec(shape, lambda *_: (0,) * nd, pipeline_mode=pl.Buffered(1))


def _rms(h, g):
    return h * lax.rsqrt(jnp.mean(h * h, axis=-1, keepdims=True) + NORM_EPS) * g


def _sigmoid(x):
    return 0.5 * jnp.tanh(0.5 * x) + 0.5


def _dot(a, b):
    return jnp.dot(a, b, preferred_element_type=F32)


def _seg_sum(x, seg_ref):
    xb = x.astype(BF16)
    gw = seg_ref.shape[0]
    return jnp.concatenate([_dot(xb[:, lo:lo + gw], seg_ref[...]) for lo in range(0, x.shape[1], gw)], axis=1)


def _ffn_body(h, g_ref, wg_ref, wu_ref, wo_ref, acc_ref):
    xb = _rms(h, g_ref[...]).astype(BF16)
    nc = D_FF // FF_CHUNK
    for c in range(nc):
        gate = _dot(xb, wg_ref[c])
        up = _dot(xb, wu_ref[c])
        act = (gate * _sigmoid(gate) * up).astype(BF16)
        part = _dot(act, wo_ref[c])
        if c == 0:
            acc_ref[...] = part
        elif c < nc - 1:
            acc_ref[...] += part
    return h + 0.5 * (acc_ref[...] + part)


def _ffn_kernel(h_ref, g_ref, wg_ref, wu_ref, wo_ref, out_ref, acc_ref):
    out_ref[...] = _ffn_body(h_ref[...], g_ref, wg_ref, wu_ref, wo_ref, acc_ref)


def _merge_ffn_kernel(h_ref, ya_ref, yb_ref, sg_ref, bra_ref, brb_ref, wout_ref,
                      g_ref, wg_ref, wu_ref, wo_ref, fin_ref, out_ref, acc_ref, *, final):
    sg = sg_ref[...]
    pa = _dot(ya_ref[...].astype(BF16), bra_ref[...])
    pb = _dot(yb_ref[...].astype(BF16), brb_ref[...])
    u = sg[:, :D_MODEL] * pa + sg[:, D_MODEL:] * pb
    h = h_ref[...] + _dot(u.astype(BF16), wout_ref[...])
    h = _ffn_body(h, g_ref, wg_ref, wu_ref, wo_ref, acc_ref)
    if final:
        h = _rms(h, fin_ref[...])
    out_ref[...] = h


def _ffn_weight_specs():
    nc = D_FF // FF_CHUNK
    return [_const_spec((1, D_MODEL)), _const_spec((nc, D_MODEL, FF_CHUNK)),
            _const_spec((nc, D_MODEL, FF_CHUNK)), _const_spec((nc, FF_CHUNK, D_MODEL))]


def _ffn_call(h, g, wg, wu, wo):
    n = h.shape[0]
    tm = min(TM_FFN, n)
    row = pl.BlockSpec((tm, D_MODEL), lambda i: (i, 0))
    return pl.pallas_call(
        _ffn_kernel,
        grid=(n // tm,),
        in_specs=[row] + _ffn_weight_specs(),
        out_specs=row,
        out_shape=jax.ShapeDtypeStruct((n, D_MODEL), F32),
        scratch_shapes=[pltpu.VMEM((tm, D_MODEL), F32)],
        compiler_params=pltpu.CompilerParams(dimension_semantics=("arbitrary",),
                                             vmem_limit_bytes=VMEM_LIMIT),
        name="ffn",
    )(h, g, wg, wu, wo)


def _merge_ffn_call(h, ya, yb, sg, bra, brb, wout, g, wg, wu, wo, fin, final):
    n = h.shape[0]
    tm = min(TM_MERGE, n)
    row = lambda w: pl.BlockSpec((tm, w), lambda i: (i, 0))
    return pl.pallas_call(
        functools.partial(_merge_ffn_kernel, final=final),
        grid=(n // tm,),
        in_specs=[row(D_MODEL), row(D_RWKV), row(D_MLSTM), row(2 * D_MODEL),
                  _const_spec((D_RWKV, D_MODEL)), _const_spec((D_MLSTM, D_MODEL)),
                  _const_spec((D_MODEL, D_MODEL))] + _ffn_weight_specs() + [_const_spec((1, D_MODEL))],
        out_specs=row(D_MODEL),
        out_shape=jax.ShapeDtypeStruct((n, D_MODEL), F32),
        scratch_shapes=[pltpu.VMEM((tm, D_MODEL), F32)],
        compiler_params=pltpu.CompilerParams(dimension_semantics=("arbitrary",),
                                             vmem_limit_bytes=VMEM_LIMIT),
        name="merge_ffn",
    )(h, ya, yb, sg, bra, brb, wout, g, wg, wu, wo, fin)


def _bdot(a, b):
    return jnp.dot(a.astype(BF16), b.astype(BF16), preferred_element_type=F32)


def _bdot_nt(a, b):
    return lax.dot_general(a.astype(BF16), b.astype(BF16), (((1,), (1,)), ((), ())),
                           preferred_element_type=F32)


def _bdot_tn(a, b):
    return lax.dot_general(a.astype(BF16), b.astype(BF16), (((0,), (0,)), ((), ())),
                           preferred_element_type=F32)


def _split3(x):
    hi = x.astype(BF16)
    r1 = x - hi.astype(F32)
    mid = r1.astype(BF16)
    lo = (r1 - mid.astype(F32)).astype(BF16)
    return hi, mid, lo


def _log_sigmoid(x):
    return jnp.minimum(x, 0.0) - jnp.log(1.0 + jnp.exp(-jnp.abs(x)))


def _prep_stages(zr, zs, zm, vfirst_ref, vfirst_o, keep_vfirst, w, s, xb_ref, sg_o, tt):
    D = D_RWKV

    def shifted(buf, lo, hi, mu):
        z = buf[8:8 + tt, lo:hi]
        return z + mu * (buf[7:7 + tt, lo:hi] - z)

    def refill(buf, w_ref, lo, hi):
        buf[:8, lo:hi] = buf[tt:tt + 8, lo:hi]
        buf[8:8 + tt, lo:hi] = _dot(xb_ref[...], w_ref[:, lo:hi])

    z_sm = zs[8:8 + tt, :]
    m_sm = shifted(zs, 0, SM_W, w["mu_sm"][...])
    s["ifb"][...] = z_sm[:, 256:]
    xwa = m_sm[:, :128]
    y = -(w["w0"][...] + _dot(jnp.tanh(xwa).astype(BF16), w["wup"][...]))
    softplus = jnp.maximum(y, 0.0) + jnp.log(1.0 + jnp.exp(-jnp.abs(y)))
    s["ld"][...] = -jnp.exp(-softplus - 0.5)
    a = _sigmoid(w["a0"][...] + _dot(xwa.astype(BF16), w["aup"][...]))
    s["a"][...] = a
    s["g"][...] = _dot(_sigmoid(m_sm[:, 128:]).astype(BF16), w["gup"][...])
    if vfirst_ref is not None:
        vg = _sigmoid(w["vbias"][...] + _dot(z_sm[:, 128:].astype(BF16), w["vup"][...]))
    refill(zs, w["w_sm"], 0, SM_W)
    yield

    mu = w["mu_rkv"]
    k = shifted(zr, D, 2 * D, mu[:, D:2 * D])
    refill(zr, w["w_rkv"], D, 2 * D)
    kk = k * w["k_k"][...]
    s["kk"][...] = kk * lax.rsqrt(jnp.maximum(_seg_sum(kk * kk, w["seg"]), 1e-24))
    k2 = k * (1.0 + (a - 1.0) * w["k_a"][...])
    s["k"][...] = k2
    yield

    v = shifted(zr, 2 * D, 3 * D, mu[:, 2 * D:])
    refill(zr, w["w_rkv"], 2 * D, 3 * D)
    if vfirst_ref is not None:
        v = v + (vfirst_ref[...] - v) * vg
    s["v"][...] = v
    if vfirst_o is not None:
        @pl.when(keep_vfirst)
        def _():
            vfirst_o[...] = v
    yield

    r = shifted(zr, 0, D, mu[:, :D])
    refill(zr, w["w_rkv"], 0, D)
    s["r"][...] = r
    s["bonus"][...] = _seg_sum(r * k2 * w["r_k"][...], w["seg"]) * v
    yield

    cw = w["convw"]
    for name, lo, scale in (("q", 0, MLSTM_HEAD ** -0.5), ("mk", D_MLSTM, 1.0)):
        hi = lo + D_MLSTM
        conv = w["convb"][:, lo:hi] + cw[CONV_W - 1:CONV_W, lo:hi] * zm[8:8 + tt, lo:hi]
        for sft in range(1, CONV_W):
            conv = conv + cw[CONV_W - 1 - sft:CONV_W - sft, lo:hi] * zm[8 - sft:8 - sft + tt, lo:hi]
        refill(zm, w["w_ml"], lo, hi)
        s[name][...] = conv * _sigmoid(conv) * scale
        yield
    s["mv"][...] = zm[8:8 + tt, 2 * D_MLSTM:3 * D_MLSTM]
    refill(zm, w["w_ml"], 2 * D_MLSTM, 3 * D_MLSTM)
    yield
    s["o"][...] = _sigmoid(zm[8:8 + tt, 3 * D_MLSTM:])
    refill(zm, w["w_ml"], 3 * D_MLSTM, 4 * D_MLSTM)
    yield

    for lo in range(0, 2 * D_MODEL, D_MLSTM):
        sg_o[:, lo:lo + D_MLSTM] = _sigmoid(_dot(xb_ref[...], w["w_gate"][:, lo:lo + D_MLSTM]))
        yield


def _rwkv_stages(s, w, y_ref, h_ref):
    C, N = CHUNK, RWKV_HEAD
    tt = y_ref.shape[0]
    ti = lax.broadcasted_iota(jnp.int32, (C, C), 0)
    si = lax.broadcasted_iota(jnp.int32, (C, C), 1)
    strict = si < ti
    incl = si <= ti
    eye = (si == ti).astype(F32)

    tr = lax.broadcasted_iota(jnp.int32, (tt, tt), 0)
    tc = lax.broadcasted_iota(jnp.int32, (tt, tt), 1)
    same = (tr // C) == (tc // C)
    tri = (same & (tc <= tr)).astype(BF16)
    lw = s["ld"][...]
    kk = s["kk"][...]
    kka = kk * s["a"][...]
    k2 = s["k"][...]
    r_in = s["r"][...]
    V = s["v"][...].astype(BF16)
    bonus = s["bonus"][...]
    g_out = s["g"][...]
    parts = _split3(lw)
    cs = sum(_dot(tri, p) for p in parts)
    nc = tt // C
    cs_tot = jnp.concatenate([jnp.broadcast_to(cs[(c + 1) * C - 1:(c + 1) * C, :], (C, D_RWKV))
                              for c in range(nc)], axis=0)
    yield

    g_inv = jnp.exp(-cs)
    g_end = jnp.exp(cs_tot - cs)
    g_tot = jnp.exp(cs_tot)
    Rt = (r_in * jnp.exp(cs)).astype(BF16)
    At = (-kk * jnp.exp(cs - lw)).astype(BF16)
    Kb = (k2 * g_inv).astype(BF16)
    Bb = (kka * g_inv).astype(BF16)
    Ke = (k2 * g_end).astype(BF16)
    Be = (kka * g_end).astype(BF16)

    nc = tt // C
    inst = [(c, h) for c in range(nc) for h in range(RWKV_HEADS)]
    sl = lambda x, c, h: x[c * C:(c + 1) * C, h * N:(h + 1) * N]
    each = lambda f, *lists: [f(*xs) for xs in zip(*lists)]

    At_i = [sl(At, c, h) for c, h in inst]
    Rt_i = [sl(Rt, c, h) for c, h in inst]
    V_i = [sl(V, c, h) for c, h in inst]
    P = [_bdot_nt(jnp.concatenate([a, r], axis=0),
                  jnp.concatenate([sl(Kb, c, h), sl(Bb, c, h)], axis=0))
         for a, r, (c, h) in zip(At_i, Rt_i, inst)]
    yield
    A_ak = [jnp.where(strict, p[:C, :C], 0.0).astype(BF16) for p in P]
    L = [jnp.where(strict, p[:C, C:], 0.0).astype(BF16) for p in P]
    A_rk = [jnp.where(incl, p[C:, :C], 0.0).astype(BF16) for p in P]
    A_rb = [jnp.where(incl, p[C:, C:], 0.0).astype(BF16) for p in P]
    AV = each(lambda ak, rk, v: _bdot(jnp.concatenate([ak, rk], axis=0), v), A_ak, A_rk, V_i)
    yield
    X = [eye + l for l in L]
    M = each(lambda l: _bdot(l, l), L)
    yield
    for _ in range(4):
        XM = each(lambda x, m: _bdot(jnp.concatenate([x, m], axis=0), m), X, M)
        yield
        X = each(lambda x, xm: x + xm[:C], X, XM)
        M = [xm[C:] for xm in XM]
    Tm = each(lambda x, m: x + _bdot(x, m), X, M)
    yield
    W = each(lambda t, a, av: _bdot(t, jnp.concatenate([a, av[:C].astype(BF16)], axis=1)).astype(BF16),
             Tm, At_i, AV)
    yield
    Z = each(_bdot, A_rb, W)
    yield
    G = [_bdot_tn(sl(Be, c, h), w_) for (c, h), w_ in zip(inst, W)]
    KV = [_bdot_tn(sl(Ke, c, h), v) for (c, h), v in zip(inst, V_i)]
    yield
    RM = [jnp.concatenate([r.astype(F32) + z[:, :N],
                           eye * g_tot[c * C:c * C + 1, h * N:(h + 1) * N] + g[:, :N]], axis=0).astype(BF16)
          for r, z, g, (c, h) in zip(Rt_i, Z, G, inst)]
    Y0 = each(lambda av, z: av[C:] + z[:, N:], AV, Z)
    G0 = each(lambda kv, g: kv + g[:, N:], KV, G)

    Hs = [h_ref[h] for h in range(RWKV_HEADS)]
    y_rows = []
    for c in range(nc):
        RMH = [_bdot(RM[c * RWKV_HEADS + h], Hs[h]) for h in range(RWKV_HEADS)]
        yield
        y_rows.append(jnp.concatenate([RMH[h][:C] + Y0[c * RWKV_HEADS + h]
                                       for h in range(RWKV_HEADS)], axis=1))
        Hs = [RMH[h][C:] + G0[c * RWKV_HEADS + h] for h in range(RWKV_HEADS)]
    for h in range(RWKV_HEADS):
        h_ref[h] = Hs[h]
    y = jnp.concatenate(y_rows, axis=0)
    inv_n = 1.0 / N
    mu = _seg_sum(y, w["seg"]) * inv_n
    yield
    d = y - mu
    var = _seg_sum(d * d, w["seg"]) * inv_n
    yield
    yn = d * lax.rsqrt(var + RWKV_GN_EPS)
    y_ref[...] = (yn * w["gnw"][...] + w["gnb"][...] + bonus) * g_out


def _mlstm_stages(s, w, y_ref, cn_ref, m_ref):
    q_ref, k_ref, v_ref, o_ref, if_ref = s["q"], s["mk"], s["mv"], s["o"], s["ifb"]
    brow_ref, bcol_ref, nw_ref = w["if_brow"], w["if_bcol"], w["ml_nw"]
    C, d = CHUNK, MLSTM_HEAD
    tt = q_ref.shape[0]
    nc = tt // C
    NH = MLSTM_HEADS
    ti = lax.broadcasted_iota(jnp.int32, (C, C), 0)
    si = lax.broadcasted_iota(jnp.int32, (C, C), 1)
    causal = si <= ti
    ones = jnp.ones((C, d), BF16)
    i0 = IF_LANE
    f0 = IF_LANE + NH

    tr = lax.broadcasted_iota(jnp.int32, (tt, tt), 0)
    tc = lax.broadcasted_iota(jnp.int32, (tt, tt), 1)
    same = (tr // C) == (tc // C)
    tril = (same & (tc <= tr)).astype(BF16)
    triu = (same & (tr <= tc)).astype(BF16)
    pre = if_ref[...] + brow_ref[...]
    bcum_cols = sum(_dot(tril, p) for p in _split3(_log_sigmoid(pre)))
    preT = jnp.transpose(if_ref[...])[i0:i0 + 8, :] + bcol_ref[...]
    row8 = lax.broadcasted_iota(jnp.int32, (8, 1), 0)
    bcum_rows = sum(_dot(p, triu) for p in _split3(jnp.where(row8 < NH, preT, _log_sigmoid(preT))))
    inst = [(c, h) for c in range(nc) for h in range(NH)]
    rs = lambda c: slice(c * C, (c + 1) * C)
    hs_ = lambda h: slice(h * d, (h + 1) * d)
    q_i = [q_ref[rs(c), hs_(h)].astype(BF16) for c, h in inst]
    k_i = [k_ref[rs(c), hs_(h)] for c, h in inst]
    ve_i = [jnp.concatenate([v_ref[rs(c), hs_(h)].astype(BF16), ones], axis=1) for c, h in inst]
    o_i = [o_ref[rs(c), hs_(h)] for c, h in inst]
    yield

    bc_i = [bcum_cols[rs(c), f0 + h:f0 + h + 1] for c, h in inst]
    e_c = [pre[rs(c), i0 + h:i0 + h + 1] - bc for (c, h), bc in zip(inst, bc_i)]
    e_r = [preT[h:h + 1, rs(c)] - bcum_rows[NH + h:NH + h + 1, rs(c)] for c, h in inst]
    a_col = [jnp.max(jnp.where(causal, e, -jnp.inf), axis=-1, keepdims=True) for e in e_r]
    yield
    D0 = [jnp.where(causal, jnp.exp(e - a), 0.0) for e, a in zip(e_r, a_col)]
    QK = [_bdot_nt(q, k) for q, k in zip(q_i, k_i)]
    yield
    S0 = [(qk * dm).astype(BF16) for qk, dm in zip(QK, D0)]
    a_end = [a[C - 1:C, :] for a in a_col]
    KV0 = [_bdot_tn(jnp.exp(e - ae) * k, ve) for e, ae, k, ve in zip(e_c, a_end, k_i, ve_i)]
    yield
    intra = [_dot(s0, ve) for s0, ve in zip(S0, ve_i)]
    yield

    cn = [cn_ref[h] for h in range(NH)]
    m_prev = [m_ref[h][0:1, 0:1] for h in range(NH)]
    hh = []
    for c in range(nc):
        ii = [c * NH + h for h in range(NH)]
        QC = [_bdot(q_i[i], cn[h]) for h, i in enumerate(ii)]
        yield
        for h, i in enumerate(ii):
            mu_t = jnp.maximum(m_prev[h], a_col[i])
            tot = jnp.exp(m_prev[h] - mu_t) * QC[h] + jnp.exp(a_col[i] - mu_t) * intra[i]
            hh.append(tot[:, :d] / jnp.maximum(jnp.abs(tot[:, d:]), jnp.exp(-(bc_i[i] + mu_t))))
            mu_end = jnp.maximum(m_prev[h], a_end[i])
            cn[h] = jnp.exp(m_prev[h] - mu_end) * cn[h] + jnp.exp(a_end[i] - mu_end) * KV0[i]
            m_prev[h] = bc_i[i][C - 1:C, :] + mu_end
    for h in range(NH):
        cn_ref[h] = cn[h]
        m_ref[h] = jnp.broadcast_to(m_prev[h], (8, 128))
    mean = [jnp.mean(x, axis=-1, keepdims=True) for x in hh]
    yield
    dd = [x - m for x, m in zip(hh, mean)]
    var = [jnp.mean(x * x, axis=-1, keepdims=True) for x in dd]
    yield
    for (c, h), x, v, o in zip(inst, dd, var, o_i):
        y_ref[rs(c), hs_(h)] = o * (x * lax.rsqrt(v + MLSTM_NORM_EPS) * nw_ref[:, hs_(h)])


def _run_interleaved(streams, head, every):
    for k in head:
        next(streams[k], None)
    live = dict(enumerate(streams))
    rnd = 0
    while live:
        for k in [k for k in live if rnd % every[k] == 0]:
            if next(live[k], StopIteration) is StopIteration:
                del live[k]
        rnd += 1


def _mixer_kernel(*refs, has_vres, nt, n_tiles, wnames, snames):
    refs = list(refs)
    h_ref = refs.pop(0)
    vfirst_ref = refs.pop(0) if has_vres else None
    w = {n: refs.pop(0) for n in wnames}
    ya_o, yb_o, sg_o = refs.pop(0), refs.pop(0), refs.pop(0)
    vfirst_o = None if has_vres else refs.pop(0)
    zr, zs, zm, xb_ref = (refs.pop(0) for _ in range(4))
    s = {n: refs.pop(0) for n in snames}
    h_state, cn_state, m_state = refs
    tt = h_ref.shape[0]
    i = pl.program_id(0)

    @pl.when(i == 0)
    def _():
        for ref in (zr, zs, zm, h_state, cn_state, m_state) + tuple(s.values()):
            ref[...] = jnp.zeros_like(ref)

    @pl.when(lax.rem(i + nt - 1, nt) == 0)
    def _():
        for ref in (zr, zs, zm):
            ref[:8, :] = jnp.zeros((8, ref.shape[1]), F32)

    @pl.when(lax.rem(i + 2 * nt - 2, nt) == 0)
    def _():
        h_state[...] = jnp.zeros_like(h_state)
        cn_state[...] = jnp.zeros_like(cn_state)
        m_state[...] = jnp.zeros_like(m_state)

    xb_ref[...] = _rms(h_ref[...], w["mix_norm"][...]).astype(BF16)
    streams = [_rwkv_stages(s, w, ya_o, h_state), _mlstm_stages(s, w, yb_o, cn_state, m_state),
               _prep_stages(zr, zs, zm, vfirst_ref, vfirst_o, i <= n_tiles, w, s, xb_ref, sg_o, tt)]
    _run_interleaved(streams, head=(0, 1), every=(1, 1, 1))


def _mixer_call(h3, vfirst, p):
    B, T, _ = h3.shape
    tt = min(TT_MIX, T)
    nt = T // tt
    n = B * nt
    has_vres = vfirst is not None
    cur = lambda wd: pl.BlockSpec((None, tt, wd), lambda i: (jnp.minimum(i, n - 1) // nt,
                                                            jnp.minimum(i, n - 1) % nt, 0))
    def lag(wd, by):
        tile = lambda i: jnp.clip(i - by, 0, n - 1)
        return pl.BlockSpec((None, tt, wd), lambda i: (tile(i) // nt, tile(i) % nt, 0))
    wnames = ("mix_norm", "w_rkv", "w_sm", "w_ml", "w_gate", "mu_rkv", "mu_sm", "w0", "wup", "a0", "aup",
              "gup", "k_k", "k_a", "r_k", "vbias", "vup", "convw", "convb", "seg", "gnw", "gnb",
              "if_brow", "if_bcol", "ml_nw")
    snames = ("r", "k", "v", "ld", "kk", "a", "g", "bonus", "q", "mk", "mv", "o", "ifb")
    weights = [p[nm] for nm in wnames]
    in_specs = [cur(D_MODEL)] + ([lag(D_RWKV, 1)] if has_vres else []) + [_const_spec(x.shape) for x in weights]
    out_specs = [lag(D_RWKV, 2), lag(D_MLSTM, 2), cur(2 * D_MODEL)] + ([] if has_vres else [lag(D_RWKV, 1)])
    out_w = [D_RWKV, D_MLSTM, 2 * D_MODEL] + ([] if has_vres else [D_RWKV])
    stage_rows = tt + 32
    scratch = [pltpu.VMEM((stage_rows, 3 * D_RWKV), F32), pltpu.VMEM((stage_rows, SM_W), F32),
               pltpu.VMEM((stage_rows, 4 * D_MLSTM), F32), pltpu.VMEM((tt, D_MODEL), BF16)]
    scratch += [pltpu.VMEM((tt, 128 if nm == "ifb" else D_RWKV), F32) for nm in snames]
    scratch += [pltpu.VMEM((RWKV_HEADS, RWKV_HEAD, RWKV_HEAD), F32),
                pltpu.VMEM((MLSTM_HEADS, MLSTM_HEAD, 2 * MLSTM_HEAD), F32),
                pltpu.VMEM((MLSTM_HEADS, 8, 128), F32)]
    return pl.pallas_call(
        functools.partial(_mixer_kernel, has_vres=has_vres, nt=nt, n_tiles=n, wnames=wnames, snames=snames),
        grid=(n + 2,),
        in_specs=in_specs,
        out_specs=out_specs,
        out_shape=[jax.ShapeDtypeStruct((B, T, wd), F32) for wd in out_w],
        scratch_shapes=scratch,
        compiler_params=pltpu.CompilerParams(dimension_semantics=("arbitrary",),
                                             vmem_limit_bytes=VMEM_LIMIT),
        name="mixer",
    )(*([h3] + ([vfirst] if has_vres else []) + weights))


def _pad_rows(w, rows, at=0):
    out = jnp.zeros((rows, w.shape[1]), w.dtype)
    return out.at[at:at + w.shape[0]].set(w)


def _ffn_params(norm, w_in, w_out):
    nc = D_FF // FF_CHUNK
    wg = w_in[:, :D_FF].reshape(D_MODEL, nc, FF_CHUNK).transpose(1, 0, 2).astype(BF16)
    wu = w_in[:, D_FF:].reshape(D_MODEL, nc, FF_CHUNK).transpose(1, 0, 2).astype(BF16)
    wo = w_out.reshape(nc, FF_CHUNK, D_MODEL).astype(BF16)
    return norm.reshape(1, D_MODEL), wg, wu, wo


def _mixer_params(l, mix_norm, w_in, shift_mu, rw_w0, rw_w_up, rw_a0, rw_a_up, rw_g_up, rw_k_k, rw_k_a,
                  rw_r_k, vres_down, vres_up, vres_bias, ml_conv_w, ml_conv_b, ml_i_bias, ml_f_bias):
    w = w_in[l]
    o_ml = N_RW_IN
    o_gate = N_RW_IN + N_ML_IN
    w_rkv = w[:, :3 * D_RWKV]
    w_lora = w[:, 3 * D_RWKV:N_RW_IN]
    w_if = w[:, o_ml + 4 * D_MLSTM:o_gate]
    w_vd = vres_down[l - 1] if l > 0 else jnp.zeros((D_MODEL, LORA_V), F32)
    pad = jnp.zeros((D_MODEL, SM_W - SM_VRES - LORA_V), F32)
    w_sm = jnp.concatenate([w_lora, w_if, w_vd, pad], axis=1)
    mu = shift_mu[l]
    mu_sm = jnp.concatenate([mu[3 * D_RWKV:], jnp.zeros((SM_W - (N_RW_IN - 3 * D_RWKV),), F32)])
    row = lambda x: x.reshape(1, -1)
    p = {
        "mix_norm": row(mix_norm[l]),
        "w_rkv": w_rkv.astype(BF16),
        "w_sm": w_sm.astype(BF16),
        "w_ml": w[:, o_ml:o_ml + 4 * D_MLSTM].astype(BF16),
        "w_gate": w[:, o_gate:].astype(BF16),
        "mu_rkv": row(mu[:3 * D_RWKV]),
        "mu_sm": row(mu_sm),
        "w0": row(rw_w0[l]),
        "wup": _pad_rows(rw_w_up[l], 128, 0).astype(BF16),
        "a0": row(rw_a0[l]),
        "aup": _pad_rows(rw_a_up[l], 128, LORA_W).astype(BF16),
        "gup": _pad_rows(rw_g_up[l], 256, 0).astype(BF16),
        "k_k": row(rw_k_k[l]),
        "k_a": row(rw_k_a[l]),
        "r_k": row(rw_r_k[l]),
        "vbias": row(vres_bias[l - 1]) if l > 0 else jnp.zeros((1, D_RWKV), F32),
        "vup": (_pad_rows(vres_up[l - 1], 256, SM_VRES - 128) if l > 0
                else jnp.zeros((256, D_RWKV), F32)).astype(BF16),
        "convw": ml_conv_w[l],
        "convb": row(ml_conv_b[l]),
    }
    gate_bias = jnp.concatenate([ml_i_bias[l], ml_f_bias[l]])
    p["if_brow"] = jnp.zeros((1, 128), F32).at[0, IF_LANE:IF_LANE + 8].set(gate_bias)
    p["if_bcol"] = gate_bias.reshape(8, 1)
    return p


def kernel(x, ffn1_norm, ffn1_w_in, ffn1_w_out, mix_norm, w_in, shift_mu, rw_w0, rw_w_up, rw_a0, rw_a_up, rw_g_up, rw_k_k, rw_k_a, rw_r_k, rw_gn_w, rw_gn_b, vres_down, vres_up, vres_bias, ml_conv_w, ml_conv_b, ml_i_bias, ml_f_bias, ml_norm_w, br_a, br_b, w_out, ffn2_norm, ffn2_w_in, ffn2_w_out, final_norm):
    B, T, D = x.shape
    n = B * T
    assert D == D_MODEL and T % min(TT_MIX, T) == 0 and min(TT_MIX, T) % CHUNK == 0
    assert n % min(TM_FFN, n) == 0 and n % min(TM_MERGE, n) == 0
    lane = jnp.arange(256) // RWKV_HEAD
    seg = (lane[:, None] == lane[None, :]).astype(BF16)
    fin = final_norm.reshape(1, D_MODEL)
    h = x.reshape(n, D)
    v_first = None
    for l in range(DEPTH):
        h = _ffn_call(h, *_ffn_params(ffn1_norm[l], ffn1_w_in[l], ffn1_w_out[l]))
        p = _mixer_params(l, mix_norm, w_in, shift_mu, rw_w0, rw_w_up, rw_a0, rw_a_up, rw_g_up, rw_k_k,
                          rw_k_a, rw_r_k, vres_down, vres_up, vres_bias, ml_conv_w, ml_conv_b,
                          ml_i_bias, ml_f_bias)
        p.update(seg=seg, gnw=rw_gn_w[l].reshape(1, -1), gnb=rw_gn_b[l].reshape(1, -1),
                 ml_nw=ml_norm_w[l].reshape(1, -1))
        outs = _mixer_call(h.reshape(B, T, D), v_first, p)
        ya, yb, sg = outs[:3]
        if l == 0:
            v_first = outs[3]
        h = _merge_ffn_call(h, ya.reshape(n, -1), yb.reshape(n, -1), sg.reshape(n, -1),
                            br_a[l].astype(BF16), br_b[l].astype(BF16), w_out[l].astype(BF16),
                            *_ffn_params(ffn2_norm[l], ffn2_w_in[l], ffn2_w_out[l]), fin,
                            final=(l == DEPTH - 1))
    return h.reshape(B, T, D)
```

```python
import functools

import jax
import jax.numpy as jnp
from jax import lax
from jax.experimental import pallas as pl
from jax.experimental.pallas import tpu as pltpu

F32 = jnp.float32
BF16 = jnp.bfloat16

D_MODEL = 1024
DEPTH = 4
CHUNK = 64
D_RWKV = 512
RWKV_HEAD = 64
RWKV_HEADS = 8
RWKV_GROUP = 2
LORA_W, LORA_A, LORA_V, LORA_G = 64, 64, 32, 160
RWKV_GN_EPS = 64e-5
D_MLSTM = 512
MLSTM_HEADS = 4
MLSTM_HEAD = 128
CONV_W = 4
MLSTM_NORM_EPS = 1e-5
D_FF = 2816
NORM_EPS = 1e-6
N_RW_IN = 3 * D_RWKV + LORA_W + LORA_A + LORA_G
N_ML_IN = 4 * D_MLSTM + 2 * MLSTM_HEADS

SM_W = 384
SM_IF = 288
SM_VRES = 296
IF_LANE = SM_IF - 256

FF_CHUNK = 256
TM_FFN = 1024
TM_MERGE = 512
TT_MIX = 256
VMEM_LIMIT = 56 * 1024 * 1024


def _const_spec(shape):
    nd = len(shape)
    return pl.BlockSpec(shape, lambda *_: (0,) * nd, pipeline_mode=pl.Buffered(1))


def _rms(h, g):
    return h * lax.rsqrt(jnp.mean(h * h, axis=-1, keepdims=True) + NORM_EPS) * g


def _sigmoid(x):
    return 0.5 * jnp.tanh(0.5 * x) + 0.5


def _dot(a, b):
    return jnp.dot(a, b, preferred_element_type=F32)


def _seg_sum(x, seg_ref):
    xb = x.astype(BF16)
    gw = seg_ref.shape[0]
    return jnp.concatenate([_dot(xb[:, lo:lo + gw], seg_ref[...]) for lo in range(0, x.shape[1], gw)], axis=1)


def _ffn_body(h, g_ref, wg_ref, wu_ref, wo_ref, acc_ref):
    xb = _rms(h, g_ref[...]).astype(BF16)
    nc = D_FF // FF_CHUNK
    for c in range(nc):
        gate = _dot(xb, wg_ref[c])
        up = _dot(xb, wu_ref[c])
        act = (gate * _sigmoid(gate) * up).astype(BF16)
        part = _dot(act, wo_ref[c])
        if c == 0:
            acc_ref[...] = part
        elif c < nc - 1:
            acc_ref[...] += part
    return h + 0.5 * (acc_ref[...] + part)


def _ffn_kernel(h_ref, g_ref, wg_ref, wu_ref, wo_ref, out_ref, acc_ref):
    out_ref[...] = _ffn_body(h_ref[...], g_ref, wg_ref, wu_ref, wo_ref, acc_ref)


def _merge_ffn_kernel(h_ref, ya_ref, yb_ref, sg_ref, bra_ref, brb_ref, wout_ref,
                      g_ref, wg_ref, wu_ref, wo_ref, fin_ref, out_ref, acc_ref, *, final):
    sg = sg_ref[...]
    pa = _dot(ya_ref[...].astype(BF16), bra_ref[...])
    pb = _dot(yb_ref[...].astype(BF16), brb_ref[...])
    u = sg[:, :D_MODEL] * pa + sg[:, D_MODEL:] * pb
    h = h_ref[...] + _dot(u.astype(BF16), wout_ref[...])
    h = _ffn_body(h, g_ref, wg_ref, wu_ref, wo_ref, acc_ref)
    if final:
        h = _rms(h, fin_ref[...])
    out_ref[...] = h


def _ffn_weight_specs():
    nc = D_FF // FF_CHUNK
    return [_const_spec((1, D_MODEL)), _const_spec((nc, D_MODEL, FF_CHUNK)),
            _const_spec((nc, D_MODEL, FF_CHUNK)), _const_spec((nc, FF_CHUNK, D_MODEL))]


def _ffn_call(h, g, wg, wu, wo):
    n = h.shape[0]
    tm = min(TM_FFN, n)
    row = pl.BlockSpec((tm, D_MODEL), lambda i: (i, 0))
    return pl.pallas_call(
        _ffn_kernel,
        grid=(n // tm,),
        in_specs=[row] + _ffn_weight_specs(),
        out_specs=row,
        out_shape=jax.ShapeDtypeStruct((n, D_MODEL), F32),
        scratch_shapes=[pltpu.VMEM((tm, D_MODEL), F32)],
        compiler_params=pltpu.CompilerParams(dimension_semantics=("arbitrary",),
                                             vmem_limit_bytes=VMEM_LIMIT),
        name="ffn",
    )(h, g, wg, wu, wo)


def _merge_ffn_call(h, ya, yb, sg, bra, brb, wout, g, wg, wu, wo, fin, final):
    n = h.shape[0]
    tm = min(TM_MERGE, n)
    row = lambda w: pl.BlockSpec((tm, w), lambda i: (i, 0))
    return pl.pallas_call(
        functools.partial(_merge_ffn_kernel, final=final),
        grid=(n // tm,),
        in_specs=[row(D_MODEL), row(D_RWKV), row(D_MLSTM), row(2 * D_MODEL),
                  _const_spec((D_RWKV, D_MODEL)), _const_spec((D_MLSTM, D_MODEL)),
                  _const_spec((D_MODEL, D_MODEL))] + _ffn_weight_specs() + [_const_spec((1, D_MODEL))],
        out_specs=row(D_MODEL),
        out_shape=jax.ShapeDtypeStruct((n, D_MODEL), F32),
        scratch_shapes=[pltpu.VMEM((tm, D_MODEL), F32)],
        compiler_params=pltpu.CompilerParams(dimension_semantics=("arbitrary",),
                                             vmem_limit_bytes=VMEM_LIMIT),
        name="merge_ffn",
    )(h, ya, yb, sg, bra, brb, wout, g, wg, wu, wo, fin)


def _bdot(a, b):
    return jnp.dot(a.astype(BF16), b.astype(BF16), preferred_element_type=F32)


def _bdot_nt(a, b):
    return lax.dot_general(a.astype(BF16), b.astype(BF16), (((1,), (1,)), ((), ())),
                           preferred_element_type=F32)


def _bdot_tn(a, b):
    return lax.dot_general(a.astype(BF16), b.astype(BF16), (((0,), (0,)), ((), ())),
                           preferred_element_type=F32)


def _split3(x):
    hi = x.astype(BF16)
    r1 = x - hi.astype(F32)
    mid = r1.astype(BF16)
    lo = (r1 - mid.astype(F32)).astype(BF16)
    return hi, mid, lo


def _log_sigmoid(x):
    return jnp.minimum(x, 0.0) - jnp.log(1.0 + jnp.exp(-jnp.abs(x)))


def _prep_stages(zr, zs, zm, vfirst_ref, vfirst_o, w, s, xb_ref, sg_o, tt):
    D = D_RWKV

    def shifted(buf, lo, hi, mu):
        z = buf[8:8 + tt, lo:hi]
        return z + mu * (buf[7:7 + tt, lo:hi] - z)

    def refill(buf, w_ref, lo, hi):
        buf[:8, lo:hi] = buf[tt:tt + 8, lo:hi]
        buf[8:8 + tt, lo:hi] = _dot(xb_ref[...], w_ref[:, lo:hi])

    z_sm = zs[8:8 + tt, :]
    m_sm = shifted(zs, 0, SM_W, w["mu_sm"][...])
    s["ifb"][...] = z_sm[:, 256:]
    xwa = m_sm[:, :128]
    y = -(w["w0"][...] + _dot(jnp.tanh(xwa).astype(BF16), w["wup"][...]))
    softplus = jnp.maximum(y, 0.0) + jnp.log(1.0 + jnp.exp(-jnp.abs(y)))
    s["ld"][...] = -jnp.exp(-softplus - 0.5)
    a = _sigmoid(w["a0"][...] + _dot(xwa.astype(BF16), w["aup"][...]))
    s["a"][...] = a
    s["g"][...] = _dot(_sigmoid(m_sm[:, 128:]).astype(BF16), w["gup"][...])
    if vfirst_ref is not None:
        vg = _sigmoid(w["vbias"][...] + _dot(z_sm[:, 128:].astype(BF16), w["vup"][...]))
    refill(zs, w["w_sm"], 0, SM_W)
    yield

    mu = w["mu_rkv"]
    k = shifted(zr, D, 2 * D, mu[:, D:2 * D])
    refill(zr, w["w_rkv"], D, 2 * D)
    kk = k * w["k_k"][...]
    s["kk"][...] = kk * lax.rsqrt(jnp.maximum(_seg_sum(kk * kk, w["seg"]), 1e-24))
    k2 = k * (1.0 + (a - 1.0) * w["k_a"][...])
    s["k"][...] = k2
    yield

    v = shifted(zr, 2 * D, 3 * D, mu[:, 2 * D:])
    refill(zr, w["w_rkv"], 2 * D, 3 * D)
    if vfirst_ref is not None:
        v = v + (vfirst_ref[...] - v) * vg
    s["v"][...] = v
    if vfirst_o is not None:
        vfirst_o[...] = v
    yield

    r = shifted(zr, 0, D, mu[:, :D])
    refill(zr, w["w_rkv"], 0, D)
    s["r"][...] = r
    s["bonus"][...] = _seg_sum(r * k2 * w["r_k"][...], w["seg"]) * v
    yield

    cw = w["convw"]
    for name, lo, scale in (("q", 0, MLSTM_HEAD ** -0.5), ("mk", D_MLSTM, 1.0)):
        hi = lo + D_MLSTM
        conv = w["convb"][:, lo:hi] + cw[CONV_W - 1:CONV_W, lo:hi] * zm[8:8 + tt, lo:hi]
        for sft in range(1, CONV_W):
            conv = conv + cw[CONV_W - 1 - sft:CONV_W - sft, lo:hi] * zm[8 - sft:8 - sft + tt, lo:hi]
        refill(zm, w["w_ml"], lo, hi)
        s[name][...] = conv * _sigmoid(conv) * scale
        yield
    s["mv"][...] = zm[8:8 + tt, 2 * D_MLSTM:3 * D_MLSTM]
    refill(zm, w["w_ml"], 2 * D_MLSTM, 3 * D_MLSTM)
    yield
    s["o"][...] = _sigmoid(zm[8:8 + tt, 3 * D_MLSTM:])
    refill(zm, w["w_ml"], 3 * D_MLSTM, 4 * D_MLSTM)
    yield

    for lo in range(0, 2 * D_MODEL, D_MLSTM):
        sg_o[:, lo:lo + D_MLSTM] = _sigmoid(_dot(xb_ref[...], w["w_gate"][:, lo:lo + D_MLSTM]))
        yield


def _rwkv_stages(s, w, y_ref, h_ref):
    C, N = CHUNK, RWKV_HEAD
    tt = y_ref.shape[0]
    ti = lax.broadcasted_iota(jnp.int32, (C, C), 0)
    si = lax.broadcasted_iota(jnp.int32, (C, C), 1)
    strict = si < ti
    incl = si <= ti
    eye = (si == ti).astype(F32)

    tr = lax.broadcasted_iota(jnp.int32, (tt, tt), 0)
    tc = lax.broadcasted_iota(jnp.int32, (tt, tt), 1)
    same = (tr // C) == (tc // C)
    tri = (same & (tc <= tr)).astype(BF16)
    lw = s["ld"][...]
    kk = s["kk"][...]
    kka = kk * s["a"][...]
    k2 = s["k"][...]
    r_in = s["r"][...]
    V = s["v"][...].astype(BF16)
    bonus = s["bonus"][...]
    g_out = s["g"][...]
    parts = _split3(lw)
    cs = sum(_dot(tri, p) for p in parts)
    nc = tt // C
    cs_tot = jnp.concatenate([jnp.broadcast_to(cs[(c + 1) * C - 1:(c + 1) * C, :], (C, D_RWKV))
                              for c in range(nc)], axis=0)
    yield

    g_inv = jnp.exp(-cs)
    g_end = jnp.exp(cs_tot - cs)
    g_tot = jnp.exp(cs_tot)
    Rt = (r_in * jnp.exp(cs)).astype(BF16)
    At = (-kk * jnp.exp(cs - lw)).astype(BF16)
    Kb = (k2 * g_inv).astype(BF16)
    Bb = (kka * g_inv).astype(BF16)
    Ke = (k2 * g_end).astype(BF16)
    Be = (kka * g_end).astype(BF16)

    G = RWKV_GROUP
    GW = G * N
    nq = D_RWKV // GW
    lane = lax.broadcasted_iota(jnp.int32, (C, GW), 1)
    row = lax.broadcasted_iota(jnp.int32, (C, GW), 0)
    head_of = [lane // N == h for h in range(G)]
    strict_g = (lane % N) < row
    incl_g = (lane % N) <= row
    eye_g = ((lane % N) == row).astype(F32)
    br = lax.broadcasted_iota(jnp.int32, (GW, GW), 0)
    bc = lax.broadcasted_iota(jnp.int32, (GW, GW), 1)
    bd_mask = (br // N) == (bc // N)
    eye_q = (br == bc).astype(F32)

    def bd(x):
        x = x.astype(BF16)
        return jnp.concatenate([jnp.where(m, x, jnp.zeros_like(x)) for m in head_of], axis=0)

    inst = [(c, q) for c in range(nc) for q in range(nq)]
    sl = lambda x, c, q: x[c * C:(c + 1) * C, q * GW:(q + 1) * GW]
    each = lambda f, *lists: [f(*xs) for xs in zip(*lists)]

    At_i = [sl(At, c, q) for c, q in inst]
    Rt_i = [sl(Rt, c, q) for c, q in inst]
    V_i = [sl(V, c, q) for c, q in inst]
    P = [_bdot_nt(jnp.concatenate([a, r], axis=0),
                  jnp.concatenate([bd(sl(Kb, c, q)), bd(sl(Bb, c, q))], axis=0))
         for a, r, (c, q) in zip(At_i, Rt_i, inst)]
    yield
    A_ak = [jnp.where(strict_g, p[:C, :GW], 0.0).astype(BF16) for p in P]
    L = [jnp.where(strict_g, p[:C, GW:], 0.0).astype(BF16) for p in P]
    A_rk = [jnp.where(incl_g, p[C:, :GW], 0.0).astype(BF16) for p in P]
    A_rb = [jnp.where(incl_g, p[C:, GW:], 0.0).astype(BF16) for p in P]
    AV = each(lambda ak, rk, v: _bdot(jnp.concatenate([ak, rk], axis=0), bd(v)), A_ak, A_rk, V_i)
    yield
    X = [eye_g + l for l in L]
    M = each(lambda l: _bdot(l, bd(l)), L)
    yield
    for _ in range(4):
        XM = each(lambda x, m: _bdot(jnp.concatenate([x, m], axis=0), bd(m)), X, M)
        yield
        X = each(lambda x, xm: x + xm[:C], X, XM)
        M = [xm[C:] for xm in XM]
    Tm = each(lambda x, m: x + _bdot(x, bd(m)), X, M)
    yield
    W = each(lambda t, a, av: _bdot(t, jnp.concatenate([bd(a), bd(av[:C])], axis=1)).astype(BF16),
             Tm, At_i, AV)
    yield
    Z = each(lambda rb, w_: _bdot(rb, jnp.concatenate([bd(w_[:, :GW]), bd(w_[:, GW:])], axis=1)), A_rb, W)
    yield
    GA = [_bdot_tn(sl(Be, c, q), w_[:, :GW]) for (c, q), w_ in zip(inst, W)]
    G0 = [jnp.where(bd_mask, _bdot_tn(jnp.concatenate([sl(Ke, c, q), sl(Be, c, q)], axis=0),
                                      jnp.concatenate([v, w_[:, GW:]], axis=0)), 0.0)
          for (c, q), v, w_ in zip(inst, V_i, W)]
    yield
    RM = [jnp.concatenate([(r.astype(F32) + z[:, :GW]).astype(BF16),
                           (jnp.where(bd_mask, ga, 0.0)
                            + eye_q * g_tot[c * C:c * C + 1, q * GW:(q + 1) * GW]).astype(BF16)], axis=0)
          for r, z, ga, (c, q) in zip(Rt_i, Z, GA, inst)]
    Y0 = each(lambda av, z: av[C:] + z[:, GW:], AV, Z)

    Hs = [h_ref[q] for q in range(nq)]
    y_rows = []
    for c in range(nc):
        RMH = [_bdot(RM[c * nq + q], Hs[q]) for q in range(nq)]
        yield
        y_rows.append(jnp.concatenate([RMH[q][:C] + Y0[c * nq + q] for q in range(nq)], axis=1))
        Hs = [RMH[q][C:] + G0[c * nq + q] for q in range(nq)]
    for q in range(nq):
        h_ref[q] = Hs[q]
    y = jnp.concatenate(y_rows, axis=0)
    inv_n = 1.0 / N
    mu = _seg_sum(y, w["seg"]) * inv_n
    yield
    d = y - mu
    var = _seg_sum(d * d, w["seg"]) * inv_n
    yield
    yn = d * lax.rsqrt(var + RWKV_GN_EPS)
    y_ref[...] = (yn * w["gnw"][...] + w["gnb"][...] + bonus) * g_out


def _mlstm_stages(s, w, y_ref, cn_ref, m_ref):
    q_ref, k_ref, v_ref, o_ref, if_ref = s["q"], s["mk"], s["mv"], s["o"], s["ifb"]
    brow_ref, bcol_ref, nw_ref = w["if_brow"], w["if_bcol"], w["ml_nw"]
    C, d = CHUNK, MLSTM_HEAD
    tt = q_ref.shape[0]
    nc = tt // C
    NH = MLSTM_HEADS
    ti = lax.broadcasted_iota(jnp.int32, (C, C), 0)
    si = lax.broadcasted_iota(jnp.int32, (C, C), 1)
    causal = si <= ti
    ones = jnp.ones((C, d), BF16)
    i0 = IF_LANE
    f0 = IF_LANE + NH

    tr = lax.broadcasted_iota(jnp.int32, (tt, tt), 0)
    tc = lax.broadcasted_iota(jnp.int32, (tt, tt), 1)
    same = (tr // C) == (tc // C)
    tril = (same & (tc <= tr)).astype(BF16)
    triu = (same & (tr <= tc)).astype(BF16)
    pre = if_ref[...] + brow_ref[...]
    bcum_cols = sum(_dot(tril, p) for p in _split3(_log_sigmoid(pre)))
    preT = jnp.transpose(if_ref[...])[i0:i0 + 8, :] + bcol_ref[...]
    row8 = lax.broadcasted_iota(jnp.int32, (8, 1), 0)
    bcum_rows = sum(_dot(p, triu) for p in _split3(jnp.where(row8 < NH, preT, _log_sigmoid(preT))))
    inst = [(c, h) for c in range(nc) for h in range(NH)]
    rs = lambda c: slice(c * C, (c + 1) * C)
    hs_ = lambda h: slice(h * d, (h + 1) * d)
    q_i = [q_ref[rs(c), hs_(h)].astype(BF16) for c, h in inst]
    k_i = [k_ref[rs(c), hs_(h)] for c, h in inst]
    ve_i = [jnp.concatenate([v_ref[rs(c), hs_(h)].astype(BF16), ones], axis=1) for c, h in inst]
    o_i = [o_ref[rs(c), hs_(h)] for c, h in inst]
    yield

    bc_i = [bcum_cols[rs(c), f0 + h:f0 + h + 1] for c, h in inst]
    e_c = [pre[rs(c), i0 + h:i0 + h + 1] - bc for (c, h), bc in zip(inst, bc_i)]
    e_r = [preT[h:h + 1, rs(c)] - bcum_rows[NH + h:NH + h + 1, rs(c)] for c, h in inst]
    a_col = [jnp.max(jnp.where(causal, e, -jnp.inf), axis=-1, keepdims=True) for e in e_r]
    yield
    D0 = [jnp.where(causal, jnp.exp(e - a), 0.0) for e, a in zip(e_r, a_col)]
    QK = [_bdot_nt(q, k) for q, k in zip(q_i, k_i)]
    yield
    S0 = [(qk * dm).astype(BF16) for qk, dm in zip(QK, D0)]
    a_end = [a[C - 1:C, :] for a in a_col]
    KV0 = [_bdot_tn(jnp.exp(e - ae) * k, ve) for e, ae, k, ve in zip(e_c, a_end, k_i, ve_i)]
    yield
    intra = [_dot(s0, ve) for s0, ve in zip(S0, ve_i)]
    yield

    cn = [cn_ref[h] for h in range(NH)]
    m_prev = [m_ref[h][0:1, 0:1] for h in range(NH)]
    hh = []
    for c in range(nc):
        ii = [c * NH + h for h in range(NH)]
        QC = [_bdot(q_i[i], cn[h]) for h, i in enumerate(ii)]
        yield
        for h, i in enumerate(ii):
            mu_t = jnp.maximum(m_prev[h], a_col[i])
            tot = jnp.exp(m_prev[h] - mu_t) * QC[h] + jnp.exp(a_col[i] - mu_t) * intra[i]
            hh.append(tot[:, :d] / jnp.maximum(jnp.abs(tot[:, d:]), jnp.exp(-(bc_i[i] + mu_t))))
            mu_end = jnp.maximum(m_prev[h], a_end[i])
            cn[h] = jnp.exp(m_prev[h] - mu_end) * cn[h] + jnp.exp(a_end[i] - mu_end) * KV0[i]
            m_prev[h] = bc_i[i][C - 1:C, :] + mu_end
    for h in range(NH):
        cn_ref[h] = cn[h]
        m_ref[h] = jnp.broadcast_to(m_prev[h], (8, 128))
    mean = [jnp.mean(x, axis=-1, keepdims=True) for x in hh]
    yield
    dd = [x - m for x, m in zip(hh, mean)]
    var = [jnp.mean(x * x, axis=-1, keepdims=True) for x in dd]
    yield
    for (c, h), x, v, o in zip(inst, dd, var, o_i):
        y_ref[rs(c), hs_(h)] = o * (x * lax.rsqrt(v + MLSTM_NORM_EPS) * nw_ref[:, hs_(h)])


def _run_interleaved(streams, head, every):
    for k in head:
        next(streams[k], None)
    live = dict(enumerate(streams))
    rnd = 0
    while live:
        for k in [k for k in live if rnd % every[k] == 0]:
            if next(live[k], StopIteration) is StopIteration:
                del live[k]
        rnd += 1


def _mixer_kernel(*refs, has_vres, nt, wnames, snames):
    refs = list(refs)
    h_ref = refs.pop(0)
    vfirst_ref = refs.pop(0) if has_vres else None
    w = {n: refs.pop(0) for n in wnames}
    ya_o, yb_o, sg_o = refs.pop(0), refs.pop(0), refs.pop(0)
    vfirst_o = None if has_vres else refs.pop(0)
    zr, zs, zm, xb_ref = (refs.pop(0) for _ in range(4))
    s = {n: refs.pop(0) for n in snames}
    h_state, cn_state, m_state = refs
    tt = h_ref.shape[0]
    i = pl.program_id(0)

    @pl.when(i == 0)
    def _():
        for ref in (zr, zs, zm, h_state, cn_state, m_state) + tuple(s.values()):
            ref[...] = jnp.zeros_like(ref)

    @pl.when(lax.rem(i + nt - 1, nt) == 0)
    def _():
        for ref in (zr, zs, zm):
            ref[:8, :] = jnp.zeros((8, ref.shape[1]), F32)

    @pl.when(lax.rem(i + 2 * nt - 2, nt) == 0)
    def _():
        h_state[...] = jnp.zeros_like(h_state)
        cn_state[...] = jnp.zeros_like(cn_state)
        m_state[...] = jnp.zeros_like(m_state)

    xb_ref[...] = _rms(h_ref[...], w["mix_norm"][...]).astype(BF16)
    streams = [_rwkv_stages(s, w, ya_o, h_state), _mlstm_stages(s, w, yb_o, cn_state, m_state),
               _prep_stages(zr, zs, zm, vfirst_ref, vfirst_o, w, s, xb_ref, sg_o, tt)]
    _run_interleaved(streams, head=(0, 1), every=(1, 1, 1))


def _mixer_call(h3, vfirst, p):
    B, T, _ = h3.shape
    tt = min(TT_MIX, T)
    nt = T // tt
    n = B * nt
    has_vres = vfirst is not None
    cur = lambda wd: pl.BlockSpec((None, tt, wd), lambda i: (jnp.minimum(i, n - 1) // nt,
                                                            jnp.minimum(i, n - 1) % nt, 0))
    def lag(wd, by):
        tile = lambda i: jnp.clip(i - by, 0, n - 1)
        return pl.BlockSpec((None, tt, wd), lambda i: (tile(i) // nt, tile(i) % nt, 0))
    wnames = ("mix_norm", "w_rkv", "w_sm", "w_ml", "w_gate", "mu_rkv", "mu_sm", "w0", "wup", "a0", "aup",
              "gup", "k_k", "k_a", "r_k", "vbias", "vup", "convw", "convb", "seg", "gnw", "gnb",
              "if_brow", "if_bcol", "ml_nw")
    snames = ("r", "k", "v", "ld", "kk", "a", "g", "bonus", "q", "mk", "mv", "o", "ifb")
    weights = [p[nm] for nm in wnames]
    vf_spec = lambda last: pl.BlockSpec((None, tt, D_RWKV), lambda i: (jnp.clip(i - 1, 0, last), 0, 0))
    in_specs = [cur(D_MODEL)] + ([vf_spec(n - 1)] if has_vres else []) + [_const_spec(x.shape) for x in weights]
    out_specs = [lag(D_RWKV, 2), lag(D_MLSTM, 2), cur(2 * D_MODEL)] + ([] if has_vres else [vf_spec(n)])
    out_shape = [jax.ShapeDtypeStruct((B, T, wd), F32) for wd in (D_RWKV, D_MLSTM, 2 * D_MODEL)]
    if not has_vres:
        out_shape.append(jax.ShapeDtypeStruct((n + 1, tt, D_RWKV), F32))
    stage_rows = tt + 32
    scratch = [pltpu.VMEM((stage_rows, 3 * D_RWKV), F32), pltpu.VMEM((stage_rows, SM_W), F32),
               pltpu.VMEM((stage_rows, 4 * D_MLSTM), F32), pltpu.VMEM((tt, D_MODEL), BF16)]
    scratch += [pltpu.VMEM((tt, 128 if nm == "ifb" else D_RWKV), F32) for nm in snames]
    scratch += [pltpu.VMEM((RWKV_HEADS // RWKV_GROUP, RWKV_GROUP * RWKV_HEAD, RWKV_GROUP * RWKV_HEAD), F32),
                pltpu.VMEM((MLSTM_HEADS, MLSTM_HEAD, 2 * MLSTM_HEAD), F32),
                pltpu.VMEM((MLSTM_HEADS, 8, 128), F32)]
    return pl.pallas_call(
        functools.partial(_mixer_kernel, has_vres=has_vres, nt=nt, wnames=wnames, snames=snames),
        grid=(n + 2,),
        in_specs=in_specs,
        out_specs=out_specs,
        out_shape=out_shape,
        scratch_shapes=scratch,
        compiler_params=pltpu.CompilerParams(dimension_semantics=("arbitrary",),
                                             vmem_limit_bytes=VMEM_LIMIT),
        name="mixer",
    )(*([h3] + ([vfirst] if has_vres else []) + weights))


def _pad_rows(w, rows, at=0):
    out = jnp.zeros((rows, w.shape[1]), w.dtype)
    return out.at[at:at + w.shape[0]].set(w)


def _ffn_params(norm, w_in, w_out):
    nc = D_FF // FF_CHUNK
    wg = w_in[:, :D_FF].reshape(D_MODEL, nc, FF_CHUNK).transpose(1, 0, 2).astype(BF16)
    wu = w_in[:, D_FF:].reshape(D_MODEL, nc, FF_CHUNK).transpose(1, 0, 2).astype(BF16)
    wo = w_out.reshape(nc, FF_CHUNK, D_MODEL).astype(BF16)
    return norm.reshape(1, D_MODEL), wg, wu, wo


def _mixer_params(l, mix_norm, w_in, shift_mu, rw_w0, rw_w_up, rw_a0, rw_a_up, rw_g_up, rw_k_k, rw_k_a,
                  rw_r_k, vres_down, vres_up, vres_bias, ml_conv_w, ml_conv_b, ml_i_bias, ml_f_bias):
    w = w_in[l]
    o_ml = N_RW_IN
    o_gate = N_RW_IN + N_ML_IN
    w_rkv = w[:, :3 * D_RWKV]
    w_lora = w[:, 3 * D_RWKV:N_RW_IN]
    w_if = w[:, o_ml + 4 * D_MLSTM:o_gate]
    w_vd = vres_down[l - 1] if l > 0 else jnp.zeros((D_MODEL, LORA_V), F32)
    pad = jnp.zeros((D_MODEL, SM_W - SM_VRES - LORA_V), F32)
    w_sm = jnp.concatenate([w_lora, w_if, w_vd, pad], axis=1)
    mu = shift_mu[l]
    mu_sm = jnp.concatenate([mu[3 * D_RWKV:], jnp.zeros((SM_W - (N_RW_IN - 3 * D_RWKV),), F32)])
    row = lambda x: x.reshape(1, -1)
    p = {
        "mix_norm": row(mix_norm[l]),
        "w_rkv": w_rkv.astype(BF16),
        "w_sm": w_sm.astype(BF16),
        "w_ml": w[:, o_ml:o_ml + 4 * D_MLSTM].astype(BF16),
        "w_gate": w[:, o_gate:].astype(BF16),
        "mu_rkv": row(mu[:3 * D_RWKV]),
        "mu_sm": row(mu_sm),
        "w0": row(rw_w0[l]),
        "wup": _pad_rows(rw_w_up[l], 128, 0).astype(BF16),
        "a0": row(rw_a0[l]),
        "aup": _pad_rows(rw_a_up[l], 128, LORA_W).astype(BF16),
        "gup": _pad_rows(rw_g_up[l], 256, 0).astype(BF16),
        "k_k": row(rw_k_k[l]),
        "k_a": row(rw_k_a[l]),
        "r_k": row(rw_r_k[l]),
        "vbias": row(vres_bias[l - 1]) if l > 0 else jnp.zeros((1, D_RWKV), F32),
        "vup": (_pad_rows(vres_up[l - 1], 256, SM_VRES - 128) if l > 0
                else jnp.zeros((256, D_RWKV), F32)).astype(BF16),
        "convw": ml_conv_w[l],
        "convb": row(ml_conv_b[l]),
    }
    gate_bias = jnp.concatenate([ml_i_bias[l], ml_f_bias[l]])
    p["if_brow"] = jnp.zeros((1, 128), F32).at[0, IF_LANE:IF_LANE + 8].set(gate_bias)
    p["if_bcol"] = gate_bias.reshape(8, 1)
    return p


def kernel(x, ffn1_norm, ffn1_w_in, ffn1_w_out, mix_norm, w_in, shift_mu, rw_w0, rw_w_up, rw_a0, rw_a_up, rw_g_up, rw_k_k, rw_k_a, rw_r_k, rw_gn_w, rw_gn_b, vres_down, vres_up, vres_bias, ml_conv_w, ml_conv_b, ml_i_bias, ml_f_bias, ml_norm_w, br_a, br_b, w_out, ffn2_norm, ffn2_w_in, ffn2_w_out, final_norm):
    B, T, D = x.shape
    n = B * T
    assert D == D_MODEL and T % min(TT_MIX, T) == 0 and min(TT_MIX, T) % CHUNK == 0
    assert n % min(TM_FFN, n) == 0 and n % min(TM_MERGE, n) == 0
    lane = jnp.arange(256) // RWKV_HEAD
    seg = (lane[:, None] == lane[None, :]).astype(BF16)
    fin = final_norm.reshape(1, D_MODEL)
    h = x.reshape(n, D)
    v_first = None
    for l in range(DEPTH):
        h = _ffn_call(h, *_ffn_params(ffn1_norm[l], ffn1_w_in[l], ffn1_w_out[l]))
        p = _mixer_params(l, mix_norm, w_in, shift_mu, rw_w0, rw_w_up, rw_a0, rw_a_up, rw_g_up, rw_k_k,
                          rw_k_a, rw_r_k, vres_down, vres_up, vres_bias, ml_conv_w, ml_conv_b,
                          ml_i_bias, ml_f_bias)
        p.update(seg=seg, gnw=rw_gn_w[l].reshape(1, -1), gnb=rw_gn_b[l].reshape(1, -1),
                 ml_nw=ml_norm_w[l].reshape(1, -1))
        outs = _mixer_call(h.reshape(B, T, D), v_first, p)
        ya, yb, sg = outs[:3]
        if l == 0:
            v_first = outs[3]
        h = _merge_ffn_call(h, ya.reshape(n, -1), yb.reshape(n, -1), sg.reshape(n, -1),
                            br_a[l].astype(BF16), br_b[l].astype(BF16), w_out[l].astype(BF16),
                            *_ffn_params(ffn2_norm[l], ffn2_w_in[l], ffn2_w_out[l]), fin,
                            final=(l == DEPTH - 1))
    return h.reshape(B, T, D)
```

```python
import functools

import jax
import jax.numpy as jnp
from jax import lax
from jax.experimental import pallas as pl
from jax.experimental.pallas import tpu as pltpu

F32 = jnp.float32
BF16 = jnp.bfloat16

D_MODEL = 1024
DEPTH = 4
CHUNK = 64
D_RWKV = 512
RWKV_HEAD = 64
RWKV_HEADS = 8
RWKV_GROUP = 2
LORA_W, LORA_A, LORA_V, LORA_G = 64, 64, 32, 160
RWKV_GN_EPS = 64e-5
D_MLSTM = 512
MLSTM_HEADS = 4
MLSTM_HEAD = 128
CONV_W = 4
MLSTM_NORM_EPS = 1e-5
D_FF = 2816
NORM_EPS = 1e-6
N_RW_IN = 3 * D_RWKV + LORA_W + LORA_A + LORA_G
N_ML_IN = 4 * D_MLSTM + 2 * MLSTM_HEADS

SM_W = 384
SM_IF = 288
SM_VRES = 296
IF_LANE = SM_IF - 256

FF_CHUNK = 256
TM_FFN = 1024
TM_MERGE = 512
TT_MIX = 256
STAGE_ROW0 = 16
VMEM_LIMIT = 56 * 1024 * 1024


def _const_spec(shape):
    nd = len(shape)
    return pl.BlockSpec(shape, lambda *_: (0,) * nd, pipeline_mode=pl.Buffered(1))


def _rms(h, g):
    return h * lax.rsqrt(jnp.mean(h * h, axis=-1, keepdims=True) + NORM_EPS) * g


def _sigmoid(x):
    return 0.5 * jnp.tanh(0.5 * x) + 0.5


def _dot(a, b):
    return jnp.dot(a, b, preferred_element_type=F32)


def _seg_sum(x, seg_ref):
    xb = x.astype(BF16)
    gw = seg_ref.shape[0]
    return jnp.concatenate([_dot(xb[:, lo:lo + gw], seg_ref[...]) for lo in range(0, x.shape[1], gw)], axis=1)


def _ffn_body(h, g_ref, wg_ref, wu_ref, wo_ref, acc_ref):
    xb = _rms(h, g_ref[...]).astype(BF16)
    nc = D_FF // FF_CHUNK
    for c in range(nc):
        gate = _dot(xb, wg_ref[c])
        up = _dot(xb, wu_ref[c])
        act = (gate * _sigmoid(gate) * up).astype(BF16)
        part = _dot(act, wo_ref[c])
        if c == 0:
            acc_ref[...] = part
        elif c < nc - 1:
            acc_ref[...] += part
    return h + 0.5 * (acc_ref[...] + part)


def _ffn_kernel(h_ref, g_ref, wg_ref, wu_ref, wo_ref, out_ref, acc_ref):
    out_ref[...] = _ffn_body(h_ref[...], g_ref, wg_ref, wu_ref, wo_ref, acc_ref)


def _merge_ffn_kernel(h_ref, ya_ref, yb_ref, sg_ref, bra_ref, brb_ref, wout_ref,
                      g_ref, wg_ref, wu_ref, wo_ref, fin_ref, out_ref, acc_ref, *, final):
    sg = sg_ref[...]
    pa = _dot(ya_ref[...].astype(BF16), bra_ref[...])
    pb = _dot(yb_ref[...].astype(BF16), brb_ref[...])
    u = sg[:, :D_MODEL] * pa + sg[:, D_MODEL:] * pb
    h = h_ref[...] + _dot(u.astype(BF16), wout_ref[...])
    h = _ffn_body(h, g_ref, wg_ref, wu_ref, wo_ref, acc_ref)
    if final:
        h = _rms(h, fin_ref[...])
    out_ref[...] = h


def _ffn_weight_specs():
    nc = D_FF // FF_CHUNK
    return [_const_spec((1, D_MODEL)), _const_spec((nc, D_MODEL, FF_CHUNK)),
            _const_spec((nc, D_MODEL, FF_CHUNK)), _const_spec((nc, FF_CHUNK, D_MODEL))]


def _ffn_call(h, g, wg, wu, wo):
    n = h.shape[0]
    tm = min(TM_FFN, n)
    row = pl.BlockSpec((tm, D_MODEL), lambda i: (i, 0))
    return pl.pallas_call(
        _ffn_kernel,
        grid=(n // tm,),
        in_specs=[row] + _ffn_weight_specs(),
        out_specs=row,
        out_shape=jax.ShapeDtypeStruct((n, D_MODEL), F32),
        scratch_shapes=[pltpu.VMEM((tm, D_MODEL), F32)],
        compiler_params=pltpu.CompilerParams(dimension_semantics=("arbitrary",),
                                             vmem_limit_bytes=VMEM_LIMIT),
        name="ffn",
    )(h, g, wg, wu, wo)


def _merge_ffn_call(h, ya, yb, sg, bra, brb, wout, g, wg, wu, wo, fin, final):
    n = h.shape[0]
    tm = min(TM_MERGE, n)
    row = lambda w: pl.BlockSpec((tm, w), lambda i: (i, 0))
    return pl.pallas_call(
        functools.partial(_merge_ffn_kernel, final=final),
        grid=(n // tm,),
        in_specs=[row(D_MODEL), row(D_RWKV), row(D_MLSTM), row(2 * D_MODEL),
                  _const_spec((D_RWKV, D_MODEL)), _const_spec((D_MLSTM, D_MODEL)),
                  _const_spec((D_MODEL, D_MODEL))] + _ffn_weight_specs() + [_const_spec((1, D_MODEL))],
        out_specs=row(D_MODEL),
        out_shape=jax.ShapeDtypeStruct((n, D_MODEL), F32),
        scratch_shapes=[pltpu.VMEM((tm, D_MODEL), F32)],
        compiler_params=pltpu.CompilerParams(dimension_semantics=("arbitrary",),
                                             vmem_limit_bytes=VMEM_LIMIT),
        name="merge_ffn",
    )(h, ya, yb, sg, bra, brb, wout, g, wg, wu, wo, fin)


def _bdot(a, b):
    return jnp.dot(a.astype(BF16), b.astype(BF16), preferred_element_type=F32)


def _bdot_nt(a, b):
    return lax.dot_general(a.astype(BF16), b.astype(BF16), (((1,), (1,)), ((), ())),
                           preferred_element_type=F32)


def _bdot_tn(a, b):
    return lax.dot_general(a.astype(BF16), b.astype(BF16), (((0,), (0,)), ((), ())),
                           preferred_element_type=F32)


def _split3(x):
    hi = x.astype(BF16)
    r1 = x - hi.astype(F32)
    mid = r1.astype(BF16)
    lo = (r1 - mid.astype(F32)).astype(BF16)
    return hi, mid, lo


def _log_sigmoid(x):
    return jnp.minimum(x, 0.0) - jnp.log(1.0 + jnp.exp(-jnp.abs(x)))


def _prep_stages(zr, zs, zm, vfirst_ref, vfirst_o, w, s, xb_ref, sg_o, tt):
    D = D_RWKV
    R0 = STAGE_ROW0

    def shifted(buf, lo, hi, mu):
        z = buf[R0:R0 + tt, lo:hi]
        return z + mu * (buf[R0 - 1:R0 - 1 + tt, lo:hi] - z)

    def refill(buf, w_ref, lo, hi):
        buf[R0 - 8:R0, lo:hi] = buf[R0 + tt - 8:R0 + tt, lo:hi]
        buf[R0:R0 + tt, lo:hi] = _dot(xb_ref[...], w_ref[:, lo:hi])

    z_sm = zs[R0:R0 + tt, :]
    m_sm = shifted(zs, 0, SM_W, w["mu_sm"][...])
    s["ifb"][...] = z_sm[:, 256:]
    xwa = m_sm[:, :128]
    y = -(w["w0"][...] + _dot(jnp.tanh(xwa).astype(BF16), w["wup"][...]))
    softplus = jnp.maximum(y, 0.0) + jnp.log(1.0 + jnp.exp(-jnp.abs(y)))
    s["ld"][...] = -jnp.exp(-softplus - 0.5)
    a = _sigmoid(w["a0"][...] + _dot(xwa.astype(BF16), w["aup"][...]))
    s["a"][...] = a
    s["g"][...] = _dot(_sigmoid(m_sm[:, 128:]).astype(BF16), w["gup"][...])
    if vfirst_ref is not None:
        vg = _sigmoid(w["vbias"][...] + _dot(z_sm[:, 128:].astype(BF16), w["vup"][...]))
    refill(zs, w["w_sm"], 0, SM_W)
    yield

    mu = w["mu_rkv"]
    k = shifted(zr, D, 2 * D, mu[:, D:2 * D])
    refill(zr, w["w_rkv"], D, 2 * D)
    kk = k * w["k_k"][...]
    s["kk"][...] = kk * lax.rsqrt(jnp.maximum(_seg_sum(kk * kk, w["seg"]), 1e-24))
    k2 = k * (1.0 + (a - 1.0) * w["k_a"][...])
    s["k"][...] = k2
    yield

    v = shifted(zr, 2 * D, 3 * D, mu[:, 2 * D:])
    refill(zr, w["w_rkv"], 2 * D, 3 * D)
    if vfirst_ref is not None:
        v = v + (vfirst_ref[...] - v) * vg
    s["v"][...] = v
    if vfirst_o is not None:
        vfirst_o[...] = v
    yield

    r = shifted(zr, 0, D, mu[:, :D])
    refill(zr, w["w_rkv"], 0, D)
    s["r"][...] = r
    s["bonus"][...] = _seg_sum(r * k2 * w["r_k"][...], w["seg"]) * v
    yield

    cw = w["convw"]
    for name, lo, scale in (("q", 0, MLSTM_HEAD ** -0.5), ("mk", D_MLSTM, 1.0)):
        hi = lo + D_MLSTM
        conv = w["convb"][:, lo:hi] + cw[CONV_W - 1:CONV_W, lo:hi] * zm[R0:R0 + tt, lo:hi]
        for sft in range(1, CONV_W):
            conv = conv + cw[CONV_W - 1 - sft:CONV_W - sft, lo:hi] * zm[R0 - sft:R0 - sft + tt, lo:hi]
        refill(zm, w["w_ml"], lo, hi)
        s[name][...] = conv * _sigmoid(conv) * scale
        yield
    s["mv"][...] = zm[R0:R0 + tt, 2 * D_MLSTM:3 * D_MLSTM]
    refill(zm, w["w_ml"], 2 * D_MLSTM, 3 * D_MLSTM)
    yield
    s["o"][...] = _sigmoid(zm[R0:R0 + tt, 3 * D_MLSTM:])
    refill(zm, w["w_ml"], 3 * D_MLSTM, 4 * D_MLSTM)
    yield

    for lo in range(0, 2 * D_MODEL, D_MLSTM):
        sg_o[:, lo:lo + D_MLSTM] = _sigmoid(_dot(xb_ref[...], w["w_gate"][:, lo:lo + D_MLSTM]))
        yield


def _rwkv_stages(s, w, y_ref, h_ref):
    C, N = CHUNK, RWKV_HEAD
    tt = y_ref.shape[0]
    ti = lax.broadcasted_iota(jnp.int32, (C, C), 0)
    si = lax.broadcasted_iota(jnp.int32, (C, C), 1)
    strict = si < ti
    incl = si <= ti
    eye = (si == ti).astype(F32)

    tr = lax.broadcasted_iota(jnp.int32, (tt, tt), 0)
    tc = lax.broadcasted_iota(jnp.int32, (tt, tt), 1)
    same = (tr // C) == (tc // C)
    tri = (same & (tc <= tr)).astype(BF16)
    lw = s["ld"][...]
    kk = s["kk"][...]
    kka = kk * s["a"][...]
    k2 = s["k"][...]
    r_in = s["r"][...]
    V = s["v"][...].astype(BF16)
    bonus = s["bonus"][...]
    g_out = s["g"][...]
    parts = _split3(lw)
    cs = sum(_dot(tri, p) for p in parts)
    nc = tt // C
    yield

    E = jnp.exp(cs)
    g_inv = 1.0 / E
    g_tot = jnp.concatenate([jnp.broadcast_to(E[(c + 1) * C - 1:(c + 1) * C, :], (C, D_RWKV))
                             for c in range(nc)], axis=0)
    g_end = g_tot * g_inv
    trow = lax.broadcasted_iota(jnp.int32, (tt, 1), 0)
    E_prev = jnp.where(trow % C == 0, 1.0, pltpu.roll(E, 1, axis=0))
    Rt = (r_in * E).astype(BF16)
    At = (-kk * E_prev).astype(BF16)
    Kb = (k2 * g_inv).astype(BF16)
    Bb = (kka * g_inv).astype(BF16)
    Ke = (k2 * g_end).astype(BF16)
    Be = (kka * g_end).astype(BF16)

    G = RWKV_GROUP
    GW = G * N
    nq = D_RWKV // GW
    lane = lax.broadcasted_iota(jnp.int32, (C, GW), 1)
    row = lax.broadcasted_iota(jnp.int32, (C, GW), 0)
    head_of = [lane // N == h for h in range(G)]
    strict_g = (lane % N) < row
    incl_g = (lane % N) <= row
    eye_g = ((lane % N) == row).astype(F32)
    br = lax.broadcasted_iota(jnp.int32, (GW, GW), 0)
    bc = lax.broadcasted_iota(jnp.int32, (GW, GW), 1)
    bd_mask = (br // N) == (bc // N)
    eye_q = (br == bc).astype(F32)

    def bd(x):
        x = x.astype(BF16)
        return jnp.concatenate([jnp.where(m, x, jnp.zeros_like(x)) for m in head_of], axis=0)

    inst = [(c, q) for c in range(nc) for q in range(nq)]
    sl = lambda x, c, q: x[c * C:(c + 1) * C, q * GW:(q + 1) * GW]
    each = lambda f, *lists: [f(*xs) for xs in zip(*lists)]

    At_i = [sl(At, c, q) for c, q in inst]
    Rt_i = [sl(Rt, c, q) for c, q in inst]
    V_i = [sl(V, c, q) for c, q in inst]
    P = [_bdot_nt(jnp.concatenate([a, r], axis=0),
                  jnp.concatenate([bd(sl(Kb, c, q)), bd(sl(Bb, c, q))], axis=0))
         for a, r, (c, q) in zip(At_i, Rt_i, inst)]
    yield
    A_ak = [jnp.where(strict_g, p[:C, :GW], 0.0).astype(BF16) for p in P]
    L = [jnp.where(strict_g, p[:C, GW:], 0.0).astype(BF16) for p in P]
    A_rk = [jnp.where(incl_g, p[C:, :GW], 0.0).astype(BF16) for p in P]
    A_rb = [jnp.where(incl_g, p[C:, GW:], 0.0).astype(BF16) for p in P]
    AV = each(lambda ak, rk, v: _bdot(jnp.concatenate([ak, rk], axis=0), bd(v)), A_ak, A_rk, V_i)
    yield
    X = [eye_g + l for l in L]
    M = each(lambda l: _bdot(l, bd(l)), L)
    yield
    for _ in range(4):
        XM = each(lambda x, m: _bdot(jnp.concatenate([x, m], axis=0), bd(m)), X, M)
        yield
        X = each(lambda x, xm: x + xm[:C], X, XM)
        M = [xm[C:] for xm in XM]
    Tm = each(lambda x, m: x + _bdot(x, bd(m)), X, M)
    yield
    W = each(lambda t, a, av: _bdot(t, jnp.concatenate([bd(a), bd(av[:C])], axis=1)).astype(BF16),
             Tm, At_i, AV)
    yield
    Z = each(lambda rb, w_: _bdot(rb, jnp.concatenate([bd(w_[:, :GW]), bd(w_[:, GW:])], axis=1)), A_rb, W)
    yield
    GA = [_bdot_tn(sl(Be, c, q), w_[:, :GW]) for (c, q), w_ in zip(inst, W)]
    G0 = [jnp.where(bd_mask, _bdot_tn(jnp.concatenate([sl(Ke, c, q), sl(Be, c, q)], axis=0),
                                      jnp.concatenate([v, w_[:, GW:]], axis=0)), 0.0)
          for (c, q), v, w_ in zip(inst, V_i, W)]
    yield
    RM = [jnp.concatenate([(r.astype(F32) + z[:, :GW]).astype(BF16),
                           (jnp.where(bd_mask, ga, 0.0)
                            + eye_q * g_tot[c * C:c * C + 1, q * GW:(q + 1) * GW]).astype(BF16)], axis=0)
          for r, z, ga, (c, q) in zip(Rt_i, Z, GA, inst)]
    Y0 = each(lambda av, z: av[C:] + z[:, GW:], AV, Z)

    Hs = [h_ref[q] for q in range(nq)]
    y_rows = []
    for c in range(nc):
        RMH = [_bdot(RM[c * nq + q], Hs[q]) for q in range(nq)]
        yield
        y_rows.append(jnp.concatenate([RMH[q][:C] + Y0[c * nq + q] for q in range(nq)], axis=1))
        Hs = [RMH[q][C:] + G0[c * nq + q] for q in range(nq)]
    for q in range(nq):
        h_ref[q] = Hs[q]
    y = jnp.concatenate(y_rows, axis=0)
    inv_n = 1.0 / N
    mu = _seg_sum(y, w["seg"]) * inv_n
    yield
    d = y - mu
    var = _seg_sum(d * d, w["seg"]) * inv_n
    yield
    yn = d * lax.rsqrt(var + RWKV_GN_EPS)
    y_ref[...] = (yn * w["gnw"][...] + w["gnb"][...] + bonus) * g_out


def _mlstm_stages(s, w, y_ref, cn_ref, m_ref):
    q_ref, k_ref, v_ref, o_ref, if_ref = s["q"], s["mk"], s["mv"], s["o"], s["ifb"]
    brow_ref, bcol_ref, nw_ref = w["if_brow"], w["if_bcol"], w["ml_nw"]
    C, d = CHUNK, MLSTM_HEAD
    tt = q_ref.shape[0]
    nc = tt // C
    NH = MLSTM_HEADS
    ti = lax.broadcasted_iota(jnp.int32, (C, C), 0)
    si = lax.broadcasted_iota(jnp.int32, (C, C), 1)
    causal = si <= ti
    ones = jnp.ones((C, d), BF16)
    i0 = IF_LANE
    f0 = IF_LANE + NH

    tr = lax.broadcasted_iota(jnp.int32, (tt, tt), 0)
    tc = lax.broadcasted_iota(jnp.int32, (tt, tt), 1)
    same = (tr // C) == (tc // C)
    tril = (same & (tc <= tr)).astype(BF16)
    triu = (same & (tr <= tc)).astype(BF16)
    pre = if_ref[...] + brow_ref[...]
    bcum_cols = sum(_dot(tril, p) for p in _split3(_log_sigmoid(pre)))
    preT = jnp.transpose(if_ref[...])[i0:i0 + 8, :] + bcol_ref[...]
    row8 = lax.broadcasted_iota(jnp.int32, (8, 1), 0)
    bcum_rows = sum(_dot(p, triu) for p in _split3(jnp.where(row8 < NH, preT, _log_sigmoid(preT))))
    inst = [(c, h) for c in range(nc) for h in range(NH)]
    rs = lambda c: slice(c * C, (c + 1) * C)
    hs_ = lambda h: slice(h * d, (h + 1) * d)
    q_i = [q_ref[rs(c), hs_(h)].astype(BF16) for c, h in inst]
    k_i = [k_ref[rs(c), hs_(h)] for c, h in inst]
    ve_i = [jnp.concatenate([v_ref[rs(c), hs_(h)].astype(BF16), ones], axis=1) for c, h in inst]
    o_i = [o_ref[rs(c), hs_(h)] for c, h in inst]
    yield

    bc_i = [bcum_cols[rs(c), f0 + h:f0 + h + 1] for c, h in inst]
    e_c = [pre[rs(c), i0 + h:i0 + h + 1] - bc for (c, h), bc in zip(inst, bc_i)]
    e_r = [preT[h:h + 1, rs(c)] - bcum_rows[NH + h:NH + h + 1, rs(c)] for c, h in inst]
    a_col = [jnp.max(jnp.where(causal, e, -jnp.inf), axis=-1, keepdims=True) for e in e_r]
    yield
    D0 = [jnp.where(causal, jnp.exp(e - a), 0.0) for e, a in zip(e_r, a_col)]
    QK = [_bdot_nt(q, k) for q, k in zip(q_i, k_i)]
    yield
    S0 = [(qk * dm).astype(BF16) for qk, dm in zip(QK, D0)]
    a_end = [a[C - 1:C, :] for a in a_col]
    KV0 = [_bdot_tn(jnp.exp(e - ae) * k, ve) for e, ae, k, ve in zip(e_c, a_end, k_i, ve_i)]
    yield
    intra = [_dot(s0, ve) for s0, ve in zip(S0, ve_i)]
    yield

    cn = [cn_ref[h] for h in range(NH)]
    m_prev = [m_ref[h][0:1, 0:1] for h in range(NH)]
    hh = []
    for c in range(nc):
        ii = [c * NH + h for h in range(NH)]
        QC = [_bdot(q_i[i], cn[h]) for h, i in enumerate(ii)]
        yield
        for h, i in enumerate(ii):
            mu_t = jnp.maximum(m_prev[h], a_col[i])
            tot = jnp.exp(m_prev[h] - mu_t) * QC[h] + jnp.exp(a_col[i] - mu_t) * intra[i]
            hh.append(tot[:, :d] / jnp.maximum(jnp.abs(tot[:, d:]), jnp.exp(-(bc_i[i] + mu_t))))
            mu_end = jnp.maximum(m_prev[h], a_end[i])
            cn[h] = jnp.exp(m_prev[h] - mu_end) * cn[h] + jnp.exp(a_end[i] - mu_end) * KV0[i]
            m_prev[h] = bc_i[i][C - 1:C, :] + mu_end
    for h in range(NH):
        cn_ref[h] = cn[h]
        m_ref[h] = jnp.broadcast_to(m_prev[h], (8, 128))
    mean = [jnp.mean(x, axis=-1, keepdims=True) for x in hh]
    yield
    dd = [x - m for x, m in zip(hh, mean)]
    var = [jnp.mean(x * x, axis=-1, keepdims=True) for x in dd]
    yield
    for (c, h), x, v, o in zip(inst, dd, var, o_i):
        y_ref[rs(c), hs_(h)] = o * (x * lax.rsqrt(v + MLSTM_NORM_EPS) * nw_ref[:, hs_(h)])


def _run_interleaved(streams, head, every):
    for k in head:
        next(streams[k], None)
    live = dict(enumerate(streams))
    rnd = 0
    while live:
        for k in [k for k in live if rnd % every[k] == 0]:
            if next(live[k], StopIteration) is StopIteration:
                del live[k]
        rnd += 1


def _mixer_kernel(*refs, has_vres, nt, wnames, snames):
    refs = list(refs)
    h_ref = refs.pop(0)
    vfirst_ref = refs.pop(0) if has_vres else None
    w = {n: refs.pop(0) for n in wnames}
    ya_o, yb_o, sg_o = refs.pop(0), refs.pop(0), refs.pop(0)
    vfirst_o = None if has_vres else refs.pop(0)
    zr, zs, zm, xb_ref = (refs.pop(0) for _ in range(4))
    s = {n: refs.pop(0) for n in snames}
    h_state, cn_state, m_state = refs
    tt = h_ref.shape[0]
    i = pl.program_id(0)

    @pl.when(i == 0)
    def _():
        for ref in (zr, zs, zm, h_state, cn_state, m_state) + tuple(s.values()):
            ref[...] = jnp.zeros_like(ref)

    @pl.when(lax.rem(i + nt - 1, nt) == 0)
    def _():
        for ref in (zr, zs, zm):
            ref[:STAGE_ROW0, :] = jnp.zeros((STAGE_ROW0, ref.shape[1]), F32)

    @pl.when(lax.rem(i + 2 * nt - 2, nt) == 0)
    def _():
        h_state[...] = jnp.zeros_like(h_state)
        cn_state[...] = jnp.zeros_like(cn_state)
        m_state[...] = jnp.zeros_like(m_state)

    xb_ref[...] = _rms(h_ref[...], w["mix_norm"][...]).astype(BF16)
    streams = [_rwkv_stages(s, w, ya_o, h_state), _mlstm_stages(s, w, yb_o, cn_state, m_state),
               _prep_stages(zr, zs, zm, vfirst_ref, vfirst_o, w, s, xb_ref, sg_o, tt)]
    _run_interleaved(streams, head=(0, 1), every=(1, 1, 1))


def _mixer_call(h3, vfirst, p):
    B, T, _ = h3.shape
    tt = min(TT_MIX, T)
    nt = T // tt
    n = B * nt
    has_vres = vfirst is not None
    cur = lambda wd: pl.BlockSpec((None, tt, wd), lambda i: (jnp.minimum(i, n - 1) // nt,
                                                            jnp.minimum(i, n - 1) % nt, 0))
    def lag(wd, by):
        tile = lambda i: jnp.clip(i - by, 0, n - 1)
        return pl.BlockSpec((None, tt, wd), lambda i: (tile(i) // nt, tile(i) % nt, 0))
    wnames = ("mix_norm", "w_rkv", "w_sm", "w_ml", "w_gate", "mu_rkv", "mu_sm", "w0", "wup", "a0", "aup",
              "gup", "k_k", "k_a", "r_k", "vbias", "vup", "convw", "convb", "seg", "gnw", "gnb",
              "if_brow", "if_bcol", "ml_nw")
    snames = ("r", "k", "v", "ld", "kk", "a", "g", "bonus", "q", "mk", "mv", "o", "ifb")
    weights = [p[nm] for nm in wnames]
    vf_spec = lambda last: pl.BlockSpec((None, tt, D_RWKV), lambda i: (jnp.clip(i - 1, 0, last), 0, 0))
    in_specs = [cur(D_MODEL)] + ([vf_spec(n - 1)] if has_vres else []) + [_const_spec(x.shape) for x in weights]
    out_specs = [lag(D_RWKV, 2), lag(D_MLSTM, 2), cur(2 * D_MODEL)] + ([] if has_vres else [vf_spec(n)])
    out_shape = [jax.ShapeDtypeStruct((B, T, wd), F32) for wd in (D_RWKV, D_MLSTM, 2 * D_MODEL)]
    if not has_vres:
        out_shape.append(jax.ShapeDtypeStruct((n + 1, tt, D_RWKV), F32))
    stage_rows = tt + 32
    scratch = [pltpu.VMEM((stage_rows, 3 * D_RWKV), F32), pltpu.VMEM((stage_rows, SM_W), F32),
               pltpu.VMEM((stage_rows, 4 * D_MLSTM), F32), pltpu.VMEM((tt, D_MODEL), BF16)]
    scratch += [pltpu.VMEM((tt, 128 if nm == "ifb" else D_RWKV), F32) for nm in snames]
    scratch += [pltpu.VMEM((RWKV_HEADS // RWKV_GROUP, RWKV_GROUP * RWKV_HEAD, RWKV_GROUP * RWKV_HEAD), F32),
                pltpu.VMEM((MLSTM_HEADS, MLSTM_HEAD, 2 * MLSTM_HEAD), F32),
                pltpu.VMEM((MLSTM_HEADS, 8, 128), F32)]
    return pl.pallas_call(
        functools.partial(_mixer_kernel, has_vres=has_vres, nt=nt, wnames=wnames, snames=snames),
        grid=(n + 2,),
        in_specs=in_specs,
        out_specs=out_specs,
        out_shape=out_shape,
        scratch_shapes=scratch,
        compiler_params=pltpu.CompilerParams(dimension_semantics=("arbitrary",),
                                             vmem_limit_bytes=VMEM_LIMIT),
        name="mixer",
    )(*([h3] + ([vfirst] if has_vres else []) + weights))


def _pad_rows(w, rows, at=0):
    out = jnp.zeros((rows, w.shape[1]), w.dtype)
    return out.at[at:at + w.shape[0]].set(w)


def _ffn_params(norm, w_in, w_out):
    nc = D_FF // FF_CHUNK
    wg = w_in[:, :D_FF].reshape(D_MODEL, nc, FF_CHUNK).transpose(1, 0, 2).astype(BF16)
    wu = w_in[:, D_FF:].reshape(D_MODEL, nc, FF_CHUNK).transpose(1, 0, 2).astype(BF16)
    wo = w_out.reshape(nc, FF_CHUNK, D_MODEL).astype(BF16)
    return norm.reshape(1, D_MODEL), wg, wu, wo


def _mixer_params(l, mix_norm, w_in, shift_mu, rw_w0, rw_w_up, rw_a0, rw_a_up, rw_g_up, rw_k_k, rw_k_a,
                  rw_r_k, vres_down, vres_up, vres_bias, ml_conv_w, ml_conv_b, ml_i_bias, ml_f_bias):
    w = w_in[l]
    o_ml = N_RW_IN
    o_gate = N_RW_IN + N_ML_IN
    w_rkv = w[:, :3 * D_RWKV]
    w_lora = w[:, 3 * D_RWKV:N_RW_IN]
    w_if = w[:, o_ml + 4 * D_MLSTM:o_gate]
    w_vd = vres_down[l - 1] if l > 0 else jnp.zeros((D_MODEL, LORA_V), F32)
    pad = jnp.zeros((D_MODEL, SM_W - SM_VRES - LORA_V), F32)
    w_sm = jnp.concatenate([w_lora, w_if, w_vd, pad], axis=1)
    mu = shift_mu[l]
    mu_sm = jnp.concatenate([mu[3 * D_RWKV:], jnp.zeros((SM_W - (N_RW_IN - 3 * D_RWKV),), F32)])
    row = lambda x: x.reshape(1, -1)
    p = {
        "mix_norm": row(mix_norm[l]),
        "w_rkv": w_rkv.astype(BF16),
        "w_sm": w_sm.astype(BF16),
        "w_ml": w[:, o_ml:o_ml + 4 * D_MLSTM].astype(BF16),
        "w_gate": w[:, o_gate:].astype(BF16),
        "mu_rkv": row(mu[:3 * D_RWKV]),
        "mu_sm": row(mu_sm),
        "w0": row(rw_w0[l]),
        "wup": _pad_rows(rw_w_up[l], 128, 0).astype(BF16),
        "a0": row(rw_a0[l]),
        "aup": _pad_rows(rw_a_up[l], 128, LORA_W).astype(BF16),
        "gup": _pad_rows(rw_g_up[l], 256, 0).astype(BF16),
        "k_k": row(rw_k_k[l]),
        "k_a": row(rw_k_a[l]),
        "r_k": row(rw_r_k[l]),
        "vbias": row(vres_bias[l - 1]) if l > 0 else jnp.zeros((1, D_RWKV), F32),
        "vup": (_pad_rows(vres_up[l - 1], 256, SM_VRES - 128) if l > 0
                else jnp.zeros((256, D_RWKV), F32)).astype(BF16),
        "convw": ml_conv_w[l],
        "convb": row(ml_conv_b[l]),
    }
    gate_bias = jnp.concatenate([ml_i_bias[l], ml_f_bias[l]])
    p["if_brow"] = jnp.zeros((1, 128), F32).at[0, IF_LANE:IF_LANE + 8].set(gate_bias)
    p["if_bcol"] = gate_bias.reshape(8, 1)
    return p


def kernel(x, ffn1_norm, ffn1_w_in, ffn1_w_out, mix_norm, w_in, shift_mu, rw_w0, rw_w_up, rw_a0, rw_a_up, rw_g_up, rw_k_k, rw_k_a, rw_r_k, rw_gn_w, rw_gn_b, vres_down, vres_up, vres_bias, ml_conv_w, ml_conv_b, ml_i_bias, ml_f_bias, ml_norm_w, br_a, br_b, w_out, ffn2_norm, ffn2_w_in, ffn2_w_out, final_norm):
    B, T, D = x.shape
    n = B * T
    assert D == D_MODEL and T % min(TT_MIX, T) == 0 and min(TT_MIX, T) % CHUNK == 0
    assert n % min(TM_FFN, n) == 0 and n % min(TM_MERGE, n) == 0
    lane = jnp.arange(256) // RWKV_HEAD
    seg = (lane[:, None] == lane[None, :]).astype(BF16)
    fin = final_norm.reshape(1, D_MODEL)
    h = x.reshape(n, D)
    v_first = None
    for l in range(DEPTH):
        h = _ffn_call(h, *_ffn_params(ffn1_norm[l], ffn1_w_in[l], ffn1_w_out[l]))
        p = _mixer_params(l, mix_norm, w_in, shift_mu, rw_w0, rw_w_up, rw_a0, rw_a_up, rw_g_up, rw_k_k,
                          rw_k_a, rw_r_k, vres_down, vres_up, vres_bias, ml_conv_w, ml_conv_b,
                          ml_i_bias, ml_f_bias)
        p.update(seg=seg, gnw=rw_gn_w[l].reshape(1, -1), gnb=rw_gn_b[l].reshape(1, -1),
                 ml_nw=ml_norm_w[l].reshape(1, -1))
        outs = _mixer_call(h.reshape(B, T, D), v_first, p)
        ya, yb, sg = outs[:3]
        if l == 0:
            v_first = outs[3]
        h = _merge_ffn_call(h, ya.reshape(n, -1), yb.reshape(n, -1), sg.reshape(n, -1),
                            br_a[l].astype(BF16), br_b[l].astype(BF16), w_out[l].astype(BF16),
                            *_ffn_params(ffn2_norm[l], ffn2_w_in[l], ffn2_w_out[l]), fin,
                            final=(l == DEPTH - 1))
    return h.reshape(B, T, D)
```

```python
import functools

import jax
import jax.numpy as jnp
from jax import lax
from jax.experimental import pallas as pl
from jax.experimental.pallas import tpu as pltpu

F32 = jnp.float32
BF16 = jnp.bfloat16

D_MODEL = 1024
DEPTH = 4
CHUNK = 64
D_RWKV = 512
RWKV_HEAD = 64
RWKV_HEADS = 8
RWKV_GROUP = 2
LORA_W, LORA_A, LORA_V, LORA_G = 64, 64, 32, 160
RWKV_GN_EPS = 64e-5
D_MLSTM = 512
MLSTM_HEADS = 4
MLSTM_HEAD = 128
CONV_W = 4
MLSTM_NORM_EPS = 1e-5
D_FF = 2816
NORM_EPS = 1e-6
N_RW_IN = 3 * D_RWKV + LORA_W + LORA_A + LORA_G
N_ML_IN = 4 * D_MLSTM + 2 * MLSTM_HEADS

SM_W = 384
SM_IF = 288
SM_VRES = 296
IF_LANE = SM_IF - 256

FF_CHUNK = 256
TM_FFN = 1024
TM_MERGE = 512
TT_MIX = 256
STAGE_ROW0 = 16
VMEM_LIMIT = 56 * 1024 * 1024


def _const_spec(shape):
    nd = len(shape)
    return pl.BlockSpec(shape, lambda *_: (0,) * nd, pipeline_mode=pl.Buffered(1))


def _rms(h, g):
    return h * lax.rsqrt(jnp.mean(h * h, axis=-1, keepdims=True) + NORM_EPS) * g


def _sigmoid(x):
    return 0.5 * jnp.tanh(0.5 * x) + 0.5


def _dot(a, b):
    return jnp.dot(a, b, preferred_element_type=F32)


def _seg_sum(x, seg_ref):
    xb = x.astype(BF16)
    gw = seg_ref.shape[0]
    return jnp.concatenate([_dot(xb[:, lo:lo + gw], seg_ref[...]) for lo in range(0, x.shape[1], gw)], axis=1)


def _ffn_body(h, g_ref, wg_ref, wu_ref, wo_ref, acc_ref):
    xb = _rms(h, g_ref[...]).astype(BF16)
    nc = D_FF // FF_CHUNK
    for c in range(nc):
        gate = _dot(xb, wg_ref[c])
        up = _dot(xb, wu_ref[c])
        act = (gate * _sigmoid(gate) * up).astype(BF16)
        part = _dot(act, wo_ref[c])
        if c == 0:
            acc_ref[...] = part
        elif c < nc - 1:
            acc_ref[...] += part
    return h + 0.5 * (acc_ref[...] + part)


def _ffn_kernel(h_ref, g_ref, wg_ref, wu_ref, wo_ref, out_ref, acc_ref):
    out_ref[...] = _ffn_body(h_ref[...], g_ref, wg_ref, wu_ref, wo_ref, acc_ref)


def _merge_ffn_kernel(h_ref, ya_ref, yb_ref, sg_ref, bra_ref, brb_ref, wout_ref,
                      g_ref, wg_ref, wu_ref, wo_ref, fin_ref, out_ref, acc_ref, *, final):
    sg = sg_ref[...]
    pa = _dot(ya_ref[...].astype(BF16), bra_ref[...])
    pb = _dot(yb_ref[...].astype(BF16), brb_ref[...])
    u = sg[:, :D_MODEL] * pa + sg[:, D_MODEL:] * pb
    h = h_ref[...] + _dot(u.astype(BF16), wout_ref[...])
    h = _ffn_body(h, g_ref, wg_ref, wu_ref, wo_ref, acc_ref)
    if final:
        h = _rms(h, fin_ref[...])
    out_ref[...] = h


def _ffn_weight_specs():
    nc = D_FF // FF_CHUNK
    return [_const_spec((1, D_MODEL)), _const_spec((nc, D_MODEL, FF_CHUNK)),
            _const_spec((nc, D_MODEL, FF_CHUNK)), _const_spec((nc, FF_CHUNK, D_MODEL))]


def _ffn_call(h, g, wg, wu, wo):
    n = h.shape[0]
    tm = min(TM_FFN, n)
    row = pl.BlockSpec((tm, D_MODEL), lambda i: (i, 0))
    return pl.pallas_call(
        _ffn_kernel,
        grid=(n // tm,),
        in_specs=[row] + _ffn_weight_specs(),
        out_specs=row,
        out_shape=jax.ShapeDtypeStruct((n, D_MODEL), F32),
        scratch_shapes=[pltpu.VMEM((tm, D_MODEL), F32)],
        compiler_params=pltpu.CompilerParams(dimension_semantics=("arbitrary",),
                                             vmem_limit_bytes=VMEM_LIMIT),
        name="ffn",
    )(h, g, wg, wu, wo)


def _merge_ffn_call(h, ya, yb, sg, bra, brb, wout, g, wg, wu, wo, fin, final):
    n = h.shape[0]
    tm = min(TM_MERGE, n)
    row = lambda w: pl.BlockSpec((tm, w), lambda i: (i, 0))
    return pl.pallas_call(
        functools.partial(_merge_ffn_kernel, final=final),
        grid=(n // tm,),
        in_specs=[row(D_MODEL), row(D_RWKV), row(D_MLSTM), row(2 * D_MODEL),
                  _const_spec((D_RWKV, D_MODEL)), _const_spec((D_MLSTM, D_MODEL)),
                  _const_spec((D_MODEL, D_MODEL))] + _ffn_weight_specs() + [_const_spec((1, D_MODEL))],
        out_specs=row(D_MODEL),
        out_shape=jax.ShapeDtypeStruct((n, D_MODEL), F32),
        scratch_shapes=[pltpu.VMEM((tm, D_MODEL), F32)],
        compiler_params=pltpu.CompilerParams(dimension_semantics=("arbitrary",),
                                             vmem_limit_bytes=VMEM_LIMIT),
        name="merge_ffn",
    )(h, ya, yb, sg, bra, brb, wout, g, wg, wu, wo, fin)


def _bdot(a, b):
    return jnp.dot(a.astype(BF16), b.astype(BF16), preferred_element_type=F32)


def _bdot_nt(a, b):
    return lax.dot_general(a.astype(BF16), b.astype(BF16), (((1,), (1,)), ((), ())),
                           preferred_element_type=F32)


def _bdot_tn(a, b):
    return lax.dot_general(a.astype(BF16), b.astype(BF16), (((0,), (0,)), ((), ())),
                           preferred_element_type=F32)


def _split3(x):
    hi = x.astype(BF16)
    r1 = x - hi.astype(F32)
    mid = r1.astype(BF16)
    lo = (r1 - mid.astype(F32)).astype(BF16)
    return hi, mid, lo


def _log_sigmoid(x):
    return jnp.minimum(x, 0.0) - jnp.log(1.0 + jnp.exp(-jnp.abs(x)))


def _prep_stages(zr, zs, zm, vfirst_ref, vfirst_o, w, s, xb_ref, sg_o, tt):
    D = D_RWKV
    R0 = STAGE_ROW0

    def shifted(buf, lo, hi, mu):
        z = buf[R0:R0 + tt, lo:hi]
        return z + mu * (buf[R0 - 1:R0 - 1 + tt, lo:hi] - z)

    def refill(buf, w_ref, lo, hi):
        buf[R0 - 8:R0, lo:hi] = buf[R0 + tt - 8:R0 + tt, lo:hi]
        buf[R0:R0 + tt, lo:hi] = _dot(xb_ref[...], w_ref[:, lo:hi])

    z_sm = zs[R0:R0 + tt, :]
    m_sm = shifted(zs, 0, SM_W, w["mu_sm"][...])
    s["ifb"][...] = z_sm[:, 256:]
    xwa = m_sm[:, :128]
    y = -(w["w0"][...] + _dot(jnp.tanh(xwa).astype(BF16), w["wup"][...]))
    softplus = jnp.maximum(y, 0.0) + jnp.log(1.0 + jnp.exp(-jnp.abs(y)))
    s["ld"][...] = -jnp.exp(-softplus - 0.5)
    a = _sigmoid(w["a0"][...] + _dot(xwa.astype(BF16), w["aup"][...]))
    s["a"][...] = a
    s["g"][...] = _dot(_sigmoid(m_sm[:, 128:]).astype(BF16), w["gup"][...])
    if vfirst_ref is not None:
        vg = _sigmoid(w["vbias"][...] + _dot(z_sm[:, 128:].astype(BF16), w["vup"][...]))
    refill(zs, w["w_sm"], 0, SM_W)
    yield

    mu = w["mu_rkv"]
    k = shifted(zr, D, 2 * D, mu[:, D:2 * D])
    refill(zr, w["w_rkv"], D, 2 * D)
    kk = k * w["k_k"][...]
    s["kk"][...] = kk * lax.rsqrt(jnp.maximum(_seg_sum(kk * kk, w["seg"]), 1e-24))
    k2 = k * (1.0 + (a - 1.0) * w["k_a"][...])
    s["k"][...] = k2
    yield

    v = shifted(zr, 2 * D, 3 * D, mu[:, 2 * D:])
    refill(zr, w["w_rkv"], 2 * D, 3 * D)
    if vfirst_ref is not None:
        v = v + (vfirst_ref[...] - v) * vg
    s["v"][...] = v
    if vfirst_o is not None:
        vfirst_o[...] = v
    yield

    r = shifted(zr, 0, D, mu[:, :D])
    refill(zr, w["w_rkv"], 0, D)
    s["r"][...] = r
    s["bonus"][...] = _seg_sum(r * k2 * w["r_k"][...], w["seg"]) * v
    yield

    cw = w["convw"]
    for name, lo, scale in (("q", 0, MLSTM_HEAD ** -0.5), ("mk", D_MLSTM, 1.0)):
        hi = lo + D_MLSTM
        conv = w["convb"][:, lo:hi] + cw[CONV_W - 1:CONV_W, lo:hi] * zm[R0:R0 + tt, lo:hi]
        for sft in range(1, CONV_W):
            conv = conv + cw[CONV_W - 1 - sft:CONV_W - sft, lo:hi] * zm[R0 - sft:R0 - sft + tt, lo:hi]
        refill(zm, w["w_ml"], lo, hi)
        s[name][...] = conv * _sigmoid(conv) * scale
        yield
    s["mv"][...] = zm[R0:R0 + tt, 2 * D_MLSTM:3 * D_MLSTM]
    refill(zm, w["w_ml"], 2 * D_MLSTM, 3 * D_MLSTM)
    yield
    s["o"][...] = _sigmoid(zm[R0:R0 + tt, 3 * D_MLSTM:])
    refill(zm, w["w_ml"], 3 * D_MLSTM, 4 * D_MLSTM)
    yield

    for lo in range(0, 2 * D_MODEL, D_MLSTM):
        sg_o[:, lo:lo + D_MLSTM] = _sigmoid(_dot(xb_ref[...], w["w_gate"][:, lo:lo + D_MLSTM]))
        yield


def _rwkv_stages(s, w, y_ref, h_ref):
    C, N = CHUNK, RWKV_HEAD
    tt = y_ref.shape[0]
    ti = lax.broadcasted_iota(jnp.int32, (C, C), 0)
    si = lax.broadcasted_iota(jnp.int32, (C, C), 1)
    strict = si < ti
    incl = si <= ti
    eye = (si == ti).astype(F32)

    tr = lax.broadcasted_iota(jnp.int32, (tt, tt), 0)
    tc = lax.broadcasted_iota(jnp.int32, (tt, tt), 1)
    same = (tr // C) == (tc // C)
    tri = (same & (tc <= tr)).astype(BF16)
    lw = s["ld"][...]
    kk = s["kk"][...]
    kka = kk * s["a"][...]
    k2 = s["k"][...]
    r_in = s["r"][...]
    V = s["v"][...].astype(BF16)
    bonus = s["bonus"][...]
    g_out = s["g"][...]
    parts = _split3(lw)
    cs = sum(_dot(tri, p) for p in parts)
    nc = tt // C
    yield

    E = jnp.exp(cs)
    g_inv = 1.0 / E
    g_tot = jnp.concatenate([jnp.broadcast_to(E[(c + 1) * C - 1:(c + 1) * C, :], (C, D_RWKV))
                             for c in range(nc)], axis=0)
    g_end = g_tot * g_inv
    trow = lax.broadcasted_iota(jnp.int32, (tt, 1), 0)
    E_prev = jnp.where(trow % C == 0, 1.0, pltpu.roll(E, 1, axis=0))
    Rt = (r_in * E).astype(BF16)
    At = (-kk * E_prev).astype(BF16)
    Kb = (k2 * g_inv).astype(BF16)
    Bb = (kka * g_inv).astype(BF16)
    Ke = (k2 * g_end).astype(BF16)
    Be = (kka * g_end).astype(BF16)

    G = RWKV_GROUP
    GW = G * N
    nq = D_RWKV // GW
    lane = lax.broadcasted_iota(jnp.int32, (C, GW), 1)
    row = lax.broadcasted_iota(jnp.int32, (C, GW), 0)
    head_of = [lane // N == h for h in range(G)]
    strict_g = (lane % N) < row
    incl_g = (lane % N) <= row
    eye_g = ((lane % N) == row).astype(F32)
    br = lax.broadcasted_iota(jnp.int32, (GW, GW), 0)
    bc = lax.broadcasted_iota(jnp.int32, (GW, GW), 1)
    bd_mask = (br // N) == (bc // N)
    eye_q = (br == bc).astype(F32)

    def bd(x):
        x = x.astype(BF16)
        return jnp.concatenate([jnp.where(m, x, jnp.zeros_like(x)) for m in head_of], axis=0)

    inst = [(c, q) for c in range(nc) for q in range(nq)]
    sl = lambda x, c, q: x[c * C:(c + 1) * C, q * GW:(q + 1) * GW]
    each = lambda f, *lists: [f(*xs) for xs in zip(*lists)]

    At_i = [sl(At, c, q) for c, q in inst]
    Rt_i = [sl(Rt, c, q) for c, q in inst]
    V_i = [sl(V, c, q) for c, q in inst]
    P = [_bdot_nt(jnp.concatenate([a, r], axis=0),
                  jnp.concatenate([bd(sl(Kb, c, q)), bd(sl(Bb, c, q))], axis=0))
         for a, r, (c, q) in zip(At_i, Rt_i, inst)]
    yield
    A_ak = [jnp.where(strict_g, p[:C, :GW], 0.0).astype(BF16) for p in P]
    L = [jnp.where(strict_g, p[:C, GW:], 0.0).astype(BF16) for p in P]
    A_rk = [jnp.where(incl_g, p[C:, :GW], 0.0).astype(BF16) for p in P]
    A_rb = [jnp.where(incl_g, p[C:, GW:], 0.0).astype(BF16) for p in P]
    AV = each(lambda ak, rk, v: _bdot(jnp.concatenate([ak, rk], axis=0), bd(v)), A_ak, A_rk, V_i)
    yield
    X = [eye_g + l for l in L]
    M = each(lambda l: _bdot(l, bd(l)), L)
    yield
    for _ in range(4):
        XM = each(lambda x, m: _bdot(jnp.concatenate([x, m], axis=0), bd(m)), X, M)
        yield
        X = each(lambda x, xm: x + xm[:C], X, XM)
        M = [xm[C:] for xm in XM]
    Tm = each(lambda x, m: x + _bdot(x, bd(m)), X, M)
    yield
    W = each(lambda t, a, av: _bdot(t, jnp.concatenate([bd(a), bd(av[:C])], axis=1)).astype(BF16),
             Tm, At_i, AV)
    yield
    Z = each(lambda rb, w_: _bdot(rb, jnp.concatenate([bd(w_[:, :GW]), bd(w_[:, GW:])], axis=1)), A_rb, W)
    yield
    GA = [_bdot_tn(sl(Be, c, q), w_[:, :GW]) for (c, q), w_ in zip(inst, W)]
    G0 = [jnp.where(bd_mask, _bdot_tn(jnp.concatenate([sl(Ke, c, q), sl(Be, c, q)], axis=0),
                                      jnp.concatenate([v, w_[:, GW:]], axis=0)), 0.0)
          for (c, q), v, w_ in zip(inst, V_i, W)]
    yield
    RM = [jnp.concatenate([(r.astype(F32) + z[:, :GW]).astype(BF16),
                           (jnp.where(bd_mask, ga, 0.0)
                            + eye_q * g_tot[c * C:c * C + 1, q * GW:(q + 1) * GW]).astype(BF16)], axis=0)
          for r, z, ga, (c, q) in zip(Rt_i, Z, GA, inst)]
    Y0 = each(lambda av, z: av[C:] + z[:, GW:], AV, Z)

    Hs = [h_ref[q] for q in range(nq)]
    y_rows = []
    for c in range(nc):
        RMH = [_bdot(RM[c * nq + q], Hs[q]) for q in range(nq)]
        yield
        y_rows.append(jnp.concatenate([RMH[q][:C] + Y0[c * nq + q] for q in range(nq)], axis=1))
        Hs = [RMH[q][C:] + G0[c * nq + q] for q in range(nq)]
    for q in range(nq):
        h_ref[q] = Hs[q]
    y = jnp.concatenate(y_rows, axis=0)
    inv_n = 1.0 / N
    mu = _seg_sum(y, w["seg"]) * inv_n
    yield
    d = y - mu
    var = _seg_sum(d * d, w["seg"]) * inv_n
    yield
    yn = d * lax.rsqrt(var + RWKV_GN_EPS)
    y_ref[...] = (yn * w["gnw"][...] + w["gnb"][...] + bonus) * g_out


def _mlstm_stages(s, w, y_ref, cn_ref, m_ref):
    q_ref, k_ref, v_ref, o_ref, if_ref = s["q"], s["mk"], s["mv"], s["o"], s["ifb"]
    brow_ref, bcol_ref, nw_ref = w["if_brow"], w["if_bcol"], w["ml_nw"]
    C, d = CHUNK, MLSTM_HEAD
    tt = q_ref.shape[0]
    nc = tt // C
    NH = MLSTM_HEADS
    ti = lax.broadcasted_iota(jnp.int32, (C, C), 0)
    si = lax.broadcasted_iota(jnp.int32, (C, C), 1)
    causal = si <= ti
    ones = jnp.ones((C, d), BF16)
    i0 = IF_LANE
    f0 = IF_LANE + NH

    tr = lax.broadcasted_iota(jnp.int32, (tt, tt), 0)
    tc = lax.broadcasted_iota(jnp.int32, (tt, tt), 1)
    same = (tr // C) == (tc // C)
    tril = (same & (tc <= tr)).astype(BF16)
    triu = (same & (tr <= tc)).astype(BF16)
    pre = if_ref[...] + brow_ref[...]
    bcum_cols = sum(_dot(tril, p) for p in _split3(_log_sigmoid(pre)))
    preT = jnp.transpose(if_ref[...])[i0:i0 + 8, :] + bcol_ref[...]
    row8 = lax.broadcasted_iota(jnp.int32, (8, 1), 0)
    bcum_rows = sum(_dot(p, triu) for p in _split3(jnp.where(row8 < NH, preT, _log_sigmoid(preT))))
    inst = [(c, h) for c in range(nc) for h in range(NH)]
    rs = lambda c: slice(c * C, (c + 1) * C)
    hs_ = lambda h: slice(h * d, (h + 1) * d)
    q_i = [q_ref[rs(c), hs_(h)].astype(BF16) for c, h in inst]
    k_i = [k_ref[rs(c), hs_(h)] for c, h in inst]
    ve_i = [jnp.concatenate([v_ref[rs(c), hs_(h)].astype(BF16), ones], axis=1) for c, h in inst]
    o_i = [o_ref[rs(c), hs_(h)] for c, h in inst]
    yield

    bc_i = [bcum_cols[rs(c), f0 + h:f0 + h + 1] for c, h in inst]
    e_c = [pre[rs(c), i0 + h:i0 + h + 1] - bc for (c, h), bc in zip(inst, bc_i)]
    e_r = [preT[h:h + 1, rs(c)] - bcum_rows[NH + h:NH + h + 1, rs(c)] for c, h in inst]
    a_col = [jnp.max(jnp.where(causal, e, -jnp.inf), axis=-1, keepdims=True) for e in e_r]
    yield
    D0 = [jnp.where(causal, jnp.exp(e - a), 0.0) for e, a in zip(e_r, a_col)]
    QK = [_bdot_nt(q, k) for q, k in zip(q_i, k_i)]
    yield
    S0 = [(qk * dm).astype(BF16) for qk, dm in zip(QK, D0)]
    a_end = [a[C - 1:C, :] for a in a_col]
    KV0 = [_bdot_tn(jnp.exp(e - ae) * k, ve) for e, ae, k, ve in zip(e_c, a_end, k_i, ve_i)]
    yield
    intra = [_dot(s0, ve) for s0, ve in zip(S0, ve_i)]
    yield

    cn = [cn_ref[h] for h in range(NH)]
    m_prev = [m_ref[h][0:1, 0:1] for h in range(NH)]
    hh = []
    for c in range(nc):
        ii = [c * NH + h for h in range(NH)]
        QC = [_bdot(q_i[i], cn[h]) for h, i in enumerate(ii)]
        yield
        for h, i in enumerate(ii):
            mu_t = jnp.maximum(m_prev[h], a_col[i])
            tot = jnp.exp(m_prev[h] - mu_t) * QC[h] + jnp.exp(a_col[i] - mu_t) * intra[i]
            hh.append(tot[:, :d] / jnp.maximum(jnp.abs(tot[:, d:]), jnp.exp(-(bc_i[i] + mu_t))))
            mu_end = jnp.maximum(m_prev[h], a_end[i])
            cn[h] = jnp.exp(m_prev[h] - mu_end) * cn[h] + jnp.exp(a_end[i] - mu_end) * KV0[i]
            m_prev[h] = bc_i[i][C - 1:C, :] + mu_end
    for h in range(NH):
        cn_ref[h] = cn[h]
        m_ref[h] = jnp.broadcast_to(m_prev[h], (8, 128))
    mean = [jnp.mean(x, axis=-1, keepdims=True) for x in hh]
    yield
    dd = [x - m for x, m in zip(hh, mean)]
    var = [jnp.mean(x * x, axis=-1, keepdims=True) for x in dd]
    yield
    for (c, h), x, v, o in zip(inst, dd, var, o_i):
        y_ref[rs(c), hs_(h)] = o * (x * lax.rsqrt(v + MLSTM_NORM_EPS) * nw_ref[:, hs_(h)])


def _run_interleaved(streams, head, every):
    for k in head:
        next(streams[k], None)
    live = dict(enumerate(streams))
    rnd = 0
    while live:
        for k in [k for k in live if rnd % every[k] == 0]:
            if next(live[k], StopIteration) is StopIteration:
                del live[k]
        rnd += 1


def _mixer_kernel(*refs, has_vres, nt, wnames, snames):
    refs = list(refs)
    h_ref = refs.pop(0)
    vfirst_ref = refs.pop(0) if has_vres else None
    w = {n: refs.pop(0) for n in wnames}
    ya_o, yb_o, sg_o = refs.pop(0), refs.pop(0), refs.pop(0)
    vfirst_o = None if has_vres else refs.pop(0)
    zr, zs, zm, xb_ref = (refs.pop(0) for _ in range(4))
    s = {n: refs.pop(0) for n in snames}
    h_state, cn_state, m_state = refs
    tt = h_ref.shape[0]
    i = pl.program_id(0)

    @pl.when(i == 0)
    def _():
        for ref in (zr, zs, zm, h_state, cn_state, m_state) + tuple(s.values()):
            ref[...] = jnp.zeros_like(ref)

    @pl.when(lax.rem(i + nt - 1, nt) == 0)
    def _():
        for ref in (zr, zs, zm):
            ref[:STAGE_ROW0, :] = jnp.zeros((STAGE_ROW0, ref.shape[1]), F32)

    @pl.when(lax.rem(i + 2 * nt - 2, nt) == 0)
    def _():
        h_state[...] = jnp.zeros_like(h_state)
        cn_state[...] = jnp.zeros_like(cn_state)
        m_state[...] = jnp.zeros_like(m_state)

    xb_ref[...] = _rms(h_ref[...], w["mix_norm"][...]).astype(BF16)
    streams = [_rwkv_stages(s, w, ya_o, h_state), _mlstm_stages(s, w, yb_o, cn_state, m_state),
               _prep_stages(zr, zs, zm, vfirst_ref, vfirst_o, w, s, xb_ref, sg_o, tt)]
    _run_interleaved(streams, head=(0, 1), every=(1, 3, 2))


def _mixer_call(h3, vfirst, p):
    B, T, _ = h3.shape
    tt = min(TT_MIX, T)
    nt = T // tt
    n = B * nt
    has_vres = vfirst is not None
    cur = lambda wd: pl.BlockSpec((None, tt, wd), lambda i: (jnp.minimum(i, n - 1) // nt,
                                                            jnp.minimum(i, n - 1) % nt, 0))
    def lag(wd, by):
        tile = lambda i: jnp.clip(i - by, 0, n - 1)
        return pl.BlockSpec((None, tt, wd), lambda i: (tile(i) // nt, tile(i) % nt, 0))
    wnames = ("mix_norm", "w_rkv", "w_sm", "w_ml", "w_gate", "mu_rkv", "mu_sm", "w0", "wup", "a0", "aup",
              "gup", "k_k", "k_a", "r_k", "vbias", "vup", "convw", "convb", "seg", "gnw", "gnb",
              "if_brow", "if_bcol", "ml_nw")
    snames = ("r", "k", "v", "ld", "kk", "a", "g", "bonus", "q", "mk", "mv", "o", "ifb")
    weights = [p[nm] for nm in wnames]
    vf_spec = lambda last: pl.BlockSpec((None, tt, D_RWKV), lambda i: (jnp.clip(i - 1, 0, last), 0, 0))
    in_specs = [cur(D_MODEL)] + ([vf_spec(n - 1)] if has_vres else []) + [_const_spec(x.shape) for x in weights]
    out_specs = [lag(D_RWKV, 2), lag(D_MLSTM, 2), cur(2 * D_MODEL)] + ([] if has_vres else [vf_spec(n)])
    out_shape = [jax.ShapeDtypeStruct((B, T, wd), F32) for wd in (D_RWKV, D_MLSTM, 2 * D_MODEL)]
    if not has_vres:
        out_shape.append(jax.ShapeDtypeStruct((n + 1, tt, D_RWKV), F32))
    stage_rows = tt + 32
    scratch = [pltpu.VMEM((stage_rows, 3 * D_RWKV), F32), pltpu.VMEM((stage_rows, SM_W), F32),
               pltpu.VMEM((stage_rows, 4 * D_MLSTM), F32), pltpu.VMEM((tt, D_MODEL), BF16)]
    scratch += [pltpu.VMEM((tt, 128 if nm == "ifb" else D_RWKV), F32) for nm in snames]
    scratch += [pltpu.VMEM((RWKV_HEADS // RWKV_GROUP, RWKV_GROUP * RWKV_HEAD, RWKV_GROUP * RWKV_HEAD), F32),
                pltpu.VMEM((MLSTM_HEADS, MLSTM_HEAD, 2 * MLSTM_HEAD), F32),
                pltpu.VMEM((MLSTM_HEADS, 8, 128), F32)]
    return pl.pallas_call(
        functools.partial(_mixer_kernel, has_vres=has_vres, nt=nt, wnames=wnames, snames=snames),
        grid=(n + 2,),
        in_specs=in_specs,
        out_specs=out_specs,
        out_shape=out_shape,
        scratch_shapes=scratch,
        compiler_params=pltpu.CompilerParams(dimension_semantics=("arbitrary",),
                                             vmem_limit_bytes=VMEM_LIMIT),
        name="mixer",
    )(*([h3] + ([vfirst] if has_vres else []) + weights))


def _pad_rows(w, rows, at=0):
    out = jnp.zeros((rows, w.shape[1]), w.dtype)
    return out.at[at:at + w.shape[0]].set(w)


def _ffn_params(norm, w_in, w_out):
    nc = D_FF // FF_CHUNK
    wg = w_in[:, :D_FF].reshape(D_MODEL, nc, FF_CHUNK).transpose(1, 0, 2).astype(BF16)
    wu = w_in[:, D_FF:].reshape(D_MODEL, nc, FF_CHUNK).transpose(1, 0, 2).astype(BF16)
    wo = w_out.reshape(nc, FF_CHUNK, D_MODEL).astype(BF16)
    return norm.reshape(1, D_MODEL), wg, wu, wo


def _mixer_params(l, mix_norm, w_in, shift_mu, rw_w0, rw_w_up, rw_a0, rw_a_up, rw_g_up, rw_k_k, rw_k_a,
                  rw_r_k, vres_down, vres_up, vres_bias, ml_conv_w, ml_conv_b, ml_i_bias, ml_f_bias):
    w = w_in[l]
    o_ml = N_RW_IN
    o_gate = N_RW_IN + N_ML_IN
    w_rkv = w[:, :3 * D_RWKV]
    w_lora = w[:, 3 * D_RWKV:N_RW_IN]
    w_if = w[:, o_ml + 4 * D_MLSTM:o_gate]
    w_vd = vres_down[l - 1] if l > 0 else jnp.zeros((D_MODEL, LORA_V), F32)
    pad = jnp.zeros((D_MODEL, SM_W - SM_VRES - LORA_V), F32)
    w_sm = jnp.concatenate([w_lora, w_if, w_vd, pad], axis=1)
    mu = shift_mu[l]
    mu_sm = jnp.concatenate([mu[3 * D_RWKV:], jnp.zeros((SM_W - (N_RW_IN - 3 * D_RWKV),), F32)])
    row = lambda x: x.reshape(1, -1)
    p = {
        "mix_norm": row(mix_norm[l]),
        "w_rkv": w_rkv.astype(BF16),
        "w_sm": w_sm.astype(BF16),
        "w_ml": w[:, o_ml:o_ml + 4 * D_MLSTM].astype(BF16),
        "w_gate": w[:, o_gate:].astype(BF16),
        "mu_rkv": row(mu[:3 * D_RWKV]),
        "mu_sm": row(mu_sm),
        "w0": row(rw_w0[l]),
        "wup": _pad_rows(rw_w_up[l], 128, 0).astype(BF16),
        "a0": row(rw_a0[l]),
        "aup": _pad_rows(rw_a_up[l], 128, LORA_W).astype(BF16),
        "gup": _pad_rows(rw_g_up[l], 256, 0).astype(BF16),
        "k_k": row(rw_k_k[l]),
        "k_a": row(rw_k_a[l]),
        "r_k": row(rw_r_k[l]),
        "vbias": row(vres_bias[l - 1]) if l > 0 else jnp.zeros((1, D_RWKV), F32),
        "vup": (_pad_rows(vres_up[l - 1], 256, SM_VRES - 128) if l > 0
                else jnp.zeros((256, D_RWKV), F32)).astype(BF16),
        "convw": ml_conv_w[l],
        "convb": row(ml_conv_b[l]),
    }
    gate_bias = jnp.concatenate([ml_i_bias[l], ml_f_bias[l]])
    p["if_brow"] = jnp.zeros((1, 128), F32).at[0, IF_LANE:IF_LANE + 8].set(gate_bias)
    p["if_bcol"] = gate_bias.reshape(8, 1)
    return p


def kernel(x, ffn1_norm, ffn1_w_in, ffn1_w_out, mix_norm, w_in, shift_mu, rw_w0, rw_w_up, rw_a0, rw_a_up, rw_g_up, rw_k_k, rw_k_a, rw_r_k, rw_gn_w, rw_gn_b, vres_down, vres_up, vres_bias, ml_conv_w, ml_conv_b, ml_i_bias, ml_f_bias, ml_norm_w, br_a, br_b, w_out, ffn2_norm, ffn2_w_in, ffn2_w_out, final_norm):
    B, T, D = x.shape
    n = B * T
    assert D == D_MODEL and T % min(TT_MIX, T) == 0 and min(TT_MIX, T) % CHUNK == 0
    assert n % min(TM_FFN, n) == 0 and n % min(TM_MERGE, n) == 0
    lane = jnp.arange(256) // RWKV_HEAD
    seg = (lane[:, None] == lane[None, :]).astype(BF16)
    fin = final_norm.reshape(1, D_MODEL)
    h = x.reshape(n, D)
    v_first = None
    for l in range(DEPTH):
        h = _ffn_call(h, *_ffn_params(ffn1_norm[l], ffn1_w_in[l], ffn1_w_out[l]))
        p = _mixer_params(l, mix_norm, w_in, shift_mu, rw_w0, rw_w_up, rw_a0, rw_a_up, rw_g_up, rw_k_k,
                          rw_k_a, rw_r_k, vres_down, vres_up, vres_bias, ml_conv_w, ml_conv_b,
                          ml_i_bias, ml_f_bias)
        p.update(seg=seg, gnw=rw_gn_w[l].reshape(1, -1), gnb=rw_gn_b[l].reshape(1, -1),
                 ml_nw=ml_norm_w[l].reshape(1, -1))
        outs = _mixer_call(h.reshape(B, T, D), v_first, p)
        ya, yb, sg = outs[:3]
        if l == 0:
            v_first = outs[3]
        h = _merge_ffn_call(h, ya.reshape(n, -1), yb.reshape(n, -1), sg.reshape(n, -1),
                            br_a[l].astype(BF16), br_b[l].astype(BF16), w_out[l].astype(BF16),
                            *_ffn_params(ffn2_norm[l], ffn2_w_in[l], ffn2_w_out[l]), fin,
                            final=(l == DEPTH - 1))
    return h.reshape(B, T, D)
```

```python
import functools

import jax
import jax.numpy as jnp
from jax import lax
from jax.experimental import pallas as pl
from jax.experimental.pallas import tpu as pltpu

F32 = jnp.float32
BF16 = jnp.bfloat16

D_MODEL = 1024
DEPTH = 4
CHUNK = 64
D_RWKV = 512
RWKV_HEAD = 64
RWKV_HEADS = 8
RWKV_GROUP = 2
LORA_W, LORA_A, LORA_V, LORA_G = 64, 64, 32, 160
RWKV_GN_EPS = 64e-5
D_MLSTM = 512
MLSTM_HEADS = 4
MLSTM_HEAD = 128
CONV_W = 4
MLSTM_NORM_EPS = 1e-5
D_FF = 2816
NORM_EPS = 1e-6
N_RW_IN = 3 * D_RWKV + LORA_W + LORA_A + LORA_G
N_ML_IN = 4 * D_MLSTM + 2 * MLSTM_HEADS

SM_W = 384
SM_IF = 288
SM_VRES = 296
IF_LANE = SM_IF - 256

FF_CHUNK = 256
TM_FFN = 1024
TM_MERGE = 512
TT_MIX = 256
STAGE_ROW0 = 16
VMEM_LIMIT = 56 * 1024 * 1024


def _const_spec(shape):
    nd = len(shape)
    return pl.BlockSpec(shape, lambda *_: (0,) * nd, pipeline_mode=pl.Buffered(1))


def _rms(h, g):
    return h * lax.rsqrt(jnp.mean(h * h, axis=-1, keepdims=True) + NORM_EPS) * g


def _sigmoid(x):
    return 0.5 * jnp.tanh(0.5 * x) + 0.5


def _dot(a, b):
    return jnp.dot(a, b, preferred_element_type=F32)


def _seg_sum(x, seg_ref):
    xb = x.astype(BF16)
    gw = seg_ref.shape[0]
    return jnp.concatenate([_dot(xb[:, lo:lo + gw], seg_ref[...]) for lo in range(0, x.shape[1], gw)], axis=1)


def _ffn_body(h, g_ref, wg_ref, wu_ref, wo_ref, acc_ref):
    xb = _rms(h, g_ref[...]).astype(BF16)
    nc = D_FF // FF_CHUNK
    for c in range(nc):
        gate = _dot(xb, wg_ref[c])
        up = _dot(xb, wu_ref[c])
        act = (gate * _sigmoid(gate) * up).astype(BF16)
        part = _dot(act, wo_ref[c])
        if c == 0:
            acc_ref[...] = part
        elif c < nc - 1:
            acc_ref[...] += part
    return h + 0.5 * (acc_ref[...] + part)


def _ffn_kernel(h_ref, g_ref, wg_ref, wu_ref, wo_ref, out_ref, acc_ref):
    out_ref[...] = _ffn_body(h_ref[...], g_ref, wg_ref, wu_ref, wo_ref, acc_ref)


def _merge_ffn_kernel(h_ref, ya_ref, yb_ref, sg_ref, bra_ref, brb_ref, wout_ref,
                      g_ref, wg_ref, wu_ref, wo_ref, fin_ref, out_ref, acc_ref, *, final):
    sg = sg_ref[...]
    pa = _dot(ya_ref[...].astype(BF16), bra_ref[...])
    pb = _dot(yb_ref[...].astype(BF16), brb_ref[...])
    u = sg[:, :D_MODEL] * pa + sg[:, D_MODEL:] * pb
    h = h_ref[...] + _dot(u.astype(BF16), wout_ref[...])
    h = _ffn_body(h, g_ref, wg_ref, wu_ref, wo_ref, acc_ref)
    if final:
        h = _rms(h, fin_ref[...])
    out_ref[...] = h


def _ffn_weight_specs():
    nc = D_FF // FF_CHUNK
    return [_const_spec((1, D_MODEL)), _const_spec((nc, D_MODEL, FF_CHUNK)),
            _const_spec((nc, D_MODEL, FF_CHUNK)), _const_spec((nc, FF_CHUNK, D_MODEL))]


def _ffn_call(h, g, wg, wu, wo):
    n = h.shape[0]
    tm = min(TM_FFN, n)
    row = pl.BlockSpec((tm, D_MODEL), lambda i: (i, 0))
    return pl.pallas_call(
        _ffn_kernel,
        grid=(n // tm,),
        in_specs=[row] + _ffn_weight_specs(),
        out_specs=row,
        out_shape=jax.ShapeDtypeStruct((n, D_MODEL), F32),
        scratch_shapes=[pltpu.VMEM((tm, D_MODEL), F32)],
        compiler_params=pltpu.CompilerParams(dimension_semantics=("arbitrary",),
                                             vmem_limit_bytes=VMEM_LIMIT),
        name="ffn",
    )(h, g, wg, wu, wo)


def _merge_ffn_call(h, ya, yb, sg, bra, brb, wout, g, wg, wu, wo, fin, final):
    n = h.shape[0]
    tm = min(TM_MERGE, n)
    row = lambda w: pl.BlockSpec((tm, w), lambda i: (i, 0))
    return pl.pallas_call(
        functools.partial(_merge_ffn_kernel, final=final),
        grid=(n // tm,),
        in_specs=[row(D_MODEL), row(D_RWKV), row(D_MLSTM), row(2 * D_MODEL),
                  _const_spec((D_RWKV, D_MODEL)), _const_spec((D_MLSTM, D_MODEL)),
                  _const_spec((D_MODEL, D_MODEL))] + _ffn_weight_specs() + [_const_spec((1, D_MODEL))],
        out_specs=row(D_MODEL),
        out_shape=jax.ShapeDtypeStruct((n, D_MODEL), F32),
        scratch_shapes=[pltpu.VMEM((tm, D_MODEL), F32)],
        compiler_params=pltpu.CompilerParams(dimension_semantics=("arbitrary",),
                                             vmem_limit_bytes=VMEM_LIMIT),
        name="merge_ffn",
    )(h, ya, yb, sg, bra, brb, wout, g, wg, wu, wo, fin)


def _bdot(a, b):
    return jnp.dot(a.astype(BF16), b.astype(BF16), preferred_element_type=F32)


def _bdot_nt(a, b):
    return lax.dot_general(a.astype(BF16), b.astype(BF16), (((1,), (1,)), ((), ())),
                           preferred_element_type=F32)


def _bdot_tn(a, b):
    return lax.dot_general(a.astype(BF16), b.astype(BF16), (((0,), (0,)), ((), ())),
                           preferred_element_type=F32)


def _split3(x):
    hi = x.astype(BF16)
    r1 = x - hi.astype(F32)
    mid = r1.astype(BF16)
    lo = (r1 - mid.astype(F32)).astype(BF16)
    return hi, mid, lo


def _log_sigmoid(x):
    return jnp.minimum(x, 0.0) - jnp.log(1.0 + jnp.exp(-jnp.abs(x)))


def _prep_stages(zr, zs, zm, vfirst_ref, vfirst_o, w, s, xb_ref, sg_o, tt):
    D = D_RWKV
    R0 = STAGE_ROW0

    def shifted(buf, lo, hi, mu):
        z = buf[R0:R0 + tt, lo:hi]
        return z + mu * (buf[R0 - 1:R0 - 1 + tt, lo:hi] - z)

    def refill(buf, w_ref, lo, hi):
        buf[R0 - 8:R0, lo:hi] = buf[R0 + tt - 8:R0 + tt, lo:hi]
        buf[R0:R0 + tt, lo:hi] = _dot(xb_ref[...], w_ref[:, lo:hi])

    z_sm = zs[R0:R0 + tt, :]
    m_sm = shifted(zs, 0, SM_W, w["mu_sm"][...])
    s["ifb"][...] = z_sm[:, 256:]
    xwa = m_sm[:, :128]
    y = -(w["w0"][...] + _dot(jnp.tanh(xwa).astype(BF16), w["wup"][...]))
    softplus = jnp.maximum(y, 0.0) + jnp.log(1.0 + jnp.exp(-jnp.abs(y)))
    s["ld"][...] = -jnp.exp(-softplus - 0.5)
    a = _sigmoid(w["a0"][...] + _dot(xwa.astype(BF16), w["aup"][...]))
    s["a"][...] = a
    s["g"][...] = _dot(_sigmoid(m_sm[:, 128:]).astype(BF16), w["gup"][...])
    if vfirst_ref is not None:
        vg = _sigmoid(w["vbias"][...] + _dot(z_sm[:, 128:].astype(BF16), w["vup"][...]))
    refill(zs, w["w_sm"], 0, SM_W)
    yield

    mu = w["mu_rkv"]
    k = shifted(zr, D, 2 * D, mu[:, D:2 * D])
    refill(zr, w["w_rkv"], D, 2 * D)
    kk = k * w["k_k"][...]
    s["kk"][...] = kk * lax.rsqrt(jnp.maximum(_seg_sum(kk * kk, w["seg"]), 1e-24))
    k2 = k * (1.0 + (a - 1.0) * w["k_a"][...])
    s["k"][...] = k2
    yield

    v = shifted(zr, 2 * D, 3 * D, mu[:, 2 * D:])
    refill(zr, w["w_rkv"], 2 * D, 3 * D)
    if vfirst_ref is not None:
        v = v + (vfirst_ref[...] - v) * vg
    s["v"][...] = v
    if vfirst_o is not None:
        vfirst_o[...] = v
    yield

    r = shifted(zr, 0, D, mu[:, :D])
    refill(zr, w["w_rkv"], 0, D)
    s["r"][...] = r
    s["bonus"][...] = _seg_sum(r * k2 * w["r_k"][...], w["seg"]) * v
    yield

    cw = w["convw"]
    for name, lo, scale in (("q", 0, MLSTM_HEAD ** -0.5), ("mk", D_MLSTM, 1.0)):
        hi = lo + D_MLSTM
        conv = w["convb"][:, lo:hi] + cw[CONV_W - 1:CONV_W, lo:hi] * zm[R0:R0 + tt, lo:hi]
        for sft in range(1, CONV_W):
            conv = conv + cw[CONV_W - 1 - sft:CONV_W - sft, lo:hi] * zm[R0 - sft:R0 - sft + tt, lo:hi]
        refill(zm, w["w_ml"], lo, hi)
        s[name][...] = conv * _sigmoid(conv) * scale
        yield
    s["mv"][...] = zm[R0:R0 + tt, 2 * D_MLSTM:3 * D_MLSTM]
    refill(zm, w["w_ml"], 2 * D_MLSTM, 3 * D_MLSTM)
    yield
    s["o"][...] = _sigmoid(zm[R0:R0 + tt, 3 * D_MLSTM:])
    refill(zm, w["w_ml"], 3 * D_MLSTM, 4 * D_MLSTM)
    yield

    for lo in range(0, 2 * D_MODEL, D_MLSTM):
        sg_o[:, lo:lo + D_MLSTM] = _sigmoid(_dot(xb_ref[...], w["w_gate"][:, lo:lo + D_MLSTM]))
        yield


def _rwkv_stages(s, w, y_ref, h_ref):
    C, N = CHUNK, RWKV_HEAD
    tt = y_ref.shape[0]

    tr = lax.broadcasted_iota(jnp.int32, (tt, tt), 0)
    tc = lax.broadcasted_iota(jnp.int32, (tt, tt), 1)
    same = (tr // C) == (tc // C)
    tri = (same & (tc <= tr)).astype(BF16)
    lw = s["ld"][...]
    kk = s["kk"][...]
    kka = kk * s["a"][...]
    k2 = s["k"][...]
    r_in = s["r"][...]
    V = s["v"][...].astype(BF16)
    bonus = s["bonus"][...]
    g_out = s["g"][...]
    parts = _split3(lw)
    cs = sum(_dot(tri, p) for p in parts)
    nc = tt // C
    yield

    E = jnp.exp(cs)
    g_inv = 1.0 / E
    g_tot = jnp.concatenate([jnp.broadcast_to(E[(c + 1) * C - 1:(c + 1) * C, :], (C, D_RWKV))
                             for c in range(nc)], axis=0)
    g_end = g_tot * g_inv
    trow = lax.broadcasted_iota(jnp.int32, (tt, 1), 0)
    E_prev = jnp.where(trow % C == 0, 1.0, pltpu.roll(E, 1, axis=0))
    Rt = (r_in * E).astype(BF16)
    At = (-kk * E_prev).astype(BF16)
    Kb = (k2 * g_inv).astype(BF16)
    Bb = (kka * g_inv).astype(BF16)
    Ke = (k2 * g_end).astype(BF16)
    Be = (kka * g_end).astype(BF16)

    G = RWKV_GROUP
    GW = G * N
    nq = D_RWKV // GW
    lane = lax.broadcasted_iota(jnp.int32, (C, GW), 1)
    row = lax.broadcasted_iota(jnp.int32, (C, GW), 0)
    head_of = [lane // N == h for h in range(G)]
    strict_g = (lane % N) < row
    incl_g = (lane % N) <= row
    eye_g = ((lane % N) == row).astype(F32)
    br = lax.broadcasted_iota(jnp.int32, (GW, GW), 0)
    bc = lax.broadcasted_iota(jnp.int32, (GW, GW), 1)
    bd_mask = (br // N) == (bc // N)
    eye_q = (br == bc).astype(F32)

    def bd(x):
        x = x.astype(BF16)
        return jnp.concatenate([jnp.where(m, x, jnp.zeros_like(x)) for m in head_of], axis=0)

    inst = [(c, q) for c in range(nc) for q in range(nq)]
    sl = lambda x, c, q: x[c * C:(c + 1) * C, q * GW:(q + 1) * GW]
    each = lambda f, *lists: [f(*xs) for xs in zip(*lists)]

    At_i = [sl(At, c, q) for c, q in inst]
    Rt_i = [sl(Rt, c, q) for c, q in inst]
    V_i = [sl(V, c, q) for c, q in inst]
    P = [_bdot_nt(jnp.concatenate([a, r], axis=0),
                  jnp.concatenate([bd(sl(Kb, c, q)), bd(sl(Bb, c, q))], axis=0))
         for a, r, (c, q) in zip(At_i, Rt_i, inst)]
    yield
    A_ak = [jnp.where(strict_g, p[:C, :GW], 0.0).astype(BF16) for p in P]
    L = [jnp.where(strict_g, p[:C, GW:], 0.0).astype(BF16) for p in P]
    A_rk = [jnp.where(incl_g, p[C:, :GW], 0.0).astype(BF16) for p in P]
    A_rb = [jnp.where(incl_g, p[C:, GW:], 0.0).astype(BF16) for p in P]
    AV = each(lambda ak, rk, v: _bdot(jnp.concatenate([ak, rk], axis=0), bd(v)), A_ak, A_rk, V_i)
    yield
    X = [eye_g + l for l in L]
    M = each(lambda l: _bdot(l, bd(l)), L)
    yield
    for _ in range(4):
        XM = each(lambda x, m: _bdot(jnp.concatenate([x, m], axis=0), bd(m)), X, M)
        yield
        X = each(lambda x, xm: x + xm[:C], X, XM)
        M = [xm[C:] for xm in XM]
    Tm = each(lambda x, m: x + _bdot(x, bd(m)), X, M)
    yield
    W = each(lambda t, a, av: _bdot(t, jnp.concatenate([bd(a), bd(av[:C])], axis=1)).astype(BF16),
             Tm, At_i, AV)
    yield
    Z = each(lambda rb, w_: _bdot(rb, jnp.concatenate([bd(w_[:, :GW]), bd(w_[:, GW:])], axis=1)), A_rb, W)
    yield
    GA = [_bdot_tn(sl(Be, c, q), w_[:, :GW]) for (c, q), w_ in zip(inst, W)]
    G0 = [jnp.where(bd_mask, _bdot_tn(jnp.concatenate([sl(Ke, c, q), sl(Be, c, q)], axis=0),
                                      jnp.concatenate([v, w_[:, GW:]], axis=0)), 0.0)
          for (c, q), v, w_ in zip(inst, V_i, W)]
    yield
    RM = [jnp.concatenate([(r.astype(F32) + z[:, :GW]).astype(BF16),
                           (jnp.where(bd_mask, ga, 0.0)
                            + eye_q * g_tot[c * C:c * C + 1, q * GW:(q + 1) * GW]).astype(BF16)], axis=0)
          for r, z, ga, (c, q) in zip(Rt_i, Z, GA, inst)]
    Y0 = each(lambda av, z: av[C:] + z[:, GW:], AV, Z)

    Hs = [h_ref[q] for q in range(nq)]
    y_rows = []
    for c in range(nc):
        RMH = [_bdot(RM[c * nq + q], Hs[q]) for q in range(nq)]
        yield
        y_rows.append(jnp.concatenate([RMH[q][:C] + Y0[c * nq + q] for q in range(nq)], axis=1))
        Hs = [RMH[q][C:] + G0[c * nq + q] for q in range(nq)]
    for q in range(nq):
        h_ref[q] = Hs[q]
    y = jnp.concatenate(y_rows, axis=0)
    inv_n = 1.0 / N
    mu = _seg_sum(y, w["seg"]) * inv_n
    yield
    d = y - mu
    var = _seg_sum(d * d, w["seg"]) * inv_n
    yield
    yn = d * lax.rsqrt(var + RWKV_GN_EPS)
    y_ref[...] = (yn * w["gnw"][...] + w["gnb"][...] + bonus) * g_out


def _mlstm_stages(s, w, y_ref, cn_ref, m_ref):
    q_ref, k_ref, v_ref, o_ref, if_ref = s["q"], s["mk"], s["mv"], s["o"], s["ifb"]
    brow_ref, bcol_ref, nw_ref = w["if_brow"], w["if_bcol"], w["ml_nw"]
    C, d = CHUNK, MLSTM_HEAD
    tt = q_ref.shape[0]
    nc = tt // C
    NH = MLSTM_HEADS
    ti = lax.broadcasted_iota(jnp.int32, (C, C), 0)
    si = lax.broadcasted_iota(jnp.int32, (C, C), 1)
    causal = si <= ti
    ones = jnp.ones((C, d), BF16)
    i0 = IF_LANE
    f0 = IF_LANE + NH

    tr = lax.broadcasted_iota(jnp.int32, (tt, tt), 0)
    tc = lax.broadcasted_iota(jnp.int32, (tt, tt), 1)
    same = (tr // C) == (tc // C)
    tril = (same & (tc <= tr)).astype(BF16)
    triu = (same & (tr <= tc)).astype(BF16)
    pre = if_ref[...] + brow_ref[...]
    bcum_cols = sum(_dot(tril, p) for p in _split3(_log_sigmoid(pre)))
    preT = jnp.transpose(if_ref[...])[i0:i0 + 8, :] + bcol_ref[...]
    row8 = lax.broadcasted_iota(jnp.int32, (8, 1), 0)
    bcum_rows = sum(_dot(p, triu) for p in _split3(jnp.where(row8 < NH, preT, _log_sigmoid(preT))))
    inst = [(c, h) for c in range(nc) for h in range(NH)]
    rs = lambda c: slice(c * C, (c + 1) * C)
    hs_ = lambda h: slice(h * d, (h + 1) * d)
    q_i = [q_ref[rs(c), hs_(h)].astype(BF16) for c, h in inst]
    k_i = [k_ref[rs(c), hs_(h)] for c, h in inst]
    ve_i = [jnp.concatenate([v_ref[rs(c), hs_(h)].astype(BF16), ones], axis=1) for c, h in inst]
    o_i = [o_ref[rs(c), hs_(h)] for c, h in inst]
    yield

    bc_i = [bcum_cols[rs(c), f0 + h:f0 + h + 1] for c, h in inst]
    e_c = [pre[rs(c), i0 + h:i0 + h + 1] - bc for (c, h), bc in zip(inst, bc_i)]
    e_r = [preT[h:h + 1, rs(c)] - bcum_rows[NH + h:NH + h + 1, rs(c)] for c, h in inst]
    a_col = [jnp.max(jnp.where(causal, e, -jnp.inf), axis=-1, keepdims=True) for e in e_r]
    yield
    D0 = [jnp.where(causal, jnp.exp(e - a), 0.0) for e, a in zip(e_r, a_col)]
    QK = [_bdot_nt(q, k) for q, k in zip(q_i, k_i)]
    yield
    S0 = [(qk * dm).astype(BF16) for qk, dm in zip(QK, D0)]
    a_end = [a[C - 1:C, :] for a in a_col]
    KV0 = [_bdot_tn(jnp.exp(e - ae) * k, ve) for e, ae, k, ve in zip(e_c, a_end, k_i, ve_i)]
    yield
    intra = [_dot(s0, ve) for s0, ve in zip(S0, ve_i)]
    yield

    cn = [cn_ref[h] for h in range(NH)]
    m_prev = [m_ref[h][0:1, 0:1] for h in range(NH)]
    hh = []
    for c in range(nc):
        ii = [c * NH + h for h in range(NH)]
        QC = [_bdot(q_i[i], cn[h]) for h, i in enumerate(ii)]
        yield
        for h, i in enumerate(ii):
            mu_t = jnp.maximum(m_prev[h], a_col[i])
            tot = jnp.exp(m_prev[h] - mu_t) * QC[h] + jnp.exp(a_col[i] - mu_t) * intra[i]
            hh.append(tot[:, :d] / jnp.maximum(jnp.abs(tot[:, d:]), jnp.exp(-(bc_i[i] + mu_t))))
            mu_end = jnp.maximum(m_prev[h], a_end[i])
            cn[h] = jnp.exp(m_prev[h] - mu_end) * cn[h] + jnp.exp(a_end[i] - mu_end) * KV0[i]
            m_prev[h] = bc_i[i][C - 1:C, :] + mu_end
    for h in range(NH):
        cn_ref[h] = cn[h]
        m_ref[h] = jnp.broadcast_to(m_prev[h], (8, 128))
    mean = [jnp.mean(x, axis=-1, keepdims=True) for x in hh]
    yield
    dd = [x - m for x, m in zip(hh, mean)]
    var = [jnp.mean(x * x, axis=-1, keepdims=True) for x in dd]
    yield
    for (c, h), x, v, o in zip(inst, dd, var, o_i):
        y_ref[rs(c), hs_(h)] = o * (x * lax.rsqrt(v + MLSTM_NORM_EPS) * nw_ref[:, hs_(h)])


def _run_interleaved(streams, head, every):
    for k in head:
        next(streams[k], None)
    live = dict(enumerate(streams))
    rnd = 0
    while live:
        for k in [k for k in live if rnd % every[k] == 0]:
            if next(live[k], StopIteration) is StopIteration:
                del live[k]
        rnd += 1


def _mixer_kernel(*refs, has_vres, nt, wnames, snames):
    refs = list(refs)
    h_ref = refs.pop(0)
    vfirst_ref = refs.pop(0) if has_vres else None
    w = {n: refs.pop(0) for n in wnames}
    ya_o, yb_o, sg_o = refs.pop(0), refs.pop(0), refs.pop(0)
    vfirst_o = None if has_vres else refs.pop(0)
    zr, zs, zm, xb_ref = (refs.pop(0) for _ in range(4))
    s = {n: refs.pop(0) for n in snames}
    h_state, cn_state, m_state = refs
    tt = h_ref.shape[0]
    i = pl.program_id(0)

    @pl.when(i == 0)
    def _():
        for ref in (zr, zs, zm, h_state, cn_state, m_state) + tuple(s.values()):
            ref[...] = jnp.zeros_like(ref)

    @pl.when(lax.rem(i + nt - 1, nt) == 0)
    def _():
        for ref in (zr, zs, zm):
            ref[:STAGE_ROW0, :] = jnp.zeros((STAGE_ROW0, ref.shape[1]), F32)

    @pl.when(lax.rem(i + 2 * nt - 2, nt) == 0)
    def _():
        h_state[...] = jnp.zeros_like(h_state)
        cn_state[...] = jnp.zeros_like(cn_state)
        m_state[...] = jnp.zeros_like(m_state)

    xb_ref[...] = _rms(h_ref[...], w["mix_norm"][...]).astype(BF16)
    streams = [_rwkv_stages(s, w, ya_o, h_state), _mlstm_stages(s, w, yb_o, cn_state, m_state),
               _prep_stages(zr, zs, zm, vfirst_ref, vfirst_o, w, s, xb_ref, sg_o, tt)]
    _run_interleaved(streams, head=(0, 1), every=(1, 3, 2))


def _mixer_call(h3, vfirst, p):
    B, T, _ = h3.shape
    tt = min(TT_MIX, T)
    nt = T // tt
    n = B * nt
    has_vres = vfirst is not None
    cur = lambda wd: pl.BlockSpec((None, tt, wd), lambda i: (jnp.minimum(i, n - 1) // nt,
                                                            jnp.minimum(i, n - 1) % nt, 0))
    def lag(wd, by):
        tile = lambda i: jnp.clip(i - by, 0, n - 1)
        return pl.BlockSpec((None, tt, wd), lambda i: (tile(i) // nt, tile(i) % nt, 0))
    wnames = ("mix_norm", "w_rkv", "w_sm", "w_ml", "w_gate", "mu_rkv", "mu_sm", "w0", "wup", "a0", "aup",
              "gup", "k_k", "k_a", "r_k", "vbias", "vup", "convw", "convb", "seg", "gnw", "gnb",
              "if_brow", "if_bcol", "ml_nw")
    snames = ("r", "k", "v", "ld", "kk", "a", "g", "bonus", "q", "mk", "mv", "o", "ifb")
    weights = [p[nm] for nm in wnames]
    vf_spec = lambda last: pl.BlockSpec((None, tt, D_RWKV), lambda i: (jnp.clip(i - 1, 0, last), 0, 0))
    in_specs = [cur(D_MODEL)] + ([vf_spec(n - 1)] if has_vres else []) + [_const_spec(x.shape) for x in weights]
    out_specs = [lag(D_RWKV, 2), lag(D_MLSTM, 2), cur(2 * D_MODEL)] + ([] if has_vres else [vf_spec(n)])
    out_shape = [jax.ShapeDtypeStruct((B, T, wd), F32) for wd in (D_RWKV, D_MLSTM, 2 * D_MODEL)]
    if not has_vres:
        out_shape.append(jax.ShapeDtypeStruct((n + 1, tt, D_RWKV), F32))
    stage_rows = tt + 32
    scratch = [pltpu.VMEM((stage_rows, 3 * D_RWKV), F32), pltpu.VMEM((stage_rows, SM_W), F32),
               pltpu.VMEM((stage_rows, 4 * D_MLSTM), F32), pltpu.VMEM((tt, D_MODEL), BF16)]
    scratch += [pltpu.VMEM((tt, 128 if nm == "ifb" else D_RWKV), F32) for nm in snames]
    scratch += [pltpu.VMEM((RWKV_HEADS // RWKV_GROUP, RWKV_GROUP * RWKV_HEAD, RWKV_GROUP * RWKV_HEAD), F32),
                pltpu.VMEM((MLSTM_HEADS, MLSTM_HEAD, 2 * MLSTM_HEAD), F32),
                pltpu.VMEM((MLSTM_HEADS, 8, 128), F32)]
    return pl.pallas_call(
        functools.partial(_mixer_kernel, has_vres=has_vres, nt=nt, wnames=wnames, snames=snames),
        grid=(n + 2,),
        in_specs=in_specs,
        out_specs=out_specs,
        out_shape=out_shape,
        scratch_shapes=scratch,
        compiler_params=pltpu.CompilerParams(dimension_semantics=("arbitrary",),
                                             vmem_limit_bytes=VMEM_LIMIT),
        name="mixer",
    )(*([h3] + ([vfirst] if has_vres else []) + weights))


def _pad_rows(w, rows, at=0):
    out = jnp.zeros((rows, w.shape[1]), w.dtype)
    return out.at[at:at + w.shape[0]].set(w)


def _ffn_params(norm, w_in, w_out):
    nc = D_FF // FF_CHUNK
    wg = w_in[:, :D_FF].reshape(D_MODEL, nc, FF_CHUNK).transpose(1, 0, 2).astype(BF16)
    wu = w_in[:, D_FF:].reshape(D_MODEL, nc, FF_CHUNK).transpose(1, 0, 2).astype(BF16)
    wo = w_out.reshape(nc, FF_CHUNK, D_MODEL).astype(BF16)
    return norm.reshape(1, D_MODEL), wg, wu, wo


def _mixer_params(l, mix_norm, w_in, shift_mu, rw_w0, rw_w_up, rw_a0, rw_a_up, rw_g_up, rw_k_k, rw_k_a,
                  rw_r_k, vres_down, vres_up, vres_bias, ml_conv_w, ml_conv_b, ml_i_bias, ml_f_bias):
    w = w_in[l]
    o_ml = N_RW_IN
    o_gate = N_RW_IN + N_ML_IN
    w_rkv = w[:, :3 * D_RWKV]
    w_lora = w[:, 3 * D_RWKV:N_RW_IN]
    w_if = w[:, o_ml + 4 * D_MLSTM:o_gate]
    w_vd = vres_down[l - 1] if l > 0 else jnp.zeros((D_MODEL, LORA_V), F32)
    pad = jnp.zeros((D_MODEL, SM_W - SM_VRES - LORA_V), F32)
    w_sm = jnp.concatenate([w_lora, w_if, w_vd, pad], axis=1)
    mu = shift_mu[l]
    mu_sm = jnp.concatenate([mu[3 * D_RWKV:], jnp.zeros((SM_W - (N_RW_IN - 3 * D_RWKV),), F32)])
    row = lambda x: x.reshape(1, -1)
    p = {
        "mix_norm": row(mix_norm[l]),
        "w_rkv": w_rkv.astype(BF16),
        "w_sm": w_sm.astype(BF16),
        "w_ml": w[:, o_ml:o_ml + 4 * D_MLSTM].astype(BF16),
        "w_gate": w[:, o_gate:].astype(BF16),
        "mu_rkv": row(mu[:3 * D_RWKV]),
        "mu_sm": row(mu_sm),
        "w0": row(rw_w0[l]),
        "wup": _pad_rows(rw_w_up[l], 128, 0).astype(BF16),
        "a0": row(rw_a0[l]),
        "aup": _pad_rows(rw_a_up[l], 128, LORA_W).astype(BF16),
        "gup": _pad_rows(rw_g_up[l], 256, 0).astype(BF16),
        "k_k": row(rw_k_k[l]),
        "k_a": row(rw_k_a[l]),
        "r_k": row(rw_r_k[l]),
        "vbias": row(vres_bias[l - 1]) if l > 0 else jnp.zeros((1, D_RWKV), F32),
        "vup": (_pad_rows(vres_up[l - 1], 256, SM_VRES - 128) if l > 0
                else jnp.zeros((256, D_RWKV), F32)).astype(BF16),
        "convw": ml_conv_w[l],
        "convb": row(ml_conv_b[l]),
    }
    gate_bias = jnp.concatenate([ml_i_bias[l], ml_f_bias[l]])
    p["if_brow"] = jnp.zeros((1, 128), F32).at[0, IF_LANE:IF_LANE + 8].set(gate_bias)
    p["if_bcol"] = gate_bias.reshape(8, 1)
    return p


def kernel(x, ffn1_norm, ffn1_w_in, ffn1_w_out, mix_norm, w_in, shift_mu, rw_w0, rw_w_up, rw_a0, rw_a_up, rw_g_up, rw_k_k, rw_k_a, rw_r_k, rw_gn_w, rw_gn_b, vres_down, vres_up, vres_bias, ml_conv_w, ml_conv_b, ml_i_bias, ml_f_bias, ml_norm_w, br_a, br_b, w_out, ffn2_norm, ffn2_w_in, ffn2_w_out, final_norm):
    B, T, D = x.shape
    n = B * T
    assert D == D_MODEL and T % min(TT_MIX, T) == 0 and min(TT_MIX, T) % CHUNK == 0
    assert n % min(TM_FFN, n) == 0 and n % min(TM_MERGE, n) == 0
    lane = jnp.arange(256) // RWKV_HEAD
    seg = (lane[:, None] == lane[None, :]).astype(BF16)
    fin = final_norm.reshape(1, D_MODEL)
    h = x.reshape(n, D)
    v_first = None
    for l in range(DEPTH):
        h = _ffn_call(h, *_ffn_params(ffn1_norm[l], ffn1_w_in[l], ffn1_w_out[l]))
        p = _mixer_params(l, mix_norm, w_in, shift_mu, rw_w0, rw_w_up, rw_a0, rw_a_up, rw_g_up, rw_k_k,
                          rw_k_a, rw_r_k, vres_down, vres_up, vres_bias, ml_conv_w, ml_conv_b,
                          ml_i_bias, ml_f_bias)
        p.update(seg=seg, gnw=rw_gn_w[l].reshape(1, -1), gnb=rw_gn_b[l].reshape(1, -1),
                 ml_nw=ml_norm_w[l].reshape(1, -1))
        outs = _mixer_call(h.reshape(B, T, D), v_first, p)
        ya, yb, sg = outs[:3]
        if l == 0:
            v_first = outs[3]
        h = _merge_ffn_call(h, ya.reshape(n, -1), yb.reshape(n, -1), sg.reshape(n, -1),
                            br_a[l].astype(BF16), br_b[l].astype(BF16), w_out[l].astype(BF16),
                            *_ffn_params(ffn2_norm[l], ffn2_w_in[l], ffn2_w_out[l]), fin,
                            final=(l == DEPTH - 1))
    return h.reshape(B, T, D)
```

```python
import functools

import jax
import jax.numpy as jnp
from jax import lax
from jax.experimental import pallas as pl
from jax.experimental.pallas import tpu as pltpu

F32 = jnp.float32
BF16 = jnp.bfloat16

D_MODEL = 1024
DEPTH = 4
CHUNK = 64
D_RWKV = 512
RWKV_HEAD = 64
RWKV_HEADS = 8
RWKV_GROUP = 2
LORA_W, LORA_A, LORA_V, LORA_G = 64, 64, 32, 160
RWKV_GN_EPS = 64e-5
D_MLSTM = 512
MLSTM_HEADS = 4
MLSTM_HEAD = 128
CONV_W = 4
MLSTM_NORM_EPS = 1e-5
D_FF = 2816
NORM_EPS = 1e-6
N_RW_IN = 3 * D_RWKV + LORA_W + LORA_A + LORA_G
N_ML_IN = 4 * D_MLSTM + 2 * MLSTM_HEADS

SM_W = 384
SM_IF = 288
SM_VRES = 296
IF_LANE = SM_IF - 256

FF_CHUNK = 256
TM_FFN = 1024
TM_MERGE = 512
TT_MIX = 256
STAGE_ROW0 = 16
VMEM_LIMIT = 56 * 1024 * 1024


def _const_spec(shape):
    nd = len(shape)
    return pl.BlockSpec(shape, lambda *_: (0,) * nd, pipeline_mode=pl.Buffered(1))


def _rms(h, g):
    return h * lax.rsqrt(jnp.mean(h * h, axis=-1, keepdims=True) + NORM_EPS) * g


def _sigmoid(x):
    return 0.5 * jnp.tanh(0.5 * x) + 0.5


def _dot(a, b):
    return jnp.dot(a, b, preferred_element_type=F32)


def _seg_sum(x, seg_ref):
    xb = x.astype(BF16)
    gw = seg_ref.shape[0]
    return jnp.concatenate([_dot(xb[:, lo:lo + gw], seg_ref[...]) for lo in range(0, x.shape[1], gw)], axis=1)


def _ffn_body(h, g_ref, wg_ref, wu_ref, wo_ref, acc_ref):
    xb = _rms(h, g_ref[...]).astype(BF16)
    nc = D_FF // FF_CHUNK
    for c in range(nc):
        gate = _dot(xb, wg_ref[c])
        up = _dot(xb, wu_ref[c])
        act = (gate * _sigmoid(gate) * up).astype(BF16)
        part = _dot(act, wo_ref[c])
        if c == 0:
            acc_ref[...] = part
        elif c < nc - 1:
            acc_ref[...] += part
    return h + 0.5 * (acc_ref[...] + part)


def _ffn_kernel(h_ref, g_ref, wg_ref, wu_ref, wo_ref, out_ref, acc_ref):
    out_ref[...] = _ffn_body(h_ref[...], g_ref, wg_ref, wu_ref, wo_ref, acc_ref)


def _merge_ffn_kernel(h_ref, ya_ref, yb_ref, sg_ref, bra_ref, brb_ref, wout_ref,
                      g_ref, wg_ref, wu_ref, wo_ref, fin_ref, out_ref, acc_ref, *, final):
    sg = sg_ref[...]
    pa = _dot(ya_ref[...].astype(BF16), bra_ref[...])
    pb = _dot(yb_ref[...].astype(BF16), brb_ref[...])
    u = sg[:, :D_MODEL] * pa + sg[:, D_MODEL:] * pb
    h = h_ref[...] + _dot(u.astype(BF16), wout_ref[...])
    h = _ffn_body(h, g_ref, wg_ref, wu_ref, wo_ref, acc_ref)
    if final:
        h = _rms(h, fin_ref[...])
    out_ref[...] = h


def _ffn_weight_specs():
    nc = D_FF // FF_CHUNK
    return [_const_spec((1, D_MODEL)), _const_spec((nc, D_MODEL, FF_CHUNK)),
            _const_spec((nc, D_MODEL, FF_CHUNK)), _const_spec((nc, FF_CHUNK, D_MODEL))]


def _ffn_call(h, g, wg, wu, wo):
    n = h.shape[0]
    tm = min(TM_FFN, n)
    row = pl.BlockSpec((tm, D_MODEL), lambda i: (i, 0))
    return pl.pallas_call(
        _ffn_kernel,
        grid=(n // tm,),
        in_specs=[row] + _ffn_weight_specs(),
        out_specs=row,
        out_shape=jax.ShapeDtypeStruct((n, D_MODEL), F32),
        scratch_shapes=[pltpu.VMEM((tm, D_MODEL), F32)],
        compiler_params=pltpu.CompilerParams(dimension_semantics=("arbitrary",),
                                             vmem_limit_bytes=VMEM_LIMIT),
        name="ffn",
    )(h, g, wg, wu, wo)


def _merge_ffn_call(h, ya, yb, sg, bra, brb, wout, g, wg, wu, wo, fin, final):
    n = h.shape[0]
    tm = min(TM_MERGE, n)
    row = lambda w: pl.BlockSpec((tm, w), lambda i: (i, 0))
    return pl.pallas_call(
        functools.partial(_merge_ffn_kernel, final=final),
        grid=(n // tm,),
        in_specs=[row(D_MODEL), row(D_RWKV), row(D_MLSTM), row(2 * D_MODEL),
                  _const_spec((D_RWKV, D_MODEL)), _const_spec((D_MLSTM, D_MODEL)),
                  _const_spec((D_MODEL, D_MODEL))] + _ffn_weight_specs() + [_const_spec((1, D_MODEL))],
        out_specs=row(D_MODEL),
        out_shape=jax.ShapeDtypeStruct((n, D_MODEL), F32),
        scratch_shapes=[pltpu.VMEM((tm, D_MODEL), F32)],
        compiler_params=pltpu.CompilerParams(dimension_semantics=("arbitrary",),
                                             vmem_limit_bytes=VMEM_LIMIT),
        name="merge_ffn",
    )(h, ya, yb, sg, bra, brb, wout, g, wg, wu, wo, fin)


def _bdot(a, b):
    return jnp.dot(a.astype(BF16), b.astype(BF16), preferred_element_type=F32)


def _bdot_nt(a, b):
    return lax.dot_general(a.astype(BF16), b.astype(BF16), (((1,), (1,)), ((), ())),
                           preferred_element_type=F32)


def _bdot_tn(a, b):
    return lax.dot_general(a.astype(BF16), b.astype(BF16), (((0,), (0,)), ((), ())),
                           preferred_element_type=F32)


def _split3(x):
    hi = x.astype(BF16)
    r1 = x - hi.astype(F32)
    mid = r1.astype(BF16)
    lo = (r1 - mid.astype(F32)).astype(BF16)
    return hi, mid, lo


def _log_sigmoid(x):
    return jnp.minimum(x, 0.0) - jnp.log(1.0 + jnp.exp(-jnp.abs(x)))


def _prep_stages(zr, zs, zm, vfirst_ref, vfirst_o, w, s, xb_ref, sg_o, tt):
    D = D_RWKV
    R0 = STAGE_ROW0

    def shifted(buf, lo, hi, mu):
        z = buf[R0:R0 + tt, lo:hi]
        return z + mu * (buf[R0 - 1:R0 - 1 + tt, lo:hi] - z)

    def refill(buf, w_ref, lo, hi):
        buf[R0 - 8:R0, lo:hi] = buf[R0 + tt - 8:R0 + tt, lo:hi]
        buf[R0:R0 + tt, lo:hi] = _dot(xb_ref[...], w_ref[:, lo:hi])

    z_sm = zs[R0:R0 + tt, :]
    m_sm = shifted(zs, 0, SM_W, w["mu_sm"][...])
    s["ifb"][...] = z_sm[:, 256:]
    xwa = m_sm[:, :128]
    y = -(w["w0"][...] + _dot(jnp.tanh(xwa).astype(BF16), w["wup"][...]))
    softplus = jnp.maximum(y, 0.0) + jnp.log(1.0 + jnp.exp(-jnp.abs(y)))
    s["ld"][...] = -jnp.exp(-softplus - 0.5)
    a = _sigmoid(w["a0"][...] + _dot(xwa.astype(BF16), w["aup"][...]))
    s["a"][...] = a
    s["g"][...] = _dot(_sigmoid(m_sm[:, 128:]).astype(BF16), w["gup"][...])
    if vfirst_ref is not None:
        vg = _sigmoid(w["vbias"][...] + _dot(z_sm[:, 128:].astype(BF16), w["vup"][...]))
    refill(zs, w["w_sm"], 0, SM_W)
    yield

    mu = w["mu_rkv"]
    k = shifted(zr, D, 2 * D, mu[:, D:2 * D])
    refill(zr, w["w_rkv"], D, 2 * D)
    kk = k * w["k_k"][...]
    s["kk"][...] = kk * lax.rsqrt(jnp.maximum(_seg_sum(kk * kk, w["seg"]), 1e-24))
    k2 = k * (1.0 + (a - 1.0) * w["k_a"][...])
    s["k"][...] = k2
    yield

    v = shifted(zr, 2 * D, 3 * D, mu[:, 2 * D:])
    refill(zr, w["w_rkv"], 2 * D, 3 * D)
    if vfirst_ref is not None:
        v = v + (vfirst_ref[...] - v) * vg
    s["v"][...] = v
    if vfirst_o is not None:
        vfirst_o[...] = v
    yield

    r = shifted(zr, 0, D, mu[:, :D])
    refill(zr, w["w_rkv"], 0, D)
    s["r"][...] = r
    s["bonus"][...] = _seg_sum(r * k2 * w["r_k"][...], w["seg"]) * v
    yield

    cw = w["convw"]
    for name, lo, scale in (("q", 0, MLSTM_HEAD ** -0.5), ("mk", D_MLSTM, 1.0)):
        hi = lo + D_MLSTM
        conv = w["convb"][:, lo:hi] + cw[CONV_W - 1:CONV_W, lo:hi] * zm[R0:R0 + tt, lo:hi]
        for sft in range(1, CONV_W):
            conv = conv + cw[CONV_W - 1 - sft:CONV_W - sft, lo:hi] * zm[R0 - sft:R0 - sft + tt, lo:hi]
        refill(zm, w["w_ml"], lo, hi)
        s[name][...] = conv * _sigmoid(conv) * scale
        yield
    s["mv"][...] = zm[R0:R0 + tt, 2 * D_MLSTM:3 * D_MLSTM]
    refill(zm, w["w_ml"], 2 * D_MLSTM, 3 * D_MLSTM)
    yield
    s["o"][...] = _sigmoid(zm[R0:R0 + tt, 3 * D_MLSTM:])
    refill(zm, w["w_ml"], 3 * D_MLSTM, 4 * D_MLSTM)
    yield

    for lo in range(0, 2 * D_MODEL, D_MLSTM):
        sg_o[:, lo:lo + D_MLSTM] = _sigmoid(_dot(xb_ref[...], w["w_gate"][:, lo:lo + D_MLSTM]))
        yield


def _rwkv_stages(s, w, y_ref, h_ref):
    C, N = CHUNK, RWKV_HEAD
    tt = y_ref.shape[0]

    tr = lax.broadcasted_iota(jnp.int32, (tt, tt), 0)
    tc = lax.broadcasted_iota(jnp.int32, (tt, tt), 1)
    same = (tr // C) == (tc // C)
    tri = (same & (tc <= tr)).astype(BF16)
    lw = s["ld"][...]
    kk = s["kk"][...]
    kka = kk * s["a"][...]
    k2 = s["k"][...]
    r_in = s["r"][...]
    V = s["v"][...].astype(BF16)
    bonus = s["bonus"][...]
    g_out = s["g"][...]
    parts = _split3(lw)
    cs = sum(_dot(tri, p) for p in parts)
    nc = tt // C
    yield

    E = jnp.exp(cs)
    g_inv = 1.0 / E
    g_tot = jnp.concatenate([jnp.broadcast_to(E[(c + 1) * C - 1:(c + 1) * C, :], (C, D_RWKV))
                             for c in range(nc)], axis=0)
    g_end = g_tot * g_inv
    trow = lax.broadcasted_iota(jnp.int32, (tt, 1), 0)
    E_prev = jnp.where(trow % C == 0, 1.0, pltpu.roll(E, 1, axis=0))
    Rt = (r_in * E).astype(BF16)
    At = (-kk * E_prev).astype(BF16)
    Kb = (k2 * g_inv).astype(BF16)
    Bb = (kka * g_inv).astype(BF16)
    Ke = (k2 * g_end).astype(BF16)
    Be = (kka * g_end).astype(BF16)

    G = RWKV_GROUP
    GW = G * N
    nq = D_RWKV // GW
    lane = lax.broadcasted_iota(jnp.int32, (C, GW), 1)
    row = lax.broadcasted_iota(jnp.int32, (C, GW), 0)
    head_of = [lane // N == h for h in range(G)]
    strict_g = (lane % N) < row
    incl_g = (lane % N) <= row
    eye_g = ((lane % N) == row).astype(F32)
    br = lax.broadcasted_iota(jnp.int32, (GW, GW), 0)
    bc = lax.broadcasted_iota(jnp.int32, (GW, GW), 1)
    bd_mask = (br // N) == (bc // N)
    eye_q = (br == bc).astype(F32)

    def bd(x):
        x = x.astype(BF16)
        return jnp.concatenate([jnp.where(m, x, jnp.zeros_like(x)) for m in head_of], axis=0)

    inst = [(c, q) for c in range(nc) for q in range(nq)]
    sl = lambda x, c, q: x[c * C:(c + 1) * C, q * GW:(q + 1) * GW]
    each = lambda f, *lists: [f(*xs) for xs in zip(*lists)]

    At_i = [sl(At, c, q) for c, q in inst]
    Rt_i = [sl(Rt, c, q) for c, q in inst]
    V_i = [sl(V, c, q) for c, q in inst]
    P = [_bdot_nt(jnp.concatenate([a, r], axis=0),
                  jnp.concatenate([bd(sl(Kb, c, q)), bd(sl(Bb, c, q))], axis=0))
         for a, r, (c, q) in zip(At_i, Rt_i, inst)]
    yield
    A_ak = [jnp.where(strict_g, p[:C, :GW], 0.0).astype(BF16) for p in P]
    L = [jnp.where(strict_g, p[:C, GW:], 0.0).astype(BF16) for p in P]
    A_rk = [jnp.where(incl_g, p[C:, :GW], 0.0).astype(BF16) for p in P]
    A_rb = [jnp.where(incl_g, p[C:, GW:], 0.0).astype(BF16) for p in P]
    X = [eye_g + l for l in L]
    M = each(lambda l: _bdot(l, bd(l)), L)
    AV = each(lambda ak, rk, v: _bdot(jnp.concatenate([ak, rk], axis=0), bd(v)), A_ak, A_rk, V_i)
    yield
    for _ in range(4):
        XM = each(lambda x, m: _bdot(jnp.concatenate([x, m], axis=0), bd(m)), X, M)
        yield
        X = each(lambda x, xm: x + xm[:C], X, XM)
        M = [xm[C:] for xm in XM]
    Tm = each(lambda x, m: x + _bdot(x, bd(m)), X, M)
    yield
    W = each(lambda t, a, av: _bdot(t, jnp.concatenate([bd(a), bd(av[:C])], axis=1)).astype(BF16),
             Tm, At_i, AV)
    yield
    Z = each(lambda rb, w_: _bdot(rb, jnp.concatenate([bd(w_[:, :GW]), bd(w_[:, GW:])], axis=1)), A_rb, W)
    yield
    GA = [_bdot_tn(sl(Be, c, q), w_[:, :GW]) for (c, q), w_ in zip(inst, W)]
    G0 = [jnp.where(bd_mask, _bdot_tn(jnp.concatenate([sl(Ke, c, q), sl(Be, c, q)], axis=0),
                                      jnp.concatenate([v, w_[:, GW:]], axis=0)), 0.0)
          for (c, q), v, w_ in zip(inst, V_i, W)]
    yield
    RM = [jnp.concatenate([(r.astype(F32) + z[:, :GW]).astype(BF16),
                           (jnp.where(bd_mask, ga, 0.0)
                            + eye_q * g_tot[c * C:c * C + 1, q * GW:(q + 1) * GW]).astype(BF16)], axis=0)
          for r, z, ga, (c, q) in zip(Rt_i, Z, GA, inst)]
    Y0 = each(lambda av, z: av[C:] + z[:, GW:], AV, Z)

    Hs = [h_ref[q] for q in range(nq)]
    y_rows = []
    for c in range(nc):
        RMH = [_bdot(RM[c * nq + q], Hs[q]) for q in range(nq)]
        yield
        y_rows.append(jnp.concatenate([RMH[q][:C] + Y0[c * nq + q] for q in range(nq)], axis=1))
        Hs = [RMH[q][C:] + G0[c * nq + q] for q in range(nq)]
    for q in range(nq):
        h_ref[q] = Hs[q]
    y = jnp.concatenate(y_rows, axis=0)
    inv_n = 1.0 / N
    mu = _seg_sum(y, w["seg"]) * inv_n
    yield
    d = y - mu
    var = _seg_sum(d * d, w["seg"]) * inv_n
    yield
    yn = d * lax.rsqrt(var + RWKV_GN_EPS)
    y_ref[...] = (yn * w["gnw"][...] + w["gnb"][...] + bonus) * g_out


def _mlstm_stages(s, w, y_ref, cn_ref, m_ref):
    q_ref, k_ref, v_ref, o_ref, if_ref = s["q"], s["mk"], s["mv"], s["o"], s["ifb"]
    brow_ref, bcol_ref, nw_ref = w["if_brow"], w["if_bcol"], w["ml_nw"]
    C, d = CHUNK, MLSTM_HEAD
    tt = q_ref.shape[0]
    nc = tt // C
    NH = MLSTM_HEADS
    ti = lax.broadcasted_iota(jnp.int32, (C, C), 0)
    si = lax.broadcasted_iota(jnp.int32, (C, C), 1)
    causal = si <= ti
    ones = jnp.ones((C, d), BF16)
    i0 = IF_LANE
    f0 = IF_LANE + NH

    tr = lax.broadcasted_iota(jnp.int32, (tt, tt), 0)
    tc = lax.broadcasted_iota(jnp.int32, (tt, tt), 1)
    same = (tr // C) == (tc // C)
    tril = (same & (tc <= tr)).astype(BF16)
    triu = (same & (tr <= tc)).astype(BF16)
    pre = if_ref[...] + brow_ref[...]
    bcum_cols = sum(_dot(tril, p) for p in _split3(_log_sigmoid(pre)))
    preT = jnp.transpose(if_ref[...])[i0:i0 + 8, :] + bcol_ref[...]
    row8 = lax.broadcasted_iota(jnp.int32, (8, 1), 0)
    bcum_rows = sum(_dot(p, triu) for p in _split3(jnp.where(row8 < NH, preT, _log_sigmoid(preT))))
    inst = [(c, h) for c in range(nc) for h in range(NH)]
    rs = lambda c: slice(c * C, (c + 1) * C)
    hs_ = lambda h: slice(h * d, (h + 1) * d)
    q_i = [q_ref[rs(c), hs_(h)].astype(BF16) for c, h in inst]
    k_i = [k_ref[rs(c), hs_(h)] for c, h in inst]
    ve_i = [jnp.concatenate([v_ref[rs(c), hs_(h)].astype(BF16), ones], axis=1) for c, h in inst]
    o_i = [o_ref[rs(c), hs_(h)] for c, h in inst]
    yield

    bc_i = [bcum_cols[rs(c), f0 + h:f0 + h + 1] for c, h in inst]
    e_c = [pre[rs(c), i0 + h:i0 + h + 1] - bc for (c, h), bc in zip(inst, bc_i)]
    e_r = [preT[h:h + 1, rs(c)] - bcum_rows[NH + h:NH + h + 1, rs(c)] for c, h in inst]
    a_col = [jnp.max(jnp.where(causal, e, -jnp.inf), axis=-1, keepdims=True) for e in e_r]
    yield
    D0 = [jnp.where(causal, jnp.exp(e - a), 0.0) for e, a in zip(e_r, a_col)]
    QK = [_bdot_nt(q, k) for q, k in zip(q_i, k_i)]
    yield
    S0 = [(qk * dm).astype(BF16) for qk, dm in zip(QK, D0)]
    a_end = [a[C - 1:C, :] for a in a_col]
    KV0 = [_bdot_tn(jnp.exp(e - ae) * k, ve) for e, ae, k, ve in zip(e_c, a_end, k_i, ve_i)]
    yield
    intra = [_dot(s0, ve) for s0, ve in zip(S0, ve_i)]
    yield

    cn = [cn_ref[h] for h in range(NH)]
    m_prev = [m_ref[h][0:1, 0:1] for h in range(NH)]
    hh = []
    for c in range(nc):
        ii = [c * NH + h for h in range(NH)]
        QC = [_bdot(q_i[i], cn[h]) for h, i in enumerate(ii)]
        yield
        for h, i in enumerate(ii):
            mu_t = jnp.maximum(m_prev[h], a_col[i])
            tot = jnp.exp(m_prev[h] - mu_t) * QC[h] + jnp.exp(a_col[i] - mu_t) * intra[i]
            hh.append(tot[:, :d] / jnp.maximum(jnp.abs(tot[:, d:]), jnp.exp(-(bc_i[i] + mu_t))))
            mu_end = jnp.maximum(m_prev[h], a_end[i])
            cn[h] = jnp.exp(m_prev[h] - mu_end) * cn[h] + jnp.exp(a_end[i] - mu_end) * KV0[i]
            m_prev[h] = bc_i[i][C - 1:C, :] + mu_end
    for h in range(NH):
        cn_ref[h] = cn[h]
        m_ref[h] = jnp.broadcast_to(m_prev[h], (8, 128))
    mean = [jnp.mean(x, axis=-1, keepdims=True) for x in hh]
    yield
    dd = [x - m for x, m in zip(hh, mean)]
    var = [jnp.mean(x * x, axis=-1, keepdims=True) for x in dd]
    yield
    for (c, h), x, v, o in zip(inst, dd, var, o_i):
        y_ref[rs(c), hs_(h)] = o * (x * lax.rsqrt(v + MLSTM_NORM_EPS) * nw_ref[:, hs_(h)])


def _run_interleaved(streams, head, every):
    for k in head:
        next(streams[k], None)
    live = dict(enumerate(streams))
    rnd = 0
    while live:
        for k in [k for k in live if rnd % every[k] == 0]:
            if next(live[k], StopIteration) is StopIteration:
                del live[k]
        rnd += 1


def _mixer_kernel(*refs, has_vres, nt, wnames, snames):
    refs = list(refs)
    h_ref = refs.pop(0)
    vfirst_ref = refs.pop(0) if has_vres else None
    w = {n: refs.pop(0) for n in wnames}
    ya_o, yb_o, sg_o = refs.pop(0), refs.pop(0), refs.pop(0)
    vfirst_o = None if has_vres else refs.pop(0)
    zr, zs, zm, xb_ref = (refs.pop(0) for _ in range(4))
    s = {n: refs.pop(0) for n in snames}
    h_state, cn_state, m_state = refs
    tt = h_ref.shape[0]
    i = pl.program_id(0)

    @pl.when(i == 0)
    def _():
        for ref in (zr, zs, zm, h_state, cn_state, m_state) + tuple(s.values()):
            ref[...] = jnp.zeros_like(ref)

    @pl.when(lax.rem(i + nt - 1, nt) == 0)
    def _():
        for ref in (zr, zs, zm):
            ref[:STAGE_ROW0, :] = jnp.zeros((STAGE_ROW0, ref.shape[1]), F32)

    @pl.when(lax.rem(i + 2 * nt - 2, nt) == 0)
    def _():
        h_state[...] = jnp.zeros_like(h_state)
        cn_state[...] = jnp.zeros_like(cn_state)
        m_state[...] = jnp.zeros_like(m_state)

    xb_ref[...] = _rms(h_ref[...], w["mix_norm"][...]).astype(BF16)
    streams = [_rwkv_stages(s, w, ya_o, h_state), _mlstm_stages(s, w, yb_o, cn_state, m_state),
               _prep_stages(zr, zs, zm, vfirst_ref, vfirst_o, w, s, xb_ref, sg_o, tt)]
    _run_interleaved(streams, head=(0, 1), every=(1, 3, 2))


def _mixer_call(h3, vfirst, p):
    B, T, _ = h3.shape
    tt = min(TT_MIX, T)
    nt = T // tt
    n = B * nt
    has_vres = vfirst is not None
    cur = lambda wd: pl.BlockSpec((None, tt, wd), lambda i: (jnp.minimum(i, n - 1) // nt,
                                                            jnp.minimum(i, n - 1) % nt, 0))
    def lag(wd, by):
        tile = lambda i: jnp.clip(i - by, 0, n - 1)
        return pl.BlockSpec((None, tt, wd), lambda i: (tile(i) // nt, tile(i) % nt, 0))
    wnames = ("mix_norm", "w_rkv", "w_sm", "w_ml", "w_gate", "mu_rkv", "mu_sm", "w0", "wup", "a0", "aup",
              "gup", "k_k", "k_a", "r_k", "vbias", "vup", "convw", "convb", "seg", "gnw", "gnb",
              "if_brow", "if_bcol", "ml_nw")
    snames = ("r", "k", "v", "ld", "kk", "a", "g", "bonus", "q", "mk", "mv", "o", "ifb")
    weights = [p[nm] for nm in wnames]
    vf_spec = lambda last: pl.BlockSpec((None, tt, D_RWKV), lambda i: (jnp.clip(i - 1, 0, last), 0, 0))
    in_specs = [cur(D_MODEL)] + ([vf_spec(n - 1)] if has_vres else []) + [_const_spec(x.shape) for x in weights]
    out_specs = [lag(D_RWKV, 2), lag(D_MLSTM, 2), cur(2 * D_MODEL)] + ([] if has_vres else [vf_spec(n)])
    out_shape = [jax.ShapeDtypeStruct((B, T, wd), F32) for wd in (D_RWKV, D_MLSTM, 2 * D_MODEL)]
    if not has_vres:
        out_shape.append(jax.ShapeDtypeStruct((n + 1, tt, D_RWKV), F32))
    stage_rows = tt + 32
    scratch = [pltpu.VMEM((stage_rows, 3 * D_RWKV), F32), pltpu.VMEM((stage_rows, SM_W), F32),
               pltpu.VMEM((stage_rows, 4 * D_MLSTM), F32), pltpu.VMEM((tt, D_MODEL), BF16)]
    scratch += [pltpu.VMEM((tt, 128 if nm == "ifb" else D_RWKV), F32) for nm in snames]
    scratch += [pltpu.VMEM((RWKV_HEADS // RWKV_GROUP, RWKV_GROUP * RWKV_HEAD, RWKV_GROUP * RWKV_HEAD), F32),
                pltpu.VMEM((MLSTM_HEADS, MLSTM_HEAD, 2 * MLSTM_HEAD), F32),
                pltpu.VMEM((MLSTM_HEADS, 8, 128), F32)]
    return pl.pallas_call(
        functools.partial(_mixer_kernel, has_vres=has_vres, nt=nt, wnames=wnames, snames=snames),
        grid=(n + 2,),
        in_specs=in_specs,
        out_specs=out_specs,
        out_shape=out_shape,
        scratch_shapes=scratch,
        compiler_params=pltpu.CompilerParams(dimension_semantics=("arbitrary",),
                                             vmem_limit_bytes=VMEM_LIMIT),
        name="mixer",
    )(*([h3] + ([vfirst] if has_vres else []) + weights))


def _pad_rows(w, rows, at=0):
    out = jnp.zeros((rows, w.shape[1]), w.dtype)
    return out.at[at:at + w.shape[0]].set(w)


def _ffn_params(norm, w_in, w_out):
    nc = D_FF // FF_CHUNK
    wg = w_in[:, :D_FF].reshape(D_MODEL, nc, FF_CHUNK).transpose(1, 0, 2).astype(BF16)
    wu = w_in[:, D_FF:].reshape(D_MODEL, nc, FF_CHUNK).transpose(1, 0, 2).astype(BF16)
    wo = w_out.reshape(nc, FF_CHUNK, D_MODEL).astype(BF16)
    return norm.reshape(1, D_MODEL), wg, wu, wo


def _mixer_params(l, mix_norm, w_in, shift_mu, rw_w0, rw_w_up, rw_a0, rw_a_up, rw_g_up, rw_k_k, rw_k_a,
                  rw_r_k, vres_down, vres_up, vres_bias, ml_conv_w, ml_conv_b, ml_i_bias, ml_f_bias):
    w = w_in[l]
    o_ml = N_RW_IN
    o_gate = N_RW_IN + N_ML_IN
    w_rkv = w[:, :3 * D_RWKV]
    w_lora = w[:, 3 * D_RWKV:N_RW_IN]
    w_if = w[:, o_ml + 4 * D_MLSTM:o_gate]
    w_vd = vres_down[l - 1] if l > 0 else jnp.zeros((D_MODEL, LORA_V), F32)
    pad = jnp.zeros((D_MODEL, SM_W - SM_VRES - LORA_V), F32)
    w_sm = jnp.concatenate([w_lora, w_if, w_vd, pad], axis=1)
    mu = shift_mu[l]
    mu_sm = jnp.concatenate([mu[3 * D_RWKV:], jnp.zeros((SM_W - (N_RW_IN - 3 * D_RWKV),), F32)])
    row = lambda x: x.reshape(1, -1)
    p = {
        "mix_norm": row(mix_norm[l]),
        "w_rkv": w_rkv.astype(BF16),
        "w_sm": w_sm.astype(BF16),
        "w_ml": w[:, o_ml:o_ml + 4 * D_MLSTM].astype(BF16),
        "w_gate": w[:, o_gate:].astype(BF16),
        "mu_rkv": row(mu[:3 * D_RWKV]),
        "mu_sm": row(mu_sm),
        "w0": row(rw_w0[l]),
        "wup": _pad_rows(rw_w_up[l], 128, 0).astype(BF16),
        "a0": row(rw_a0[l]),
        "aup": _pad_rows(rw_a_up[l], 128, LORA_W).astype(BF16),
        "gup": _pad_rows(rw_g_up[l], 256, 0).astype(BF16),
        "k_k": row(rw_k_k[l]),
        "k_a": row(rw_k_a[l]),
        "r_k": row(rw_r_k[l]),
        "vbias": row(vres_bias[l - 1]) if l > 0 else jnp.zeros((1, D_RWKV), F32),
        "vup": (_pad_rows(vres_up[l - 1], 256, SM_VRES - 128) if l > 0
                else jnp.zeros((256, D_RWKV), F32)).astype(BF16),
        "convw": ml_conv_w[l],
        "convb": row(ml_conv_b[l]),
    }
    gate_bias = jnp.concatenate([ml_i_bias[l], ml_f_bias[l]])
    p["if_brow"] = jnp.zeros((1, 128), F32).at[0, IF_LANE:IF_LANE + 8].set(gate_bias)
    p["if_bcol"] = gate_bias.reshape(8, 1)
    return p


def kernel(x, ffn1_norm, ffn1_w_in, ffn1_w_out, mix_norm, w_in, shift_mu, rw_w0, rw_w_up, rw_a0, rw_a_up, rw_g_up, rw_k_k, rw_k_a, rw_r_k, rw_gn_w, rw_gn_b, vres_down, vres_up, vres_bias, ml_conv_w, ml_conv_b, ml_i_bias, ml_f_bias, ml_norm_w, br_a, br_b, w_out, ffn2_norm, ffn2_w_in, ffn2_w_out, final_norm):
    B, T, D = x.shape
    n = B * T
    assert D == D_MODEL and T % min(TT_MIX, T) == 0 and min(TT_MIX, T) % CHUNK == 0
    assert n % min(TM_FFN, n) == 0 and n % min(TM_MERGE, n) == 0
    lane = jnp.arange(256) // RWKV_HEAD
    seg = (lane[:, None] == lane[None, :]).astype(BF16)
    fin = final_norm.reshape(1, D_MODEL)
    h = x.reshape(n, D)
    v_first = None
    for l in range(DEPTH):
        h = _ffn_call(h, *_ffn_params(ffn1_norm[l], ffn1_w_in[l], ffn1_w_out[l]))
        p = _mixer_params(l, mix_norm, w_in, shift_mu, rw_w0, rw_w_up, rw_a0, rw_a_up, rw_g_up, rw_k_k,
                          rw_k_a, rw_r_k, vres_down, vres_up, vres_bias, ml_conv_w, ml_conv_b,
                          ml_i_bias, ml_f_bias)
        p.update(seg=seg, gnw=rw_gn_w[l].reshape(1, -1), gnb=rw_gn_b[l].reshape(1, -1),
                 ml_nw=ml_norm_w[l].reshape(1, -1))
        outs = _mixer_call(h.reshape(B, T, D), v_first, p)
        ya, yb, sg = outs[:3]
        if l == 0:
            v_first = outs[3]
        h = _merge_ffn_call(h, ya.reshape(n, -1), yb.reshape(n, -1), sg.reshape(n, -1),
                            br_a[l].astype(BF16), br_b[l].astype(BF16), w_out[l].astype(BF16),
                            *_ffn_params(ffn2_norm[l], ffn2_w_in[l], ffn2_w_out[l]), fin,
                            final=(l == DEPTH - 1))
    return h.reshape(B, T, D)
```

```python
import functools

import jax
import jax.numpy as jnp
from jax import lax
from jax.experimental import pallas as pl
from jax.experimental.pallas import tpu as pltpu

F32 = jnp.float32
BF16 = jnp.bfloat16

D_MODEL = 1024
DEPTH = 4
CHUNK = 64
D_RWKV = 512
RWKV_HEAD = 64
RWKV_HEADS = 8
RWKV_GROUP = 2
LORA_W, LORA_A, LORA_V, LORA_G = 64, 64, 32, 160
RWKV_GN_EPS = 64e-5
D_MLSTM = 512
MLSTM_HEADS = 4
MLSTM_HEAD = 128
CONV_W = 4
MLSTM_NORM_EPS = 1e-5
D_FF = 2816
NORM_EPS = 1e-6
N_RW_IN = 3 * D_RWKV + LORA_W + LORA_A + LORA_G
N_ML_IN = 4 * D_MLSTM + 2 * MLSTM_HEADS

SM_W = 384
SM_IF = 288
SM_VRES = 296
IF_LANE = SM_IF - 256

FF_CHUNK = 256
TM_FFN = 1024
TM_MERGE = 512
TT_MIX = 256
STAGE_ROW0 = 16
VMEM_LIMIT = 56 * 1024 * 1024


def _const_spec(shape):
    nd = len(shape)
    return pl.BlockSpec(shape, lambda *_: (0,) * nd, pipeline_mode=pl.Buffered(1))


def _rms(h, g):
    return h * lax.rsqrt(jnp.mean(h * h, axis=-1, keepdims=True) + NORM_EPS) * g


def _sigmoid(x):
    return 0.5 * jnp.tanh(0.5 * x) + 0.5


def _dot(a, b):
    return jnp.dot(a, b, preferred_element_type=F32)


def _seg_sum(x, seg_ref):
    xb = x.astype(BF16)
    gw = seg_ref.shape[0]
    return jnp.concatenate([_dot(xb[:, lo:lo + gw], seg_ref[...]) for lo in range(0, x.shape[1], gw)], axis=1)


def _ffn_body(h, g_ref, wg_ref, wu_ref, wo_ref, acc_ref):
    xb = _rms(h, g_ref[...]).astype(BF16)
    nc = D_FF // FF_CHUNK
    for c in range(nc):
        gate = _dot(xb, wg_ref[c])
        up = _dot(xb, wu_ref[c])
        act = (gate * _sigmoid(gate) * up).astype(BF16)
        part = _dot(act, wo_ref[c])
        if c == 0:
            acc_ref[...] = part
        elif c < nc - 1:
            acc_ref[...] += part
    return h + 0.5 * (acc_ref[...] + part)


def _ffn_kernel(h_ref, g_ref, wg_ref, wu_ref, wo_ref, out_ref, acc_ref):
    out_ref[...] = _ffn_body(h_ref[...], g_ref, wg_ref, wu_ref, wo_ref, acc_ref)


def _merge_ffn_kernel(h_ref, ya_ref, yb_ref, sg_ref, bra_ref, brb_ref, wout_ref,
                      g_ref, wg_ref, wu_ref, wo_ref, fin_ref, out_ref, acc_ref, *, final):
    sg = sg_ref[...]
    pa = _dot(ya_ref[...].astype(BF16), bra_ref[...])
    pb = _dot(yb_ref[...].astype(BF16), brb_ref[...])
    u = sg[:, :D_MODEL] * pa + sg[:, D_MODEL:] * pb
    h = h_ref[...] + _dot(u.astype(BF16), wout_ref[...])
    h = _ffn_body(h, g_ref, wg_ref, wu_ref, wo_ref, acc_ref)
    if final:
        h = _rms(h, fin_ref[...])
    out_ref[...] = h


def _ffn_weight_specs():
    nc = D_FF // FF_CHUNK
    return [_const_spec((1, D_MODEL)), _const_spec((nc, D_MODEL, FF_CHUNK)),
            _const_spec((nc, D_MODEL, FF_CHUNK)), _const_spec((nc, FF_CHUNK, D_MODEL))]


def _ffn_call(h, g, wg, wu, wo):
    n = h.shape[0]
    tm = min(TM_FFN, n)
    row = pl.BlockSpec((tm, D_MODEL), lambda i: (i, 0))
    return pl.pallas_call(
        _ffn_kernel,
        grid=(n // tm,),
        in_specs=[row] + _ffn_weight_specs(),
        out_specs=row,
        out_shape=jax.ShapeDtypeStruct((n, D_MODEL), F32),
        scratch_shapes=[pltpu.VMEM((tm, D_MODEL), F32)],
        compiler_params=pltpu.CompilerParams(dimension_semantics=("arbitrary",),
                                             vmem_limit_bytes=VMEM_LIMIT),
        name="ffn",
    )(h, g, wg, wu, wo)


def _merge_ffn_call(h, ya, yb, sg, bra, brb, wout, g, wg, wu, wo, fin, final):
    n = h.shape[0]
    tm = min(TM_MERGE, n)
    row = lambda w: pl.BlockSpec((tm, w), lambda i: (i, 0))
    return pl.pallas_call(
        functools.partial(_merge_ffn_kernel, final=final),
        grid=(n // tm,),
        in_specs=[row(D_MODEL), row(D_RWKV), row(D_MLSTM), row(2 * D_MODEL),
                  _const_spec((D_RWKV, D_MODEL)), _const_spec((D_MLSTM, D_MODEL)),
                  _const_spec((D_MODEL, D_MODEL))] + _ffn_weight_specs() + [_const_spec((1, D_MODEL))],
        out_specs=row(D_MODEL),
        out_shape=jax.ShapeDtypeStruct((n, D_MODEL), F32),
        scratch_shapes=[pltpu.VMEM((tm, D_MODEL), F32)],
        compiler_params=pltpu.CompilerParams(dimension_semantics=("arbitrary",),
                                             vmem_limit_bytes=VMEM_LIMIT),
        name="merge_ffn",
    )(h, ya, yb, sg, bra, brb, wout, g, wg, wu, wo, fin)


def _bdot(a, b):
    return jnp.dot(a.astype(BF16), b.astype(BF16), preferred_element_type=F32)


def _bdot_nt(a, b):
    return lax.dot_general(a.astype(BF16), b.astype(BF16), (((1,), (1,)), ((), ())),
                           preferred_element_type=F32)


def _bdot_tn(a, b):
    return lax.dot_general(a.astype(BF16), b.astype(BF16), (((0,), (0,)), ((), ())),
                           preferred_element_type=F32)


def _split3(x):
    hi = x.astype(BF16)
    r1 = x - hi.astype(F32)
    mid = r1.astype(BF16)
    lo = (r1 - mid.astype(F32)).astype(BF16)
    return hi, mid, lo


def _log_sigmoid(x):
    return jnp.minimum(x, 0.0) - jnp.log(1.0 + jnp.exp(-jnp.abs(x)))


def _prep_stages(zr, zs, zm, vfirst_ref, vfirst_o, w, s, xb_ref, sg_o, tt):
    D = D_RWKV
    R0 = STAGE_ROW0

    def shifted(buf, lo, hi, mu):
        z = buf[R0:R0 + tt, lo:hi]
        return z + mu * (buf[R0 - 1:R0 - 1 + tt, lo:hi] - z)

    def refill(buf, w_ref, lo, hi):
        buf[R0 - 8:R0, lo:hi] = buf[R0 + tt - 8:R0 + tt, lo:hi]
        buf[R0:R0 + tt, lo:hi] = _dot(xb_ref[...], w_ref[:, lo:hi])

    z_sm = zs[R0:R0 + tt, :]
    m_sm = shifted(zs, 0, SM_W, w["mu_sm"][...])
    s["ifb"][...] = z_sm[:, 256:]
    xwa = m_sm[:, :128]
    y = -(w["w0"][...] + _dot(jnp.tanh(xwa).astype(BF16), w["wup"][...]))
    softplus = jnp.maximum(y, 0.0) + jnp.log(1.0 + jnp.exp(-jnp.abs(y)))
    s["ld"][...] = -jnp.exp(-softplus - 0.5)
    a = _sigmoid(w["a0"][...] + _dot(xwa.astype(BF16), w["aup"][...]))
    s["a"][...] = a
    s["g"][...] = _dot(_sigmoid(m_sm[:, 128:]).astype(BF16), w["gup"][...])
    if vfirst_ref is not None:
        vg = _sigmoid(w["vbias"][...] + _dot(z_sm[:, 128:].astype(BF16), w["vup"][...]))
    refill(zs, w["w_sm"], 0, SM_W)
    yield

    mu = w["mu_rkv"]
    k = shifted(zr, D, 2 * D, mu[:, D:2 * D])
    refill(zr, w["w_rkv"], D, 2 * D)
    kk = k * w["k_k"][...]
    s["kk"][...] = kk * lax.rsqrt(jnp.maximum(_seg_sum(kk * kk, w["seg"]), 1e-24))
    k2 = k * (1.0 + (a - 1.0) * w["k_a"][...])
    s["k"][...] = k2
    yield

    v = shifted(zr, 2 * D, 3 * D, mu[:, 2 * D:])
    refill(zr, w["w_rkv"], 2 * D, 3 * D)
    if vfirst_ref is not None:
        v = v + (vfirst_ref[...] - v) * vg
    s["v"][...] = v
    if vfirst_o is not None:
        vfirst_o[...] = v
    yield

    r = shifted(zr, 0, D, mu[:, :D])
    refill(zr, w["w_rkv"], 0, D)
    s["r"][...] = r
    s["bonus"][...] = _seg_sum(r * k2 * w["r_k"][...], w["seg"]) * v
    yield

    cw = w["convw"]
    HW = D_MLSTM // 2
    for name, base, scale in (("q", 0, MLSTM_HEAD ** -0.5), ("mk", D_MLSTM, 1.0)):
        for off in (0, HW):
            lo, hi = base + off, base + off + HW
            conv = w["convb"][:, lo:hi] + cw[CONV_W - 1:CONV_W, lo:hi] * zm[R0:R0 + tt, lo:hi]
            for sft in range(1, CONV_W):
                conv = conv + cw[CONV_W - 1 - sft:CONV_W - sft, lo:hi] * zm[R0 - sft:R0 - sft + tt, lo:hi]
            refill(zm, w["w_ml"], lo, hi)
            s[name][:, off:off + HW] = conv * _sigmoid(conv) * scale
            yield
    s["mv"][...] = zm[R0:R0 + tt, 2 * D_MLSTM:3 * D_MLSTM]
    refill(zm, w["w_ml"], 2 * D_MLSTM, 3 * D_MLSTM)
    yield
    s["o"][...] = _sigmoid(zm[R0:R0 + tt, 3 * D_MLSTM:])
    refill(zm, w["w_ml"], 3 * D_MLSTM, 4 * D_MLSTM)
    yield

    for lo in range(0, 2 * D_MODEL, D_MLSTM):
        sg_o[:, lo:lo + D_MLSTM] = _sigmoid(_dot(xb_ref[...], w["w_gate"][:, lo:lo + D_MLSTM]))
        yield


def _rwkv_stages(s, w, y_ref, h_ref):
    C, N = CHUNK, RWKV_HEAD
    tt = y_ref.shape[0]

    tr = lax.broadcasted_iota(jnp.int32, (tt, tt), 0)
    tc = lax.broadcasted_iota(jnp.int32, (tt, tt), 1)
    same = (tr // C) == (tc // C)
    tri = (same & (tc <= tr)).astype(BF16)
    lw = s["ld"][...]
    kk = s["kk"][...]
    kka = kk * s["a"][...]
    k2 = s["k"][...]
    r_in = s["r"][...]
    V = s["v"][...].astype(BF16)
    bonus = s["bonus"][...]
    g_out = s["g"][...]
    parts = _split3(lw)
    cs = sum(_dot(tri, p) for p in parts)
    nc = tt // C
    yield

    E = jnp.exp(cs)
    g_inv = 1.0 / E
    g_tot = jnp.concatenate([jnp.broadcast_to(E[(c + 1) * C - 1:(c + 1) * C, :], (C, D_RWKV))
                             for c in range(nc)], axis=0)
    g_end = g_tot * g_inv
    trow = lax.broadcasted_iota(jnp.int32, (tt, 1), 0)
    E_prev = jnp.where(trow % C == 0, 1.0, pltpu.roll(E, 1, axis=0))
    Rt = (r_in * E).astype(BF16)
    At = (-kk * E_prev).astype(BF16)
    Kb = (k2 * g_inv).astype(BF16)
    Bb = (kka * g_inv).astype(BF16)
    Ke = (k2 * g_end).astype(BF16)
    Be = (kka * g_end).astype(BF16)

    G = RWKV_GROUP
    GW = G * N
    nq = D_RWKV // GW
    lane = lax.broadcasted_iota(jnp.int32, (C, GW), 1)
    row = lax.broadcasted_iota(jnp.int32, (C, GW), 0)
    head_of = [lane // N == h for h in range(G)]
    strict_g = (lane % N) < row
    incl_g = (lane % N) <= row
    eye_g = ((lane % N) == row).astype(F32)
    br = lax.broadcasted_iota(jnp.int32, (GW, GW), 0)
    bc = lax.broadcasted_iota(jnp.int32, (GW, GW), 1)
    bd_mask = (br // N) == (bc // N)
    eye_q = (br == bc).astype(F32)

    def bd(x):
        x = x.astype(BF16)
        return jnp.concatenate([jnp.where(m, x, jnp.zeros_like(x)) for m in head_of], axis=0)

    inst = [(c, q) for c in range(nc) for q in range(nq)]
    sl = lambda x, c, q: x[c * C:(c + 1) * C, q * GW:(q + 1) * GW]
    each = lambda f, *lists: [f(*xs) for xs in zip(*lists)]

    At_i = [sl(At, c, q) for c, q in inst]
    Rt_i = [sl(Rt, c, q) for c, q in inst]
    V_i = [sl(V, c, q) for c, q in inst]
    P = [_bdot_nt(jnp.concatenate([a, r], axis=0),
                  jnp.concatenate([bd(sl(Kb, c, q)), bd(sl(Bb, c, q))], axis=0))
         for a, r, (c, q) in zip(At_i, Rt_i, inst)]
    yield
    A_ak = [jnp.where(strict_g, p[:C, :GW], 0.0).astype(BF16) for p in P]
    L = [jnp.where(strict_g, p[:C, GW:], 0.0).astype(BF16) for p in P]
    A_rk = [jnp.where(incl_g, p[C:, :GW], 0.0).astype(BF16) for p in P]
    A_rb = [jnp.where(incl_g, p[C:, GW:], 0.0).astype(BF16) for p in P]
    X = [eye_g + l for l in L]
    M = each(lambda l: _bdot(l, bd(l)), L)
    AV = each(lambda ak, rk, v: _bdot(jnp.concatenate([ak, rk], axis=0), bd(v)), A_ak, A_rk, V_i)
    yield
    for _ in range(4):
        XM = each(lambda x, m: _bdot(jnp.concatenate([x, m], axis=0), bd(m)), X, M)
        yield
        X = each(lambda x, xm: x + xm[:C], X, XM)
        M = [xm[C:] for xm in XM]
    Tm = each(lambda x, m: x + _bdot(x, bd(m)), X, M)
    yield
    W = each(lambda t, a, av: _bdot(t, jnp.concatenate([bd(a), bd(av[:C])], axis=1)).astype(BF16),
             Tm, At_i, AV)
    yield
    Z = each(lambda rb, w_: _bdot(rb, jnp.concatenate([bd(w_[:, :GW]), bd(w_[:, GW:])], axis=1)), A_rb, W)
    yield
    GA = [_bdot_tn(sl(Be, c, q), w_[:, :GW]) for (c, q), w_ in zip(inst, W)]
    G0 = [jnp.where(bd_mask, _bdot_tn(jnp.concatenate([sl(Ke, c, q), sl(Be, c, q)], axis=0),
                                      jnp.concatenate([v, w_[:, GW:]], axis=0)), 0.0)
          for (c, q), v, w_ in zip(inst, V_i, W)]
    yield
    RM = [jnp.concatenate([(r.astype(F32) + z[:, :GW]).astype(BF16),
                           (jnp.where(bd_mask, ga, 0.0)
                            + eye_q * g_tot[c * C:c * C + 1, q * GW:(q + 1) * GW]).astype(BF16)], axis=0)
          for r, z, ga, (c, q) in zip(Rt_i, Z, GA, inst)]
    Y0 = each(lambda av, z: av[C:] + z[:, GW:], AV, Z)

    Hs = [h_ref[q] for q in range(nq)]
    y_rows = []
    for c in range(nc):
        RMH = [_bdot(RM[c * nq + q], Hs[q]) for q in range(nq)]
        yield
        y_rows.append(jnp.concatenate([RMH[q][:C] + Y0[c * nq + q] for q in range(nq)], axis=1))
        Hs = [RMH[q][C:] + G0[c * nq + q] for q in range(nq)]
    for q in range(nq):
        h_ref[q] = Hs[q]
    y = jnp.concatenate(y_rows, axis=0)
    inv_n = 1.0 / N
    mu = _seg_sum(y, w["seg"]) * inv_n
    yield
    d = y - mu
    var = _seg_sum(d * d, w["seg"]) * inv_n
    yield
    yn = d * lax.rsqrt(var + RWKV_GN_EPS)
    y_ref[...] = (yn * w["gnw"][...] + w["gnb"][...] + bonus) * g_out


def _mlstm_stages(s, w, y_ref, cn_ref, m_ref):
    q_ref, k_ref, v_ref, o_ref, if_ref = s["q"], s["mk"], s["mv"], s["o"], s["ifb"]
    brow_ref, bcol_ref, nw_ref = w["if_brow"], w["if_bcol"], w["ml_nw"]
    C, d = CHUNK, MLSTM_HEAD
    tt = q_ref.shape[0]
    nc = tt // C
    NH = MLSTM_HEADS
    ti = lax.broadcasted_iota(jnp.int32, (C, C), 0)
    si = lax.broadcasted_iota(jnp.int32, (C, C), 1)
    causal = si <= ti
    ones = jnp.ones((C, d), BF16)
    i0 = IF_LANE
    f0 = IF_LANE + NH

    tr = lax.broadcasted_iota(jnp.int32, (tt, tt), 0)
    tc = lax.broadcasted_iota(jnp.int32, (tt, tt), 1)
    same = (tr // C) == (tc // C)
    tril = (same & (tc <= tr)).astype(BF16)
    triu = (same & (tr <= tc)).astype(BF16)
    pre = if_ref[...] + brow_ref[...]
    bcum_cols = sum(_dot(tril, p) for p in _split3(_log_sigmoid(pre)))
    preT = jnp.transpose(if_ref[...])[i0:i0 + 8, :] + bcol_ref[...]
    row8 = lax.broadcasted_iota(jnp.int32, (8, 1), 0)
    bcum_rows = sum(_dot(p, triu) for p in _split3(jnp.where(row8 < NH, preT, _log_sigmoid(preT))))
    inst = [(c, h) for c in range(nc) for h in range(NH)]
    rs = lambda c: slice(c * C, (c + 1) * C)
    hs_ = lambda h: slice(h * d, (h + 1) * d)
    q_i = [q_ref[rs(c), hs_(h)].astype(BF16) for c, h in inst]
    k_i = [k_ref[rs(c), hs_(h)] for c, h in inst]
    ve_i = [jnp.concatenate([v_ref[rs(c), hs_(h)].astype(BF16), ones], axis=1) for c, h in inst]
    o_i = [o_ref[rs(c), hs_(h)] for c, h in inst]
    yield

    bc_i = [bcum_cols[rs(c), f0 + h:f0 + h + 1] for c, h in inst]
    e_c = [pre[rs(c), i0 + h:i0 + h + 1] - bc for (c, h), bc in zip(inst, bc_i)]
    e_r = [preT[h:h + 1, rs(c)] - bcum_rows[NH + h:NH + h + 1, rs(c)] for c, h in inst]
    a_col = [jnp.max(jnp.where(causal, e, -jnp.inf), axis=-1, keepdims=True) for e in e_r]
    yield
    D0 = [jnp.where(causal, jnp.exp(e - a), 0.0) for e, a in zip(e_r, a_col)]
    QK = [_bdot_nt(q, k) for q, k in zip(q_i, k_i)]
    yield
    S0 = [(qk * dm).astype(BF16) for qk, dm in zip(QK, D0)]
    a_end = [a[C - 1:C, :] for a in a_col]
    KV0 = [_bdot_tn(jnp.exp(e - ae) * k, ve) for e, ae, k, ve in zip(e_c, a_end, k_i, ve_i)]
    yield
    intra = [_dot(s0, ve) for s0, ve in zip(S0, ve_i)]
    yield

    cn = [cn_ref[h] for h in range(NH)]
    m_prev = [m_ref[h][0:1, 0:1] for h in range(NH)]
    hh = []
    for c in range(nc):
        ii = [c * NH + h for h in range(NH)]
        QC = [_bdot(q_i[i], cn[h]) for h, i in enumerate(ii)]
        yield
        for h, i in enumerate(ii):
            mu_t = jnp.maximum(m_prev[h], a_col[i])
            tot = jnp.exp(m_prev[h] - mu_t) * QC[h] + jnp.exp(a_col[i] - mu_t) * intra[i]
            hh.append(tot[:, :d] / jnp.maximum(jnp.abs(tot[:, d:]), jnp.exp(-(bc_i[i] + mu_t))))
            mu_end = jnp.maximum(m_prev[h], a_end[i])
            cn[h] = jnp.exp(m_prev[h] - mu_end) * cn[h] + jnp.exp(a_end[i] - mu_end) * KV0[i]
            m_prev[h] = bc_i[i][C - 1:C, :] + mu_end
    for h in range(NH):
        cn_ref[h] = cn[h]
        m_ref[h] = jnp.broadcast_to(m_prev[h], (8, 128))
    mean = [jnp.mean(x, axis=-1, keepdims=True) for x in hh]
    yield
    dd = [x - m for x, m in zip(hh, mean)]
    var = [jnp.mean(x * x, axis=-1, keepdims=True) for x in dd]
    yield
    for (c, h), x, v, o in zip(inst, dd, var, o_i):
        y_ref[rs(c), hs_(h)] = o * (x * lax.rsqrt(v + MLSTM_NORM_EPS) * nw_ref[:, hs_(h)])


def _run_interleaved(streams, head, every):
    for k in head:
        next(streams[k], None)
    live = dict(enumerate(streams))
    rnd = 0
    while live:
        for k in [k for k in live if rnd % every[k] == 0]:
            if next(live[k], StopIteration) is StopIteration:
                del live[k]
        rnd += 1


def _mixer_kernel(*refs, has_vres, nt, wnames, snames):
    refs = list(refs)
    h_ref = refs.pop(0)
    vfirst_ref = refs.pop(0) if has_vres else None
    w = {n: refs.pop(0) for n in wnames}
    ya_o, yb_o, sg_o = refs.pop(0), refs.pop(0), refs.pop(0)
    vfirst_o = None if has_vres else refs.pop(0)
    zr, zs, zm, xb_ref = (refs.pop(0) for _ in range(4))
    s = {n: refs.pop(0) for n in snames}
    h_state, cn_state, m_state = refs
    tt = h_ref.shape[0]
    i = pl.program_id(0)

    @pl.when(i == 0)
    def _():
        for ref in (zr, zs, zm, h_state, cn_state, m_state) + tuple(s.values()):
            ref[...] = jnp.zeros_like(ref)

    @pl.when(lax.rem(i + nt - 1, nt) == 0)
    def _():
        for ref in (zr, zs, zm):
            ref[:STAGE_ROW0, :] = jnp.zeros((STAGE_ROW0, ref.shape[1]), F32)

    @pl.when(lax.rem(i + 2 * nt - 2, nt) == 0)
    def _():
        h_state[...] = jnp.zeros_like(h_state)
        cn_state[...] = jnp.zeros_like(cn_state)
        m_state[...] = jnp.zeros_like(m_state)

    xb_ref[...] = _rms(h_ref[...], w["mix_norm"][...]).astype(BF16)
    streams = [_rwkv_stages(s, w, ya_o, h_state), _mlstm_stages(s, w, yb_o, cn_state, m_state),
               _prep_stages(zr, zs, zm, vfirst_ref, vfirst_o, w, s, xb_ref, sg_o, tt)]
    _run_interleaved(streams, head=(0, 1), every=(1, 3, 2))


def _mixer_call(h3, vfirst, p):
    B, T, _ = h3.shape
    tt = min(TT_MIX, T)
    nt = T // tt
    n = B * nt
    has_vres = vfirst is not None
    cur = lambda wd: pl.BlockSpec((None, tt, wd), lambda i: (jnp.minimum(i, n - 1) // nt,
                                                            jnp.minimum(i, n - 1) % nt, 0))
    def lag(wd, by):
        tile = lambda i: jnp.clip(i - by, 0, n - 1)
        return pl.BlockSpec((None, tt, wd), lambda i: (tile(i) // nt, tile(i) % nt, 0))
    wnames = ("mix_norm", "w_rkv", "w_sm", "w_ml", "w_gate", "mu_rkv", "mu_sm", "w0", "wup", "a0", "aup",
              "gup", "k_k", "k_a", "r_k", "vbias", "vup", "convw", "convb", "seg", "gnw", "gnb",
              "if_brow", "if_bcol", "ml_nw")
    snames = ("r", "k", "v", "ld", "kk", "a", "g", "bonus", "q", "mk", "mv", "o", "ifb")
    weights = [p[nm] for nm in wnames]
    vf_spec = lambda last: pl.BlockSpec((None, tt, D_RWKV), lambda i: (jnp.clip(i - 1, 0, last), 0, 0))
    in_specs = [cur(D_MODEL)] + ([vf_spec(n - 1)] if has_vres else []) + [_const_spec(x.shape) for x in weights]
    out_specs = [lag(D_RWKV, 2), lag(D_MLSTM, 2), cur(2 * D_MODEL)] + ([] if has_vres else [vf_spec(n)])
    out_shape = [jax.ShapeDtypeStruct((B, T, wd), F32) for wd in (D_RWKV, D_MLSTM, 2 * D_MODEL)]
    if not has_vres:
        out_shape.append(jax.ShapeDtypeStruct((n + 1, tt, D_RWKV), F32))
    stage_rows = tt + 32
    scratch = [pltpu.VMEM((stage_rows, 3 * D_RWKV), F32), pltpu.VMEM((stage_rows, SM_W), F32),
               pltpu.VMEM((stage_rows, 4 * D_MLSTM), F32), pltpu.VMEM((tt, D_MODEL), BF16)]
    scratch += [pltpu.VMEM((tt, 128 if nm == "ifb" else D_RWKV), F32) for nm in snames]
    scratch += [pltpu.VMEM((RWKV_HEADS // RWKV_GROUP, RWKV_GROUP * RWKV_HEAD, RWKV_GROUP * RWKV_HEAD), F32),
                pltpu.VMEM((MLSTM_HEADS, MLSTM_HEAD, 2 * MLSTM_HEAD), F32),
                pltpu.VMEM((MLSTM_HEADS, 8, 128), F32)]
    return pl.pallas_call(
        functools.partial(_mixer_kernel, has_vres=has_vres, nt=nt, wnames=wnames, snames=snames),
        grid=(n + 2,),
        in_specs=in_specs,
        out_specs=out_specs,
        out_shape=out_shape,
        scratch_shapes=scratch,
        compiler_params=pltpu.CompilerParams(dimension_semantics=("arbitrary",),
                                             vmem_limit_bytes=VMEM_LIMIT),
        name="mixer",
    )(*([h3] + ([vfirst] if has_vres else []) + weights))


def _pad_rows(w, rows, at=0):
    out = jnp.zeros((rows, w.shape[1]), w.dtype)
    return out.at[at:at + w.shape[0]].set(w)


def _ffn_params(norm, w_in, w_out):
    nc = D_FF // FF_CHUNK
    wg = w_in[:, :D_FF].reshape(D_MODEL, nc, FF_CHUNK).transpose(1, 0, 2).astype(BF16)
    wu = w_in[:, D_FF:].reshape(D_MODEL, nc, FF_CHUNK).transpose(1, 0, 2).astype(BF16)
    wo = w_out.reshape(nc, FF_CHUNK, D_MODEL).astype(BF16)
    return norm.reshape(1, D_MODEL), wg, wu, wo


def _mixer_params(l, mix_norm, w_in, shift_mu, rw_w0, rw_w_up, rw_a0, rw_a_up, rw_g_up, rw_k_k, rw_k_a,
                  rw_r_k, vres_down, vres_up, vres_bias, ml_conv_w, ml_conv_b, ml_i_bias, ml_f_bias):
    w = w_in[l]
    o_ml = N_RW_IN
    o_gate = N_RW_IN + N_ML_IN
    w_rkv = w[:, :3 * D_RWKV]
    w_lora = w[:, 3 * D_RWKV:N_RW_IN]
    w_if = w[:, o_ml + 4 * D_MLSTM:o_gate]
    w_vd = vres_down[l - 1] if l > 0 else jnp.zeros((D_MODEL, LORA_V), F32)
    pad = jnp.zeros((D_MODEL, SM_W - SM_VRES - LORA_V), F32)
    w_sm = jnp.concatenate([w_lora, w_if, w_vd, pad], axis=1)
    mu = shift_mu[l]
    mu_sm = jnp.concatenate([mu[3 * D_RWKV:], jnp.zeros((SM_W - (N_RW_IN - 3 * D_RWKV),), F32)])
    row = lambda x: x.reshape(1, -1)
    p = {
        "mix_norm": row(mix_norm[l]),
        "w_rkv": w_rkv.astype(BF16),
        "w_sm": w_sm.astype(BF16),
        "w_ml": w[:, o_ml:o_ml + 4 * D_MLSTM].astype(BF16),
        "w_gate": w[:, o_gate:].astype(BF16),
        "mu_rkv": row(mu[:3 * D_RWKV]),
        "mu_sm": row(mu_sm),
        "w0": row(rw_w0[l]),
        "wup": _pad_rows(rw_w_up[l], 128, 0).astype(BF16),
        "a0": row(rw_a0[l]),
        "aup": _pad_rows(rw_a_up[l], 128, LORA_W).astype(BF16),
        "gup": _pad_rows(rw_g_up[l], 256, 0).astype(BF16),
        "k_k": row(rw_k_k[l]),
        "k_a": row(rw_k_a[l]),
        "r_k": row(rw_r_k[l]),
        "vbias": row(vres_bias[l - 1]) if l > 0 else jnp.zeros((1, D_RWKV), F32),
        "vup": (_pad_rows(vres_up[l - 1], 256, SM_VRES - 128) if l > 0
                else jnp.zeros((256, D_RWKV), F32)).astype(BF16),
        "convw": ml_conv_w[l],
        "convb": row(ml_conv_b[l]),
    }
    gate_bias = jnp.concatenate([ml_i_bias[l], ml_f_bias[l]])
    p["if_brow"] = jnp.zeros((1, 128), F32).at[0, IF_LANE:IF_LANE + 8].set(gate_bias)
    p["if_bcol"] = gate_bias.reshape(8, 1)
    return p


def kernel(x, ffn1_norm, ffn1_w_in, ffn1_w_out, mix_norm, w_in, shift_mu, rw_w0, rw_w_up, rw_a0, rw_a_up, rw_g_up, rw_k_k, rw_k_a, rw_r_k, rw_gn_w, rw_gn_b, vres_down, vres_up, vres_bias, ml_conv_w, ml_conv_b, ml_i_bias, ml_f_bias, ml_norm_w, br_a, br_b, w_out, ffn2_norm, ffn2_w_in, ffn2_w_out, final_norm):
    B, T, D = x.shape
    n = B * T
    assert D == D_MODEL and T % min(TT_MIX, T) == 0 and min(TT_MIX, T) % CHUNK == 0
    assert n % min(TM_FFN, n) == 0 and n % min(TM_MERGE, n) == 0
    lane = jnp.arange(256) // RWKV_HEAD
    seg = (lane[:, None] == lane[None, :]).astype(BF16)
    fin = final_norm.reshape(1, D_MODEL)
    h = x.reshape(n, D)
    v_first = None
    for l in range(DEPTH):
        h = _ffn_call(h, *_ffn_params(ffn1_norm[l], ffn1_w_in[l], ffn1_w_out[l]))
        p = _mixer_params(l, mix_norm, w_in, shift_mu, rw_w0, rw_w_up, rw_a0, rw_a_up, rw_g_up, rw_k_k,
                          rw_k_a, rw_r_k, vres_down, vres_up, vres_bias, ml_conv_w, ml_conv_b,
                          ml_i_bias, ml_f_bias)
        p.update(seg=seg, gnw=rw_gn_w[l].reshape(1, -1), gnb=rw_gn_b[l].reshape(1, -1),
                 ml_nw=ml_norm_w[l].reshape(1, -1))
        outs = _mixer_call(h.reshape(B, T, D), v_first, p)
        ya, yb, sg = outs[:3]
        if l == 0:
            v_first = outs[3]
        h = _merge_ffn_call(h, ya.reshape(n, -1), yb.reshape(n, -1), sg.reshape(n, -1),
                            br_a[l].astype(BF16), br_b[l].astype(BF16), w_out[l].astype(BF16),
                            *_ffn_params(ffn2_norm[l], ffn2_w_in[l], ffn2_w_out[l]), fin,
                            final=(l == DEPTH - 1))
    return h.reshape(B, T, D)
```
